```python
import math
import jax
import jax.numpy as jnp
from jax import lax
import numpy as np

D_MODEL = 1024
BATCH = 8
SEQ = 2048
DEPTH = 4

CHUNK = 64
MAX_STREAM_CHUNKS = 64
POOL_WINDOWS = (2, 4, 8, 16)
POOL_GROUPS = len(POOL_WINDOWS)
D_POOL = D_MODEL // 2
POOL_GW = D_POOL // POOL_GROUPS
D_CONV = D_MODEL // 2
CONV_WIDTH = 31
N_HEADS = 8
QK_NOPE = 64
QK_ROPE = 32
V_HEAD = 64
Q_LORA = 384
KV_LORA = 256
ROPE_THETA = 10000.0
Q_BLOCK = 128
N_BRANCH = 3
OFF_CONV = D_POOL
OFF_Q = OFF_CONV + 2 * D_CONV
OFF_KV = OFF_Q + Q_LORA
OFF_KR = OFF_KV + KV_LORA
OFF_GATE = OFF_KR + QK_ROPE
D_IN = OFF_GATE + N_BRANCH * D_MODEL
IN_SPLITS = (OFF_CONV, OFF_Q, OFF_KV, OFF_KR, OFF_GATE)
N_EXPERTS = 16
N_GROUPS = 4
EXPERTS_PER_GROUP = N_EXPERTS // N_GROUPS
TOPK_GROUPS = 1
TOP_K = 2
D_EXPERT = 512
MOE_BLOCK = 128
D_PLE = 256
DEEPNORM_ALPHA = (2 * DEPTH) ** 0.25
DEEPNORM_BETA = (8 * DEPTH) ** -0.25
LN_EPS = 1e-5
RMS_EPS = 1e-6

kernel_name = 'hybrid_pool_conv_mla_moe_deepnorm'


def layer_norm(x, g, b):
    xf = x.astype(jnp.float32)
    mu = jnp.mean(xf, axis=-1, keepdims=True)
    xc = xf - mu
    var = jnp.mean(xc * xc, axis=-1, keepdims=True)
    return (xc * lax.rsqrt(var + LN_EPS) * g.astype(jnp.float32) + b.astype(jnp.float32)).astype(x.dtype)


def rms_norm(x, g):
    xf = x.astype(jnp.float32)
    ms = jnp.mean(xf * xf, axis=-1, keepdims=True)
    return (xf * lax.rsqrt(ms + RMS_EPS) * g.astype(jnp.float32)).astype(x.dtype)


def rope_tables(positions):
    inv_freq = jnp.power(ROPE_THETA, -jnp.arange(0, QK_ROPE, 2, dtype=jnp.float32) / QK_ROPE)
    ang = positions.astype(jnp.float32)[..., None] * inv_freq
    return jnp.cos(ang), jnp.sin(ang)


def apply_rope(x, cos, sin):
    cos = cos.astype(x.dtype)
    sin = sin.astype(x.dtype)
    x1, x2 = jnp.split(x, 2, axis=-1)
    return jnp.concatenate([x1 * cos - x2 * sin, x2 * cos + x1 * sin], axis=-1)


def pool_mixer(u, w_grp, scale):
    B, S, C = u.shape
    uf = u.astype(jnp.float32)
    cs = jnp.cumsum(uf, axis=1)
    t = jnp.arange(S)
    means = []
    for g, w in enumerate(POOL_WINDOWS):
        csg = cs[..., g * POOL_GW:(g + 1) * POOL_GW]
        lag = jnp.pad(csg[:, :S - w], ((0, 0), (w, 0), (0, 0)))
        cnt = jnp.minimum(t + 1, w).astype(jnp.float32)[None, :, None]
        means.append((csg - lag) / cnt)
    mixed = (jnp.concatenate(means, axis=-1) - uf).astype(u.dtype)
    mixed = jnp.einsum('bsgc,gcd->bsgd', mixed.reshape(B, S, POOL_GROUPS, POOL_GW), w_grp)
    return mixed.reshape(B, S, C) * scale


def conv_module(u2, dw, db, g, b):
    a, gate = jnp.split(u2, 2, axis=-1)
    z = a * jax.nn.sigmoid(gate)
    z = lax.conv_general_dilated(z, dw[:, None, :], window_strides=(1,),
                                 padding=[(CONV_WIDTH - 1, 0)],
                                 dimension_numbers=('NWC', 'WIO', 'NWC'),
                                 feature_group_count=D_CONV) + db
    return jax.nn.silu(layer_norm(z, g, b))


def chunk_causal_attention(q, k, v):
    B, S, H, Dk = q.shape
    nq = S // Q_BLOCK
    scale = Dk ** -0.5
    qb = q.reshape(B, nq, Q_BLOCK, H, Dk).transpose(1, 0, 2, 3, 4)
    key_chunk = jnp.arange(S) // CHUNK

    def block(args):
        qi, i = args
        s = jnp.einsum('bqhd,bkhd->bhqk', qi, k, preferred_element_type=jnp.float32) * scale
        q_chunk = (i * Q_BLOCK + jnp.arange(Q_BLOCK)) // CHUNK
        mask = key_chunk[None, :] <= q_chunk[:, None]
        pr = jax.nn.softmax(jnp.where(mask, s, -jnp.inf), axis=-1).astype(v.dtype)
        return jnp.einsum('bhqk,bkhd->bqhd', pr, v)

    o = lax.map(block, (qb, jnp.arange(nq)))
    return o.transpose(1, 0, 2, 3, 4).reshape(B, S, H, v.shape[-1])


def mla(c_q, c_kv, k_r, cos, sin, q_norm_g, w_uq, kv_norm_g, w_ukv):
    B, S, _ = c_q.shape
    q = jnp.matmul(rms_norm(c_q, q_norm_g), w_uq).reshape(B, S, N_HEADS, QK_NOPE + QK_ROPE)
    q_nope, q_rope = q[..., :QK_NOPE], q[..., QK_NOPE:]
    q_rope = apply_rope(q_rope, cos[:, :, None, :], sin[:, :, None, :])
    kv = jnp.matmul(rms_norm(c_kv, kv_norm_g), w_ukv).reshape(B, S, N_HEADS, QK_NOPE + V_HEAD)
    k_nope, v = kv[..., :QK_NOPE], kv[..., QK_NOPE:]
    k_rope = apply_rope(k_r, cos, sin)
    qf = jnp.concatenate([q_nope, q_rope], axis=-1)
    kf = jnp.concatenate([k_nope, jnp.broadcast_to(k_rope[:, :, None, :], (B, S, N_HEADS, QK_ROPE))], axis=-1)
    o = chunk_causal_attention(qf, kf, v)
    return o.reshape(B, S, N_HEADS * V_HEAD)


def mixer_sublayer(h, cos, sin, w_in, b_gate, pool_w, pool_scale, pool_proj, conv_dw, conv_b,
                   conv_ln_g, conv_ln_b, conv_proj, q_norm_g, w_uq, kv_norm_g, w_ukv, mla_proj, w_out):
    B, S, D = h.shape
    proj = jnp.matmul(h, w_in)
    u_pool, u_conv, c_q, c_kv, k_r, g_logits = jnp.split(proj, IN_SPLITS, axis=-1)
    gates = jax.nn.sigmoid(g_logits.reshape(B, S, N_BRANCH, D) + b_gate)
    y_pool = jnp.matmul(pool_mixer(u_pool, pool_w, pool_scale), pool_proj)
    y_conv = jnp.matmul(conv_module(u_conv, conv_dw, conv_b, conv_ln_g, conv_ln_b), conv_proj)
    y_mla = jnp.matmul(mla(c_q, c_kv, k_r, cos, sin, q_norm_g, w_uq, kv_norm_g, w_ukv), mla_proj)
    merged = gates[:, :, 0] * y_pool + gates[:, :, 1] * y_conv + gates[:, :, 2] * y_mla
    return jnp.matmul(merged, w_out)


def moe_ffn(h, w_router, router_bias, w_up, w_down):
    B, S, D = h.shape
    T = B * S
    A = T * TOP_K
    xt = h.reshape(T, D)
    aff = jax.nn.sigmoid(jnp.matmul(xt, w_router).astype(jnp.float32))
    sel = aff + router_bias.astype(jnp.float32)
    grp_score = jnp.sum(lax.top_k(sel.reshape(T, N_GROUPS, EXPERTS_PER_GROUP), TOP_K)[0], axis=-1)
    _, g_idx = lax.top_k(grp_score, TOPK_GROUPS)
    g_keep = jnp.any(jnp.arange(N_GROUPS)[None, None, :] == g_idx[:, :, None], axis=1)
    e_keep = jnp.repeat(g_keep, EXPERTS_PER_GROUP, axis=-1)
    _, e_idx = lax.top_k(jnp.where(e_keep, sel, -jnp.inf), TOP_K)
    w = jnp.take_along_axis(aff, e_idx, axis=-1)
    w = w / jnp.sum(w, axis=-1, keepdims=True)
    n_slots = ((A + N_EXPERTS * MOE_BLOCK + MOE_BLOCK - 1) // MOE_BLOCK) * MOE_BLOCK
    n_blocks = n_slots // MOE_BLOCK
    flat_e = e_idx.reshape(A)
    order = jnp.argsort(flat_e)
    s_e = flat_e[order]
    s_tok = (order // TOP_K).astype(jnp.int32)
    s_w = w.reshape(A)[order]
    counts = jnp.bincount(flat_e, length=N_EXPERTS)
    start = jnp.cumsum(counts) - counts
    padded = (counts + MOE_BLOCK - 1) // MOE_BLOCK * MOE_BLOCK
    pad_end = jnp.cumsum(padded)
    pad_start = pad_end - padded
    dest = pad_start[s_e] + jnp.arange(A) - start[s_e]
    slot_tok = jnp.full((n_slots,), T, jnp.int32).at[dest].set(s_tok)
    slot_w = jnp.zeros((n_slots,), jnp.float32).at[dest].set(s_w)
    blk_e = jnp.minimum(jnp.searchsorted(pad_end, jnp.arange(n_blocks) * MOE_BLOCK, side='right'),
                        N_EXPERTS - 1)
    xs = jnp.concatenate([xt, jnp.zeros((1, D), xt.dtype)], axis=0)[slot_tok]
    xs = xs.reshape(n_blocks, MOE_BLOCK, D)

    def expert_block(args):
        xb, e = args
        gate, up = jnp.split(jnp.matmul(xb, w_up[e]), 2, axis=-1)
        return jnp.matmul(jax.nn.silu(gate) * up, w_down[e])

    ys = lax.map(expert_block, (xs, blk_e)).reshape(n_slots, D)
    out = jnp.zeros((T + 1, D), h.dtype).at[slot_tok].add(ys * slot_w[:, None].astype(ys.dtype))
    return out[:T].reshape(B, S, D)


def setup_inputs(seed: int = 0) -> dict:
    key = jax.random.key(seed)
    ks = iter(jax.random.split(key, 40))
    f32 = jnp.float32
    L, D = DEPTH, D_MODEL

    def dense(shape, fan_in, gain=1.0):
        return jax.random.normal(next(ks), shape, f32) * (gain * fan_in ** -0.5)

    def norm_gain(shape):
        return 1.0 + 0.05 * jax.random.normal(next(ks), shape, f32)

    def small(shape, s=0.02):
        return s * jax.random.normal(next(ks), shape, f32)

    x = jax.random.normal(next(ks), (BATCH, SEQ, D), f32)
    p = jax.random.normal(next(ks), (L, BATCH, SEQ, D_PLE), f32)
    offset = jax.random.randint(next(ks), (BATCH, 1), 0, MAX_STREAM_CHUNKS) * CHUNK
    positions = (offset + jnp.arange(SEQ, dtype=jnp.int32)[None, :]).astype(jnp.int32)
    return {
        'x': x,
        'p': p,
        'positions': positions,
        'ln_in_g': norm_gain((D,)),
        'ln_in_b': small((D,)),
        'w_in': dense((L, D, D_IN), D),
        'b_gate': small((L, N_BRANCH, D), 0.1),
        'pool_w': dense((L, POOL_GROUPS, POOL_GW, POOL_GW), POOL_GW),
        'pool_scale': norm_gain((L, D_POOL)),
        'pool_proj': dense((L, D_POOL, D), D_POOL),
        'conv_dw': dense((L, CONV_WIDTH, D_CONV), CONV_WIDTH),
        'conv_b': small((L, D_CONV)),
        'conv_ln_g': norm_gain((L, D_CONV)),
        'conv_ln_b': small((L, D_CONV)),
        'conv_proj': dense((L, D_CONV, D), D_CONV),
        'q_norm_g': norm_gain((L, Q_LORA)),
        'w_uq': dense((L, Q_LORA, N_HEADS * (QK_NOPE + QK_ROPE)), Q_LORA),
        'kv_norm_g': norm_gain((L, KV_LORA)),
        'w_ukv': dense((L, KV_LORA, N_HEADS * (QK_NOPE + V_HEAD)), KV_LORA),
        'mla_proj': dense((L, N_HEADS * V_HEAD, D), N_HEADS * V_HEAD),
        'w_out': dense((L, D, D), D, DEEPNORM_BETA),
        'ln1_g': norm_gain((L, D)),
        'ln1_b': small((L, D)),
        'w_router': dense((D, N_EXPERTS), D),
        'router_bias': small((N_EXPERTS,), 0.01),
        'exp_w_up': dense((L, N_EXPERTS, D, 2 * D_EXPERT), D),
        'exp_w_down': dense((L, N_EXPERTS, D_EXPERT, D), D_EXPERT, DEEPNORM_BETA),
        'ple_proj': dense((L, D_PLE, D), D_PLE, DEEPNORM_BETA),
        'ple_gate': dense((L, D, D), D),
        'ln2_g': norm_gain((L, D)),
        'ln2_b': small((L, D)),
    }


def reference(x, p, positions, ln_in_g, ln_in_b, w_in, b_gate, pool_w, pool_scale, pool_proj,
              conv_dw, conv_b, conv_ln_g, conv_ln_b, conv_proj, q_norm_g, w_uq, kv_norm_g, w_ukv,
              mla_proj, w_out, ln1_g, ln1_b, w_router, router_bias, exp_w_up, exp_w_down,
              ple_proj, ple_gate, ln2_g, ln2_b):
    cos, sin = rope_tables(positions)
    h = layer_norm(x, ln_in_g, ln_in_b)
    for i in range(DEPTH):
        y = mixer_sublayer(h, cos, sin, w_in[i], b_gate[i], pool_w[i], pool_scale[i], pool_proj[i],
                           conv_dw[i], conv_b[i], conv_ln_g[i], conv_ln_b[i], conv_proj[i],
                           q_norm_g[i], w_uq[i], kv_norm_g[i], w_ukv[i], mla_proj[i], w_out[i])
        h = layer_norm(DEEPNORM_ALPHA * h + y, ln1_g[i], ln1_b[i])
        e = jnp.matmul(p[i], ple_proj[i]) * jax.nn.sigmoid(jnp.matmul(h, ple_gate[i]))
        m = moe_ffn(h, w_router, router_bias, exp_w_up[i], exp_w_down[i])
        h = layer_norm(DEEPNORM_ALPHA * h + m + e, ln2_g[i], ln2_b[i])
    return h
```

```python
import functools

import jax
import jax.numpy as jnp
from jax import lax
from jax.experimental import pallas as pl
from jax.experimental.pallas import tpu as pltpu

F32 = jnp.float32
BF16 = jnp.bfloat16

CHUNK = 64
POOL_WINDOWS = (2, 4, 8, 16)
POOL_GW = 128
D_POOL = 512
D_CONV = 512
CONV_WIDTH = 31
N_HEADS = 8
QK_NOPE = 64
QK_ROPE = 32
V_HEAD = 64
Q_LORA = 384
KV_LORA = 256
ROPE_THETA = 10000.0
N_EXPERTS = 16
N_GROUPS = 4
EXPERTS_PER_GROUP = 4
D_EXPERT = 512
LN_EPS = 1e-5
RMS_EPS = 1e-6

LANES = 128
SUBLANES = 8
HEAD_BLOCK = LANES
ROPE_LANE0 = QK_NOPE
HALF_ROPE = QK_ROPE // 2

TM_PROJ = 512
TM_PREP = 512
TM_MERGE = 256
TM_FINAL = 512
TQ = 256
TK = 256
POOL_CHUNK = 256
POOL_HALO = 16
CONV_CHUNK = 64
CONV_HALO = 32
MOE_TILE = 256
ROUTE_ROWS = 8

PAIRS = ((0, 1), (0, 2), (0, 3), (1, 2), (1, 3), (2, 3))
N_CLASSES = N_GROUPS * len(PAIRS)

VMEM_LIMIT = 56 * 1024 * 1024


def _cparams(sem):
    return pltpu.CompilerParams(dimension_semantics=sem, vmem_limit_bytes=VMEM_LIMIT)


def _layer_norm(x, g, b):
    mu = jnp.mean(x, axis=-1, keepdims=True)
    xc = x - mu
    var = jnp.mean(xc * xc, axis=-1, keepdims=True)
    return xc * lax.rsqrt(var + LN_EPS) * g + b


def _rms_norm(x, g):
    ms = jnp.mean(x * x, axis=-1, keepdims=True)
    return x * lax.rsqrt(ms + RMS_EPS) * g


def _full(shape):
    n = len(shape)
    return pl.BlockSpec(shape, lambda *_: (0,) * n)


def _ln_kernel(x_ref, g_ref, b_ref, o_ref):
    o_ref[...] = _layer_norm(x_ref[...], g_ref[...], b_ref[...])


def _ln_call(x, g, b):
    t, d = x.shape
    return pl.pallas_call(
        _ln_kernel,
        grid=(t // TM_FINAL,),
        in_specs=[pl.BlockSpec((TM_FINAL, d), lambda i: (i, 0)), _full((1, d)), _full((1, d))],
        out_specs=pl.BlockSpec((TM_FINAL, d), lambda i: (i, 0)),
        out_shape=jax.ShapeDtypeStruct((t, d), F32),
        compiler_params=_cparams(("arbitrary",)),
        name="ln_in",
    )(x, g, b)


OFF_CONV = D_POOL
OFF_CQKV = OFF_CONV + 2 * D_CONV
W_CQKV = Q_LORA + KV_LORA + HEAD_BLOCK
OFF_GATE = OFF_CQKV + W_CQKV
N_COLS_CHUNK = 512


def _in_proj_kernel(h_ref, w_ref, pool_ref, conv_ref, cqkv_ref, gate_ref):
    x = h_ref[...].astype(BF16)

    def mm(lo, hi):
        return jnp.dot(x, w_ref[:, lo:hi], preferred_element_type=F32).astype(BF16)

    pool_ref[...] = mm(0, OFF_CONV)
    for c in range(2 * D_CONV // N_COLS_CHUNK):
        conv_ref[:, c * N_COLS_CHUNK:(c + 1) * N_COLS_CHUNK] = mm(
            OFF_CONV + c * N_COLS_CHUNK, OFF_CONV + (c + 1) * N_COLS_CHUNK)
    cqkv_ref[...] = mm(OFF_CQKV, OFF_GATE)
    for c in range(gate_ref.shape[1] // N_COLS_CHUNK):
        gate_ref[:, c * N_COLS_CHUNK:(c + 1) * N_COLS_CHUNK] = mm(
            OFF_GATE + c * N_COLS_CHUNK, OFF_GATE + (c + 1) * N_COLS_CHUNK)


def _in_proj_call(h, w):
    t, d = h.shape
    n = w.shape[1]
    n_gate = n - OFF_GATE
    row = lambda i: (i, 0)
    return pl.pallas_call(
        _in_proj_kernel,
        grid=(t // TM_PROJ,),
        in_specs=[pl.BlockSpec((TM_PROJ, d), row), _full((d, n))],
        out_specs=[pl.BlockSpec((TM_PROJ, D_POOL), row), pl.BlockSpec((TM_PROJ, 2 * D_CONV), row),
                   pl.BlockSpec((TM_PROJ, W_CQKV), row), pl.BlockSpec((TM_PROJ, n_gate), row)],
        out_shape=[jax.ShapeDtypeStruct((t, D_POOL), BF16), jax.ShapeDtypeStruct((t, 2 * D_CONV), BF16),
                   jax.ShapeDtypeStruct((t, W_CQKV), BF16), jax.ShapeDtypeStruct((t, n_gate), BF16)],
        compiler_params=_cparams(("arbitrary",)),
        name="in_proj",
    )(h, w)


def _pool_kernel(u_ref, w_ref, scale_ref, o_ref, buf):
    s = o_ref.shape[0]
    buf[0:POOL_HALO, :] = jnp.zeros((POOL_HALO, D_POOL), F32)

    def to_f32(c, carry):
        r = pl.multiple_of(c * POOL_CHUNK, POOL_CHUNK)
        buf[pl.ds(POOL_HALO + r, POOL_CHUNK), :] = u_ref[pl.ds(r, POOL_CHUNK), :].astype(F32)
        return carry

    lax.fori_loop(0, s // POOL_CHUNK, to_f32, 0)

    def body(c, carry):
        r0 = pl.multiple_of(c * POOL_CHUNK, POOL_CHUNK)
        t = r0 + lax.broadcasted_iota(jnp.int32, (POOL_CHUNK, 1), 0)
        for g, w in enumerate(POOL_WINDOWS):
            cols = slice(g * POOL_GW, (g + 1) * POOL_GW)
            xw = buf[pl.ds(r0, POOL_CHUNK + POOL_HALO), cols]
            acc = xw
            k = 1
            while k < w:
                acc = acc + pltpu.roll(acc, k, axis=0)
                k *= 2
            cnt = jnp.minimum(t + 1, w).astype(F32)
            mixed = acc[POOL_HALO:] / cnt - xw[POOL_HALO:]
            y = jnp.dot(mixed.astype(BF16), w_ref[g], preferred_element_type=F32) * scale_ref[:, cols]
            o_ref[pl.ds(r0, POOL_CHUNK), cols] = y.astype(BF16)
        return carry

    lax.fori_loop(0, s // POOL_CHUNK, body, 0)


def _pool_call(u, w, scale, batch, seq):
    return pl.pallas_call(
        _pool_kernel,
        grid=(batch,),
        in_specs=[pl.BlockSpec((seq, D_POOL), lambda b: (b, 0)),
                  _full(w.shape), _full(scale.shape)],
        out_specs=pl.BlockSpec((seq, D_POOL), lambda b: (b, 0)),
        out_shape=jax.ShapeDtypeStruct(u.shape, BF16),
        scratch_shapes=[pltpu.VMEM((seq + POOL_HALO, D_POOL), F32)],
        compiler_params=_cparams(("arbitrary",)),
        name="pool",
    )(u, w, scale)


GLU_CHUNK = 256


def _conv_kernel(u_ref, dw_ref, cb_ref, g_ref, b_ref, o_ref, zs):
    s = o_ref.shape[0]
    zs[0:CONV_HALO, :] = jnp.zeros((CONV_HALO, D_CONV), F32)

    def glu(c, carry):
        r = pl.multiple_of(c * GLU_CHUNK, GLU_CHUNK)
        a = u_ref[pl.ds(r, GLU_CHUNK), 0:D_CONV].astype(F32)
        gate = u_ref[pl.ds(r, GLU_CHUNK), D_CONV:2 * D_CONV].astype(F32)
        zs[pl.ds(CONV_HALO + r, GLU_CHUNK), :] = a * jax.nn.sigmoid(gate)
        return carry

    lax.fori_loop(0, s // GLU_CHUNK, glu, 0)

    def body(c, carry):
        r0 = pl.multiple_of(c * CONV_CHUNK, CONV_CHUNK)
        parts = []
        for cg in range(D_CONV // LANES):
            cols = slice(cg * LANES, (cg + 1) * LANES)
            win = zs[pl.ds(r0, CONV_CHUNK + CONV_HALO), cols]
            acc = jnp.zeros((CONV_CHUNK, LANES), F32) + cb_ref[:, cols]
            for sub in range(SUBLANES):
                shifted = win if sub == 0 else pltpu.roll(win, sub, axis=0)
                for a in range(CONV_HALO // SUBLANES):
                    lag = SUBLANES * a + sub
                    if lag >= CONV_WIDTH:
                        continue
                    k = CONV_WIDTH - 1 - lag
                    lo = CONV_HALO - SUBLANES * a
                    acc = acc + dw_ref[k:k + 1, cols] * shifted[lo:lo + CONV_CHUNK]
            parts.append(acc)
        y = jnp.concatenate(parts, axis=-1)
        y = _layer_norm(y, g_ref[...], b_ref[...])
        y = y * jax.nn.sigmoid(y)
        o_ref[pl.ds(r0, CONV_CHUNK), :] = y.astype(BF16)
        return carry

    lax.fori_loop(0, s // CONV_CHUNK, body, 0)


def _conv_call(u, dw, cb, g, b, batch, seq):
    return pl.pallas_call(
        _conv_kernel,
        grid=(batch,),
        in_specs=[pl.BlockSpec((seq, 2 * D_CONV), lambda i: (i, 0)),
                  _full(dw.shape), _full(cb.shape), _full(g.shape), _full(b.shape)],
        out_specs=pl.BlockSpec((seq, D_CONV), lambda i: (i, 0)),
        out_shape=jax.ShapeDtypeStruct((u.shape[0], D_CONV), BF16),
        scratch_shapes=[pltpu.VMEM((seq + CONV_HALO, D_CONV), F32)],
        compiler_params=_cparams(("arbitrary",)),
        name="conv",
    )(u, dw, cb, g, b)


def _rope_block(x, c, s1, s2):
    return x * c + pltpu.roll(x, LANES - HALF_ROPE, axis=1) * s1 + pltpu.roll(x, HALF_ROPE, axis=1) * s2


def _mla_prep_kernel(cqkv_ref, c_ref, s1_ref, s2_ref, qg_ref, kvg_ref, wq_ref, wk_ref, wv_ref,
                     q_ref, k_ref, v_ref):
    c = c_ref[...]
    s1 = s1_ref[...]
    s2 = s2_ref[...]
    cq = _rms_norm(cqkv_ref[:, 0:Q_LORA].astype(F32), qg_ref[...]).astype(BF16)
    ckv = _rms_norm(cqkv_ref[:, Q_LORA:Q_LORA + KV_LORA].astype(F32), kvg_ref[...]).astype(BF16)
    kr = _rope_block(cqkv_ref[:, Q_LORA + KV_LORA:W_CQKV].astype(F32), c, s1, s2)
    scale = float((QK_NOPE + QK_ROPE) ** -0.5)
    for h in range(N_HEADS):
        cols = slice(h * HEAD_BLOCK, (h + 1) * HEAD_BLOCK)
        q = jnp.dot(cq, wq_ref[:, cols], preferred_element_type=F32) * scale
        q_ref[:, cols] = _rope_block(q, c, s1, s2).astype(BF16)
        k = jnp.dot(ckv, wk_ref[:, cols], preferred_element_type=F32) + kr
        k_ref[:, cols] = k.astype(BF16)
        v_ref[:, cols] = jnp.dot(ckv, wv_ref[:, cols], preferred_element_type=F32).astype(BF16)


def _mla_prep_call(cqkv, c, s1, s2, qg, kvg, wq, wk, wv):
    t = cqkv.shape[0]
    n = N_HEADS * HEAD_BLOCK
    row = lambda i: (i, 0)
    tab = pl.BlockSpec((TM_PREP, LANES), row)
    out = pl.BlockSpec((TM_PREP, n), row)
    return pl.pallas_call(
        _mla_prep_kernel,
        grid=(t // TM_PREP,),
        in_specs=[pl.BlockSpec((TM_PREP, W_CQKV), row), tab, tab, tab,
                  _full(qg.shape), _full(kvg.shape), _full(wq.shape), _full(wk.shape), _full(wv.shape)],
        out_specs=[out, out, out],
        out_shape=[jax.ShapeDtypeStruct((t, n), BF16)] * 3,
        compiler_params=_cparams(("arbitrary",)),
        name="mla_prep",
    )(cqkv, c, s1, s2, qg, kvg, wq, wk, wv)


def _attn_kernel(q_ref, k_ref, v_ref, o_ref):
    i = pl.program_id(1)
    row_chunk = lax.broadcasted_iota(jnp.int32, (TQ, TK), 0) // CHUNK
    col_chunk = lax.broadcasted_iota(jnp.int32, (TQ, TK), 1) // CHUNK
    diag_mask = col_chunk <= row_chunk
    nt = (((1,), (1,)), ((), ()))

    for pair in range(N_HEADS // 2):
        pair_out = None
        for hh in range(2):
            h = 2 * pair + hh
            cols = slice(h * HEAD_BLOCK, (h + 1) * HEAD_BLOCK)
            q = q_ref[:, cols]

            def step(kt, carry, masked, cols=cols, q=q):
                m, l, acc = carry
                k0 = pl.multiple_of(kt * TK, TK)
                kk = k_ref[pl.ds(k0, TK), cols]
                vv = v_ref[pl.ds(k0, TK), cols]
                s = lax.dot_general(q, kk, nt, preferred_element_type=F32)
                if masked:
                    s = jnp.where(diag_mask, s, -jnp.inf)
                m_new = jnp.maximum(m, jnp.max(s, axis=-1, keepdims=True))
                alpha = jnp.exp(m - m_new)
                p = jnp.exp(s - m_new)
                l = alpha * l + jnp.sum(p, axis=-1, keepdims=True)
                acc = alpha * acc + jnp.dot(p.astype(BF16), vv, preferred_element_type=F32)
                return m_new, l, acc

            init = (jnp.full((TQ, 1), -jnp.inf, F32), jnp.zeros((TQ, 1), F32),
                    jnp.zeros((TQ, HEAD_BLOCK), F32))
            carry = step(i, init, True)
            m, l, acc = lax.fori_loop(0, i, functools.partial(step, masked=False), carry)
            o = acc * (1.0 / l)
            pair_out = o if pair_out is None else pair_out + o
        o_ref[:, pair * LANES:(pair + 1) * LANES] = pair_out.astype(BF16)


def _attn_call(q, k, v, batch, seq):
    n = N_HEADS * HEAD_BLOCK
    nq = seq // TQ
    return pl.pallas_call(
        _attn_kernel,
        grid=(batch, nq),
        in_specs=[pl.BlockSpec((TQ, n), lambda b, i: (b * nq + i, 0)),
                  pl.BlockSpec((seq, n), lambda b, i: (b, 0)),
                  pl.BlockSpec((seq, n), lambda b, i: (b, 0))],
        out_specs=pl.BlockSpec((TQ, N_HEADS * V_HEAD), lambda b, i: (b * nq + i, 0)),
        out_shape=jax.ShapeDtypeStruct((q.shape[0], N_HEADS * V_HEAD), BF16),
        compiler_params=_cparams(("arbitrary", "arbitrary")),
        name="attn",
    )(q, k, v)


def _store_token_major(ref, x):
    rows = x.shape[0]
    for j in range(x.shape[1] // LANES):
        ref[pl.ds(j, rows, stride=SUBLANES), :] = x[:, j * LANES:(j + 1) * LANES]


def _load_token_major(ref, rows, d):
    return jnp.concatenate([ref[pl.ds(j, rows, stride=SUBLANES), :] for j in range(d // LANES)], axis=-1)


def _merge_kernel(alpha, ap_ref, ac_ref, at_ref, gl_ref, h_ref, p_ref, pp_ref, cp_ref, mp_ref, bg_ref,
                  wo_ref, g_ref, b_ref, pproj_ref, pgate_ref, wrh_ref, wrl_ref,
                  h1_ref, e_ref, lg_ref):
    d = h_ref.shape[1]
    merged = None
    for br, (a_ref, w_ref) in enumerate(((ap_ref, pp_ref), (ac_ref, cp_ref), (at_ref, mp_ref))):
        y = jnp.dot(a_ref[...], w_ref[...], preferred_element_type=F32)
        gate = jax.nn.sigmoid(gl_ref[:, br * d:(br + 1) * d].astype(F32) + bg_ref[br:br + 1, :])
        merged = gate * y if merged is None else merged + gate * y
    y = jnp.dot(merged.astype(BF16), wo_ref[...], preferred_element_type=F32)
    h1 = _layer_norm(alpha * h_ref[...] + y, g_ref[...], b_ref[...])
    h1b = h1.astype(BF16)
    e = jnp.dot(p_ref[...].astype(BF16), pproj_ref[...], preferred_element_type=F32)
    e = e * jax.nn.sigmoid(jnp.dot(h1b, pgate_ref[...], preferred_element_type=F32))
    e_ref[...] = e.astype(BF16)
    h1l = (h1 - h1b.astype(F32)).astype(BF16)
    nt = (((1,), (1,)), ((), ()))
    lg = lax.dot_general(wrh_ref[...], h1b, nt, preferred_element_type=F32)
    lg = lg + lax.dot_general(wrh_ref[...], h1l, nt, preferred_element_type=F32)
    lg = lg + lax.dot_general(wrl_ref[...], h1b, nt, preferred_element_type=F32)
    lg_ref[...] = lg
    _store_token_major(h1_ref, h1)


def _merge_call(alpha, ap, ac, at, gl, h, p, pp, cp, mp, bg, wo, g, b, pproj, pgate, wrh, wrl):
    t, d = h.shape
    tm = TM_MERGE
    row = lambda i: (i, 0)
    ins = [ap, ac, at, gl, h, p]
    in_specs = [pl.BlockSpec((tm, a.shape[1]), row) for a in ins]
    consts = [pp, cp, mp, bg, wo, g, b, pproj, pgate, wrh, wrl]
    in_specs += [_full(a.shape) for a in consts]
    return pl.pallas_call(
        functools.partial(_merge_kernel, alpha),
        grid=(t // tm,),
        in_specs=in_specs,
        out_specs=[pl.BlockSpec((tm * SUBLANES, LANES), row), pl.BlockSpec((tm, d), row),
                   pl.BlockSpec((N_EXPERTS, tm), lambda i: (0, i))],
        out_shape=[jax.ShapeDtypeStruct((t * SUBLANES, LANES), F32), jax.ShapeDtypeStruct((t, d), BF16),
                   jax.ShapeDtypeStruct((N_EXPERTS, t), F32)],
        compiler_params=_cparams(("arbitrary",)),
        name="merge",
    )(*ins, *consts)


def _route_kernel(lg_ref, bias_ref, cls_ref, wa_ref, wb_ref):
    aff = [jax.nn.sigmoid(lg_ref[e]) for e in range(N_EXPERTS)]
    sel = [aff[e] + bias_ref[e] for e in range(N_EXPERTS)]
    n = EXPERTS_PER_GROUP

    def top2_sum(vals):
        best = None
        for a, b in PAIRS:
            s = vals[a] + vals[b]
            best = s if best is None else jnp.maximum(best, s)
        return best

    grp = jnp.zeros(aff[0].shape, jnp.int32)
    best = top2_sum(sel[0:n])
    for g in range(1, N_GROUPS):
        sc = top2_sum(sel[g * n:(g + 1) * n])
        better = sc > best
        grp = jnp.where(better, g, grp)
        best = jnp.where(better, sc, best)
    vs, afs = [], []
    for j in range(n):
        v, a = sel[j], aff[j]
        for g in range(1, N_GROUPS):
            v = jnp.where(grp == g, sel[g * n + j], v)
            a = jnp.where(grp == g, aff[g * n + j], a)
        vs.append(v)
        afs.append(a)
    first = jnp.zeros_like(grp)
    fv = vs[0]
    for j in range(1, n):
        better = vs[j] > fv
        first = jnp.where(better, j, first)
        fv = jnp.where(better, vs[j], fv)
    second = jnp.full_like(grp, -1)
    sv = jnp.full_like(fv, -jnp.inf)
    for j in range(n):
        better = (first != j) & ((second < 0) | (vs[j] > sv))
        second = jnp.where(better, j, second)
        sv = jnp.where(better, vs[j], sv)
    lo = jnp.minimum(first, second)
    hi = jnp.maximum(first, second)
    a_lo, a_hi = afs[0], afs[0]
    for j in range(1, n):
        a_lo = jnp.where(lo == j, afs[j], a_lo)
        a_hi = jnp.where(hi == j, afs[j], a_hi)
    pair = jnp.zeros_like(grp)
    for idx, (a, b) in enumerate(PAIRS):
        pair = jnp.where((lo == a) & (hi == b), idx, pair)
    tot = a_lo + a_hi
    cls_ref[...] = grp * len(PAIRS) + pair
    wa_ref[...] = a_lo / tot
    wb_ref[...] = a_hi / tot


def _route_call(lg3, bias):
    _, rows, lanes = lg3.shape
    blk = pl.BlockSpec((ROUTE_ROWS, lanes), lambda i: (i, 0))
    return pl.pallas_call(
        _route_kernel,
        grid=(rows // ROUTE_ROWS,),
        in_specs=[pl.BlockSpec((N_EXPERTS, ROUTE_ROWS, lanes), lambda i: (0, i, 0)),
                  pl.BlockSpec(memory_space=pltpu.SMEM)],
        out_specs=[blk, blk, blk],
        out_shape=[jax.ShapeDtypeStruct((rows, lanes), jnp.int32),
                   jax.ShapeDtypeStruct((rows, lanes), F32), jax.ShapeDtypeStruct((rows, lanes), F32)],
        compiler_params=_cparams(("arbitrary",)),
        name="route",
    )(lg3, bias)


GATHER_UNROLL = 8


def _moe_kernel(ta_ref, tb_ref, nused_ref, valid_ref, tok_ref,
                h_hbm, wa_ref, wb_ref, upa_ref, upb_ref, dna_ref, dnb_ref,
                m_hbm, xbuf, ybuf, gsem, ssem):
    i = pl.program_id(0)
    n_used = nused_ref[0]
    slot = i % 2
    rows = MOE_TILE * SUBLANES

    def row_copy_in(tile, s, r):
        tok = tok_ref[tile * MOE_TILE + r]
        return pltpu.make_async_copy(
            h_hbm.at[pl.ds(pl.multiple_of(tok * SUBLANES, SUBLANES), SUBLANES), :],
            xbuf.at[s, pl.ds(pl.multiple_of(r * SUBLANES, SUBLANES), SUBLANES), :],
            gsem.at[s])

    def row_copy_out(tile, s, r):
        tok = tok_ref[tile * MOE_TILE + r]
        return pltpu.make_async_copy(
            ybuf.at[s, pl.ds(pl.multiple_of(r * SUBLANES, SUBLANES), SUBLANES), :],
            m_hbm.at[pl.ds(pl.multiple_of(tok * SUBLANES, SUBLANES), SUBLANES), :],
            ssem.at[s])

    def start_rows(make, tile, s, count):
        def body8(c, carry):
            for u in range(GATHER_UNROLL):
                make(tile, s, c * GATHER_UNROLL + u).start()
            return carry

        def body1(r, carry):
            make(tile, s, r).start()
            return carry

        full = count // GATHER_UNROLL
        lax.fori_loop(0, full, body8, 0)
        lax.fori_loop(full * GATHER_UNROLL, count, body1, 0)

    def wait_gather(s):
        pltpu.make_async_copy(h_hbm.at[pl.ds(0, rows), :], xbuf.at[s], gsem.at[s]).wait()

    def wait_scatter(tile, s):
        n = pl.multiple_of(valid_ref[tile] * SUBLANES, SUBLANES)
        pltpu.make_async_copy(ybuf.at[s, pl.ds(0, n), :], m_hbm.at[pl.ds(0, n), :], ssem.at[s]).wait()

    @pl.when(i == 0)
    def _():
        start_rows(row_copy_in, 0, 0, MOE_TILE)

    @pl.when(i + 1 < n_used)
    def _():
        start_rows(row_copy_in, i + 1, 1 - slot, MOE_TILE)

    @pl.when(i < n_used)
    def _():
        wait_gather(slot)
        x = _load_token_major(xbuf.at[slot], MOE_TILE, upa_ref.shape[1]).astype(BF16)

        def ffn(up_ref, dn_ref):
            gu = jnp.dot(x, up_ref[0], preferred_element_type=F32)
            hid = jax.nn.silu(gu[:, :D_EXPERT]) * gu[:, D_EXPERT:]
            return jnp.dot(hid.astype(BF16), dn_ref[0], preferred_element_type=F32)

        y = ffn(upa_ref, dna_ref) * wa_ref[...] + ffn(upb_ref, dnb_ref) * wb_ref[...]

        @pl.when(i >= 2)
        def _():
            wait_scatter(i - 2, slot)

        _store_token_major(ybuf.at[slot], y)
        start_rows(row_copy_out, i, slot, valid_ref[i])

        @pl.when(i == n_used - 1)
        def _():
            @pl.when(i >= 1)
            def _():
                wait_scatter(i - 1, 1 - slot)
            wait_scatter(i, slot)


def _moe_call(tile_a, tile_b, n_used, tile_valid, slot_tok, h1_tm, slot_wa, slot_wb, w_up, w_down, n_tiles, t):
    d = w_up.shape[1]
    wspec = pl.BlockSpec((MOE_TILE, 1), lambda i, *_: (i, 0))
    grid_spec = pltpu.PrefetchScalarGridSpec(
        num_scalar_prefetch=5,
        grid=(n_tiles,),
        in_specs=[pl.BlockSpec(memory_space=pl.ANY), wspec, wspec,
                  pl.BlockSpec((1, d, 2 * D_EXPERT), lambda i, ta, tb, *_: (ta[i], 0, 0)),
                  pl.BlockSpec((1, d, 2 * D_EXPERT), lambda i, ta, tb, *_: (tb[i], 0, 0)),
                  pl.BlockSpec((1, D_EXPERT, d), lambda i, ta, tb, *_: (ta[i], 0, 0)),
                  pl.BlockSpec((1, D_EXPERT, d), lambda i, ta, tb, *_: (tb[i], 0, 0))],
        out_specs=pl.BlockSpec(memory_space=pl.ANY),
        scratch_shapes=[pltpu.VMEM((2, MOE_TILE * SUBLANES, LANES), F32),
                        pltpu.VMEM((2, MOE_TILE * SUBLANES, LANES), F32),
                        pltpu.SemaphoreType.DMA((2,)), pltpu.SemaphoreType.DMA((2,))])
    return pl.pallas_call(
        _moe_kernel,
        grid_spec=grid_spec,
        out_shape=jax.ShapeDtypeStruct((t * SUBLANES, LANES), F32),
        compiler_params=_cparams(("arbitrary",)),
        name="moe",
    )(tile_a, tile_b, n_used, tile_valid, slot_tok, h1_tm, slot_wa, slot_wb, w_up, w_up, w_down, w_down)


def _final_kernel(alpha, h1_ref, m_ref, e_ref, g_ref, b_ref, o_ref):
    rows, d = o_ref.shape
    h1 = _load_token_major(h1_ref, rows, d)
    m = _load_token_major(m_ref, rows, d)
    o_ref[...] = _layer_norm(alpha * h1 + m + e_ref[...].astype(F32), g_ref[...], b_ref[...])


def _final_call(alpha, h1_tm, m_tm, e, g, b):
    t, d = e.shape
    tm = TM_FINAL
    row = lambda i: (i, 0)
    tmaj = pl.BlockSpec((tm * SUBLANES, LANES), row)
    return pl.pallas_call(
        functools.partial(_final_kernel, alpha),
        grid=(t // tm,),
        in_specs=[tmaj, tmaj, pl.BlockSpec((tm, d), row), _full(g.shape), _full(b.shape)],
        out_specs=pl.BlockSpec((tm, d), row),
        out_shape=jax.ShapeDtypeStruct((t, d), F32),
        compiler_params=_cparams(("arbitrary",)),
        name="final_ln",
    )(h1_tm, m_tm, e, g, b)


def _pack_w_in(w_in):
    off_q = OFF_CONV + 2 * D_CONV
    off_kr = off_q + Q_LORA + KV_LORA
    off_gate = off_kr + QK_ROPE
    kr = jnp.pad(w_in[..., off_kr:off_gate], ((0, 0), (0, 0), (ROPE_LANE0, HEAD_BLOCK - ROPE_LANE0 - QK_ROPE)))
    return jnp.concatenate([w_in[..., :off_kr], kr, w_in[..., off_gate:]], axis=-1).astype(BF16)


def _pack_heads(w, lo, width, stride, lane0):
    blocks = []
    for h in range(N_HEADS):
        l0 = lane0(h)
        blocks.append(jnp.pad(w[..., h * stride + lo:h * stride + lo + width],
                              ((0, 0), (0, 0), (l0, HEAD_BLOCK - l0 - width))))
    return jnp.concatenate(blocks, axis=-1).astype(BF16)


def _rope_tables(positions):
    inv_freq = jnp.power(ROPE_THETA, -jnp.arange(0, QK_ROPE, 2, dtype=F32) / QK_ROPE)
    ang = positions.astype(F32).reshape(-1, 1) * inv_freq
    cos, sin = jnp.cos(ang), jnp.sin(ang)
    t = ang.shape[0]
    ones_lo = jnp.ones((t, ROPE_LANE0), F32)
    ones_hi = jnp.ones((t, HEAD_BLOCK - ROPE_LANE0 - QK_ROPE), F32)
    zeros_lo = jnp.zeros((t, ROPE_LANE0), F32)
    zeros_half = jnp.zeros((t, HALF_ROPE), F32)
    zeros_hi = jnp.zeros((t, HEAD_BLOCK - ROPE_LANE0 - QK_ROPE), F32)
    c = jnp.concatenate([ones_lo, cos, cos, ones_hi], axis=-1)
    s1 = jnp.concatenate([zeros_lo, -sin, zeros_half, zeros_hi], axis=-1)
    s2 = jnp.concatenate([zeros_lo, zeros_half, sin, zeros_hi], axis=-1)
    return c, s1, s2


def _routing_plan(cls, wa, wb, n_tiles):
    t = cls.shape[0]
    n_slots = n_tiles * MOE_TILE
    order = jnp.argsort(cls, stable=True).astype(jnp.int32)
    s_cls = cls[order]
    counts = jnp.zeros((N_CLASSES,), jnp.int32).at[cls].add(1)
    start = jnp.cumsum(counts) - counts
    padded = (counts + MOE_TILE - 1) // MOE_TILE * MOE_TILE
    pad_end = jnp.cumsum(padded)
    pad_start = pad_end - padded
    dest = pad_start[s_cls] + jnp.arange(t, dtype=jnp.int32) - start[s_cls]
    slot_tok = jnp.zeros((n_slots,), jnp.int32).at[dest].set(order)
    slot_wa = jnp.zeros((n_slots,), F32).at[dest].set(wa[order])
    slot_wb = jnp.zeros((n_slots,), F32).at[dest].set(wb[order])
    tile_row0 = jnp.arange(n_tiles, dtype=jnp.int32) * MOE_TILE
    tile_cls = jnp.minimum(jnp.searchsorted(pad_end, tile_row0, side='right'), N_CLASSES - 1).astype(jnp.int32)
    tile_valid = jnp.clip(pad_start[tile_cls] + counts[tile_cls] - tile_row0, 0, MOE_TILE).astype(jnp.int32)
    pair_lo = jnp.array([p[0] for p in PAIRS], jnp.int32)
    pair_hi = jnp.array([p[1] for p in PAIRS], jnp.int32)
    grp = tile_cls // len(PAIRS)
    tile_a = grp * EXPERTS_PER_GROUP + pair_lo[tile_cls % len(PAIRS)]
    tile_b = grp * EXPERTS_PER_GROUP + pair_hi[tile_cls % len(PAIRS)]
    n_used = (pad_end[-1] // MOE_TILE).astype(jnp.int32).reshape(1)
    return tile_a, tile_b, n_used, tile_valid, slot_tok, slot_wa.reshape(-1, 1), slot_wb.reshape(-1, 1)


def kernel(x, p, positions, ln_in_g, ln_in_b, w_in, b_gate, pool_w, pool_scale, pool_proj, conv_dw, conv_b, conv_ln_g, conv_ln_b, conv_proj, q_norm_g, w_uq, kv_norm_g, w_ukv, mla_proj, w_out, ln1_g, ln1_b, w_router, router_bias, exp_w_up, exp_w_down, ple_proj, ple_gate, ln2_g, ln2_b):
    batch, seq, d = x.shape
    depth = w_in.shape[0]
    t = batch * seq
    alpha = float((2 * depth) ** 0.25)
    n_tiles = (t + N_CLASSES * (MOE_TILE - 1)) // MOE_TILE + 1

    w_in_p = _pack_w_in(w_in)
    pool_w_b = pool_w.astype(BF16)
    pool_proj_b, conv_proj_b, mla_proj_b = pool_proj.astype(BF16), conv_proj.astype(BF16), mla_proj.astype(BF16)
    qk = QK_NOPE + QK_ROPE
    wq_p = (_pack_heads(w_uq, 0, QK_NOPE, qk, lambda h: 0)
            + _pack_heads(w_uq, QK_NOPE, QK_ROPE, qk, lambda h: ROPE_LANE0))
    wk_p = _pack_heads(w_ukv, 0, QK_NOPE, QK_NOPE + V_HEAD, lambda h: 0)
    wv_p = _pack_heads(w_ukv, QK_NOPE, V_HEAD, QK_NOPE + V_HEAD, lambda h: (h % 2) * V_HEAD)
    w_out_b, ple_proj_b, ple_gate_b = w_out.astype(BF16), ple_proj.astype(BF16), ple_gate.astype(BF16)
    w_up_b, w_down_b = exp_w_up.astype(BF16), exp_w_down.astype(BF16)
    wr_t = w_router.T
    wr_hi = wr_t.astype(BF16)
    wr_lo = (wr_t - wr_hi.astype(F32)).astype(BF16)
    rope_c, rope_s1, rope_s2 = _rope_tables(positions)
    p2 = p.reshape(depth, t, -1)

    h = _ln_call(x.reshape(t, d), ln_in_g.reshape(1, d), ln_in_b.reshape(1, d))
    for i in range(depth):
        u_pool, u_conv, cqkv, glog = _in_proj_call(h, w_in_p[i])
        a_pool = _pool_call(u_pool, pool_w_b[i], pool_scale[i].reshape(1, -1), batch, seq)
        a_conv = _conv_call(u_conv, conv_dw[i], conv_b[i].reshape(1, -1), conv_ln_g[i].reshape(1, -1),
                            conv_ln_b[i].reshape(1, -1), batch, seq)
        q, k, v = _mla_prep_call(cqkv, rope_c, rope_s1, rope_s2, q_norm_g[i].reshape(1, -1),
                                 kv_norm_g[i].reshape(1, -1), wq_p[i], wk_p[i], wv_p[i])
        a_attn = _attn_call(q, k, v, batch, seq)
        h1_tm, e, logits = _merge_call(alpha, a_pool, a_conv, a_attn, glog, h, p2[i],
                                       pool_proj_b[i], conv_proj_b[i], mla_proj_b[i], b_gate[i], w_out_b[i],
                                       ln1_g[i].reshape(1, d), ln1_b[i].reshape(1, d),
                                       ple_proj_b[i], ple_gate_b[i], wr_hi, wr_lo)
        cls, wa, wb = _route_call(logits.reshape(N_EXPERTS, t // LANES, LANES), router_bias)
        plan = _routing_plan(cls.reshape(t), wa.reshape(t), wb.reshape(t), n_tiles)
        m_tm = _moe_call(*plan[:5], h1_tm, plan[5], plan[6], w_up_b[i], w_down_b[i], n_tiles, t)
        h = _final_call(alpha, h1_tm, m_tm, e, ln2_g[i].reshape(1, d), ln2_b[i].reshape(1, d))
    return h.reshape(batch, seq, d)
```

```python
import functools

import jax
import jax.numpy as jnp
from jax import lax
from jax.experimental import pallas as pl
from jax.experimental.pallas import tpu as pltpu

F32 = jnp.float32
BF16 = jnp.bfloat16

CHUNK = 64
POOL_WINDOWS = (2, 4, 8, 16)
POOL_GW = 128
D_POOL = 512
D_CONV = 512
CONV_WIDTH = 31
N_HEADS = 8
QK_NOPE = 64
QK_ROPE = 32
V_HEAD = 64
Q_LORA = 384
KV_LORA = 256
ROPE_THETA = 10000.0
N_EXPERTS = 16
N_GROUPS = 4
EXPERTS_PER_GROUP = 4
D_EXPERT = 512
LN_EPS = 1e-5
RMS_EPS = 1e-6

LANES = 128
SUBLANES = 8
HEAD_BLOCK = LANES
ROPE_LANE0 = QK_NOPE
HALF_ROPE = QK_ROPE // 2
BF16_ROWS = 16
VT_BLOCK = V_HEAD + BF16_ROWS
LOG2_E = 1.4426950408889634

TM_PROJ = 512
TM_PREP = 512
TM_MERGE = 256
TM_FINAL = 512
TQ = 256
TK = 256
ATTN_LOOKAHEAD = 3
POOL_CHUNK = 256
POOL_HALO = 16
CONV_CHUNK = 64
CONV_HALO = 32
MOE_TILE = 256
ROUTE_ROWS = 8

PAIRS = ((0, 1), (0, 2), (0, 3), (1, 2), (1, 3), (2, 3))
N_CLASSES = N_GROUPS * len(PAIRS)

VMEM_LIMIT = 56 * 1024 * 1024


def _cparams(sem):
    return pltpu.CompilerParams(dimension_semantics=sem, vmem_limit_bytes=VMEM_LIMIT)


def _layer_norm(x, g, b):
    mu = jnp.mean(x, axis=-1, keepdims=True)
    xc = x - mu
    var = jnp.mean(xc * xc, axis=-1, keepdims=True)
    return xc * lax.rsqrt(var + LN_EPS) * g + b


def _rms_norm(x, g):
    ms = jnp.mean(x * x, axis=-1, keepdims=True)
    return x * lax.rsqrt(ms + RMS_EPS) * g


def _full(shape):
    n = len(shape)
    return pl.BlockSpec(shape, lambda *_: (0,) * n)


def _ln_kernel(x_ref, g_ref, b_ref, o_ref):
    o_ref[...] = _layer_norm(x_ref[...], g_ref[...], b_ref[...])


def _ln_call(x, g, b):
    t, d = x.shape
    return pl.pallas_call(
        _ln_kernel,
        grid=(t // TM_FINAL,),
        in_specs=[pl.BlockSpec((TM_FINAL, d), lambda i: (i, 0)), _full((1, d)), _full((1, d))],
        out_specs=pl.BlockSpec((TM_FINAL, d), lambda i: (i, 0)),
        out_shape=jax.ShapeDtypeStruct((t, d), F32),
        compiler_params=_cparams(("arbitrary",)),
        name="ln_in",
    )(x, g, b)


OFF_CONV = D_POOL
OFF_CQKV = OFF_CONV + 2 * D_CONV
W_CQKV = Q_LORA + KV_LORA + HEAD_BLOCK
OFF_GATE = OFF_CQKV + W_CQKV
N_COLS_CHUNK = 512


def _in_proj_kernel(h_ref, w_ref, pool_ref, conv_ref, cqkv_ref, gate_ref):
    x = h_ref[...].astype(BF16)

    def mm(lo, hi):
        return jnp.dot(x, w_ref[:, lo:hi], preferred_element_type=F32).astype(BF16)

    pool_ref[...] = mm(0, OFF_CONV)
    for c in range(2 * D_CONV // N_COLS_CHUNK):
        conv_ref[:, c * N_COLS_CHUNK:(c + 1) * N_COLS_CHUNK] = mm(
            OFF_CONV + c * N_COLS_CHUNK, OFF_CONV + (c + 1) * N_COLS_CHUNK)
    cqkv_ref[...] = mm(OFF_CQKV, OFF_GATE)
    for c in range(gate_ref.shape[1] // N_COLS_CHUNK):
        gate_ref[:, c * N_COLS_CHUNK:(c + 1) * N_COLS_CHUNK] = mm(
            OFF_GATE + c * N_COLS_CHUNK, OFF_GATE + (c + 1) * N_COLS_CHUNK)


def _in_proj_call(h, w):
    t, d = h.shape
    n = w.shape[1]
    n_gate = n - OFF_GATE
    row = lambda i: (i, 0)
    return pl.pallas_call(
        _in_proj_kernel,
        grid=(t // TM_PROJ,),
        in_specs=[pl.BlockSpec((TM_PROJ, d), row), _full((d, n))],
        out_specs=[pl.BlockSpec((TM_PROJ, D_POOL), row), pl.BlockSpec((TM_PROJ, 2 * D_CONV), row),
                   pl.BlockSpec((TM_PROJ, W_CQKV), row), pl.BlockSpec((TM_PROJ, n_gate), row)],
        out_shape=[jax.ShapeDtypeStruct((t, D_POOL), BF16), jax.ShapeDtypeStruct((t, 2 * D_CONV), BF16),
                   jax.ShapeDtypeStruct((t, W_CQKV), BF16), jax.ShapeDtypeStruct((t, n_gate), BF16)],
        compiler_params=_cparams(("arbitrary",)),
        name="in_proj",
    )(h, w)


def _pool_kernel(u_ref, w_ref, scale_ref, o_ref, buf):
    s = o_ref.shape[0]
    buf[0:POOL_HALO, :] = jnp.zeros((POOL_HALO, D_POOL), F32)

    def to_f32(c, carry):
        r = pl.multiple_of(c * POOL_CHUNK, POOL_CHUNK)
        buf[pl.ds(POOL_HALO + r, POOL_CHUNK), :] = u_ref[pl.ds(r, POOL_CHUNK), :].astype(F32)
        return carry

    lax.fori_loop(0, s // POOL_CHUNK, to_f32, 0)

    def body(c, carry):
        r0 = pl.multiple_of(c * POOL_CHUNK, POOL_CHUNK)
        t = r0 + lax.broadcasted_iota(jnp.int32, (POOL_CHUNK, 1), 0)
        for g, w in enumerate(POOL_WINDOWS):
            cols = slice(g * POOL_GW, (g + 1) * POOL_GW)
            xw = buf[pl.ds(r0, POOL_CHUNK + POOL_HALO), cols]
            acc = xw
            k = 1
            while k < w:
                acc = acc + pltpu.roll(acc, k, axis=0)
                k *= 2
            cnt = jnp.minimum(t + 1, w).astype(F32)
            mixed = acc[POOL_HALO:] / cnt - xw[POOL_HALO:]
            y = jnp.dot(mixed.astype(BF16), w_ref[g], preferred_element_type=F32) * scale_ref[:, cols]
            o_ref[pl.ds(r0, POOL_CHUNK), cols] = y.astype(BF16)
        return carry

    lax.fori_loop(0, s // POOL_CHUNK, body, 0)


def _pool_call(u, w, scale, batch, seq):
    return pl.pallas_call(
        _pool_kernel,
        grid=(batch,),
        in_specs=[pl.BlockSpec((seq, D_POOL), lambda b: (b, 0)),
                  _full(w.shape), _full(scale.shape)],
        out_specs=pl.BlockSpec((seq, D_POOL), lambda b: (b, 0)),
        out_shape=jax.ShapeDtypeStruct(u.shape, BF16),
        scratch_shapes=[pltpu.VMEM((seq + POOL_HALO, D_POOL), F32)],
        compiler_params=_cparams(("arbitrary",)),
        name="pool",
    )(u, w, scale)


GLU_CHUNK = 256


def _conv_kernel(u_ref, dw_ref, cb_ref, g_ref, b_ref, o_ref, zs):
    s = o_ref.shape[0]
    zs[0:CONV_HALO, :] = jnp.zeros((CONV_HALO, D_CONV), F32)

    def glu(c, carry):
        r = pl.multiple_of(c * GLU_CHUNK, GLU_CHUNK)
        a = u_ref[pl.ds(r, GLU_CHUNK), 0:D_CONV].astype(F32)
        gate = u_ref[pl.ds(r, GLU_CHUNK), D_CONV:2 * D_CONV].astype(F32)
        zs[pl.ds(CONV_HALO + r, GLU_CHUNK), :] = a * jax.nn.sigmoid(gate)
        return carry

    lax.fori_loop(0, s // GLU_CHUNK, glu, 0)

    def body(c, carry):
        r0 = pl.multiple_of(c * CONV_CHUNK, CONV_CHUNK)
        parts = []
        for cg in range(D_CONV // LANES):
            cols = slice(cg * LANES, (cg + 1) * LANES)
            win = zs[pl.ds(r0, CONV_CHUNK + CONV_HALO), cols]
            acc = jnp.zeros((CONV_CHUNK, LANES), F32) + cb_ref[:, cols]
            for sub in range(SUBLANES):
                shifted = win if sub == 0 else pltpu.roll(win, sub, axis=0)
                for a in range(CONV_HALO // SUBLANES):
                    lag = SUBLANES * a + sub
                    if lag >= CONV_WIDTH:
                        continue
                    k = CONV_WIDTH - 1 - lag
                    lo = CONV_HALO - SUBLANES * a
                    acc = acc + dw_ref[k:k + 1, cols] * shifted[lo:lo + CONV_CHUNK]
            parts.append(acc)
        y = jnp.concatenate(parts, axis=-1)
        y = _layer_norm(y, g_ref[...], b_ref[...])
        y = y * jax.nn.sigmoid(y)
        o_ref[pl.ds(r0, CONV_CHUNK), :] = y.astype(BF16)
        return carry

    lax.fori_loop(0, s // CONV_CHUNK, body, 0)


def _conv_call(u, dw, cb, g, b, batch, seq):
    return pl.pallas_call(
        _conv_kernel,
        grid=(batch,),
        in_specs=[pl.BlockSpec((seq, 2 * D_CONV), lambda i: (i, 0)),
                  _full(dw.shape), _full(cb.shape), _full(g.shape), _full(b.shape)],
        out_specs=pl.BlockSpec((seq, D_CONV), lambda i: (i, 0)),
        out_shape=jax.ShapeDtypeStruct((u.shape[0], D_CONV), BF16),
        scratch_shapes=[pltpu.VMEM((seq + CONV_HALO, D_CONV), F32)],
        compiler_params=_cparams(("arbitrary",)),
        name="conv",
    )(u, dw, cb, g, b)


def _rope_block(x, c, s1, s2):
    return x * c + pltpu.roll(x, LANES - HALF_ROPE, axis=1) * s1 + pltpu.roll(x, HALF_ROPE, axis=1) * s2


def _rope_block_t(x, c, s1, s2):
    return x * c + pltpu.roll(x, HEAD_BLOCK - HALF_ROPE, axis=0) * s1 + pltpu.roll(x, HALF_ROPE, axis=0) * s2


def _mla_prep_kernel(cqkv_ref, c_ref, s1_ref, s2_ref, ct_ref, s1t_ref, s2t_ref, qg_ref, kvg_ref,
                     wqt_ref, wk_ref, wvt_ref, ones_ref, qt_ref, k_ref, vt_ref):
    cq = _rms_norm(cqkv_ref[:, 0:Q_LORA].astype(F32), qg_ref[...]).astype(BF16)
    ckv = _rms_norm(cqkv_ref[:, Q_LORA:Q_LORA + KV_LORA].astype(F32), kvg_ref[...]).astype(BF16)
    kr = _rope_block(cqkv_ref[:, Q_LORA + KV_LORA:W_CQKV].astype(F32), c_ref[...], s1_ref[...], s2_ref[...])
    scale = float((QK_NOPE + QK_ROPE) ** -0.5 * LOG2_E)
    nt = (((1,), (1,)), ((), ()))
    ct, s1t, s2t = ct_ref[...], s1t_ref[...], s2t_ref[...]
    for h in range(N_HEADS):
        cols = slice(h * HEAD_BLOCK, (h + 1) * HEAD_BLOCK)
        qt = lax.dot_general(wqt_ref[cols, :], cq, nt, preferred_element_type=F32) * scale
        qt_ref[cols, :] = _rope_block_t(qt, ct, s1t, s2t).astype(BF16)
        k = jnp.dot(ckv, wk_ref[:, cols], preferred_element_type=F32) + kr
        k_ref[:, cols] = k.astype(BF16)
    vt = lax.dot_general(wvt_ref[...], ckv, nt, preferred_element_type=F32) + ones_ref[...]
    vt_ref[...] = vt.astype(BF16)


def _mla_prep_call(cqkv, tabs, tabs_t, qg, kvg, wqt, wk, wvt, ones_rows):
    t = cqkv.shape[0]
    n = N_HEADS * HEAD_BLOCK
    nv = N_HEADS * VT_BLOCK
    row = lambda i: (i, 0)
    col = lambda i: (0, i)
    tab = pl.BlockSpec((TM_PREP, LANES), row)
    tab_t = pl.BlockSpec((LANES, TM_PREP), col)
    return pl.pallas_call(
        _mla_prep_kernel,
        grid=(t // TM_PREP,),
        in_specs=[pl.BlockSpec((TM_PREP, W_CQKV), row), tab, tab, tab, tab_t, tab_t, tab_t,
                  _full(qg.shape), _full(kvg.shape), _full(wqt.shape), _full(wk.shape), _full(wvt.shape),
                  _full(ones_rows.shape)],
        out_specs=[pl.BlockSpec((n, TM_PREP), col), pl.BlockSpec((TM_PREP, n), row),
                   pl.BlockSpec((nv, TM_PREP), col)],
        out_shape=[jax.ShapeDtypeStruct((n, t), BF16), jax.ShapeDtypeStruct((t, n), BF16),
                   jax.ShapeDtypeStruct((nv, t), BF16)],
        compiler_params=_cparams(("arbitrary",)),
        name="mla_prep",
    )(cqkv, *tabs, *tabs_t, qg, kvg, wqt, wk, wvt, ones_rows)


def _attn_kernel(qt_ref, k_ref, vt_ref, o_ref, m_ref, acc_ref):
    i = pl.program_id(1)
    key_chunk = lax.broadcasted_iota(jnp.int32, (TK, TQ), 0) // CHUNK
    qry_chunk = lax.broadcasted_iota(jnp.int32, (TK, TQ), 1) // CHUNK
    diag_mask = key_chunk <= qry_chunk

    def scores(h, kt):
        k0 = pl.multiple_of(kt * TK, TK)
        kk = k_ref[pl.ds(k0, TK), h * HEAD_BLOCK:(h + 1) * HEAD_BLOCK]
        return jnp.dot(kk, qt_ref[h * HEAD_BLOCK:(h + 1) * HEAD_BLOCK, :], preferred_element_type=F32)

    def values_t(h, kt):
        k0 = pl.multiple_of(kt * TK, TK)
        return vt_ref[h * VT_BLOCK:(h + 1) * VT_BLOCK, pl.ds(k0, TK)]

    def first_update(h, kt, s):
        s = jnp.where(diag_mask, s, -jnp.inf)
        m = jnp.max(s, axis=0, keepdims=True)
        p = jnp.exp2(s - m)
        m_ref[h] = jnp.broadcast_to(m, (SUBLANES, TQ))
        acc_ref[h] = jnp.dot(values_t(h, kt), p.astype(BF16), preferred_element_type=F32)

    def update(h, kt, s):
        m_old = m_ref[h]
        m_new = jnp.maximum(m_old, jnp.max(s, axis=0, keepdims=True))
        alpha = jnp.exp2(m_old - m_new)
        p = jnp.exp2(s - m_new[0:1, :])
        m_ref[h] = m_new
        acc_ref[h] = alpha[0:1, :] * acc_ref[h] + jnp.dot(values_t(h, kt), p.astype(BF16),
                                                         preferred_element_type=F32)

    def all_heads(kt, upd):
        pending = {}
        for idx in range(N_HEADS + ATTN_LOOKAHEAD):
            if idx < N_HEADS:
                pending[idx] = scores(idx, kt)
            if idx >= ATTN_LOOKAHEAD:
                upd(idx - ATTN_LOOKAHEAD, kt, pending.pop(idx - ATTN_LOOKAHEAD))

    all_heads(i, first_update)

    def step(kt, carry):
        all_heads(kt, update)
        return carry

    lax.fori_loop(0, i, step, 0)

    outs = [acc_ref[h, 0:V_HEAD, :] * (1.0 / acc_ref[h, V_HEAD:V_HEAD + 1, :]) for h in range(N_HEADS)]
    o_ref[...] = jnp.concatenate(outs, axis=0).T.astype(BF16)


def _attn_call(qt, k, vt, batch, seq):
    n = N_HEADS * HEAD_BLOCK
    nq = seq // TQ
    return pl.pallas_call(
        _attn_kernel,
        grid=(batch, nq),
        in_specs=[pl.BlockSpec((n, TQ), lambda b, i: (0, b * nq + i)),
                  pl.BlockSpec((seq, n), lambda b, i: (b, 0)),
                  pl.BlockSpec((N_HEADS * VT_BLOCK, seq), lambda b, i: (0, b))],
        out_specs=pl.BlockSpec((TQ, N_HEADS * V_HEAD), lambda b, i: (b * nq + i, 0)),
        out_shape=jax.ShapeDtypeStruct((k.shape[0], N_HEADS * V_HEAD), BF16),
        scratch_shapes=[pltpu.VMEM((N_HEADS, SUBLANES, TQ), F32),
                        pltpu.VMEM((N_HEADS, VT_BLOCK, TQ), F32)],
        compiler_params=_cparams(("arbitrary", "arbitrary")),
        name="attn",
    )(qt, k, vt)


def _store_token_major(ref, x):
    rows = x.shape[0]
    for j in range(x.shape[1] // LANES):
        ref[pl.ds(j, rows, stride=SUBLANES), :] = x[:, j * LANES:(j + 1) * LANES]


def _load_token_major(ref, rows, d):
    return jnp.concatenate([ref[pl.ds(j, rows, stride=SUBLANES), :] for j in range(d // LANES)], axis=-1)


def _merge_kernel(alpha, ap_ref, ac_ref, at_ref, gl_ref, h_ref, p_ref, pp_ref, cp_ref, mp_ref, bg_ref,
                  wo_ref, g_ref, b_ref, pproj_ref, pgate_ref, wrh_ref, wrl_ref,
                  h1_ref, e_ref, lg_ref):
    d = h_ref.shape[1]
    merged = None
    for br, (a_ref, w_ref) in enumerate(((ap_ref, pp_ref), (ac_ref, cp_ref), (at_ref, mp_ref))):
        y = jnp.dot(a_ref[...], w_ref[...], preferred_element_type=F32)
        gate = jax.nn.sigmoid(gl_ref[:, br * d:(br + 1) * d].astype(F32) + bg_ref[br:br + 1, :])
        merged = gate * y if merged is None else merged + gate * y
    y = jnp.dot(merged.astype(BF16), wo_ref[...], preferred_element_type=F32)
    h1 = _layer_norm(alpha * h_ref[...] + y, g_ref[...], b_ref[...])
    h1b = h1.astype(BF16)
    e = jnp.dot(p_ref[...].astype(BF16), pproj_ref[...], preferred_element_type=F32)
    e = e * jax.nn.sigmoid(jnp.dot(h1b, pgate_ref[...], preferred_element_type=F32))
    e_ref[...] = e.astype(BF16)
    h1l = (h1 - h1b.astype(F32)).astype(BF16)
    nt = (((1,), (1,)), ((), ()))
    lg = lax.dot_general(wrh_ref[...], h1b, nt, preferred_element_type=F32)
    lg = lg + lax.dot_general(wrh_ref[...], h1l, nt, preferred_element_type=F32)
    lg = lg + lax.dot_general(wrl_ref[...], h1b, nt, preferred_element_type=F32)
    lg_ref[...] = lg
    _store_token_major(h1_ref, h1)


def _merge_call(alpha, ap, ac, at, gl, h, p, pp, cp, mp, bg, wo, g, b, pproj, pgate, wrh, wrl):
    t, d = h.shape
    tm = TM_MERGE
    row = lambda i: (i, 0)
    ins = [ap, ac, at, gl, h, p]
    in_specs = [pl.BlockSpec((tm, a.shape[1]), row) for a in ins]
    consts = [pp, cp, mp, bg, wo, g, b, pproj, pgate, wrh, wrl]
    in_specs += [_full(a.shape) for a in consts]
    return pl.pallas_call(
        functools.partial(_merge_kernel, alpha),
        grid=(t // tm,),
        in_specs=in_specs,
        out_specs=[pl.BlockSpec((tm * SUBLANES, LANES), row), pl.BlockSpec((tm, d), row),
                   pl.BlockSpec((N_EXPERTS, tm), lambda i: (0, i))],
        out_shape=[jax.ShapeDtypeStruct((t * SUBLANES, LANES), F32), jax.ShapeDtypeStruct((t, d), BF16),
                   jax.ShapeDtypeStruct((N_EXPERTS, t), F32)],
        compiler_params=_cparams(("arbitrary",)),
        name="merge",
    )(*ins, *consts)


def _route_kernel(lg_ref, bias_ref, cls_ref, wa_ref, wb_ref):
    aff = [jax.nn.sigmoid(lg_ref[e]) for e in range(N_EXPERTS)]
    sel = [aff[e] + bias_ref[e] for e in range(N_EXPERTS)]
    n = EXPERTS_PER_GROUP

    def top2_sum(vals):
        best = None
        for a, b in PAIRS:
            s = vals[a] + vals[b]
            best = s if best is None else jnp.maximum(best, s)
        return best

    grp = jnp.zeros(aff[0].shape, jnp.int32)
    best = top2_sum(sel[0:n])
    for g in range(1, N_GROUPS):
        sc = top2_sum(sel[g * n:(g + 1) * n])
        better = sc > best
        grp = jnp.where(better, g, grp)
        best = jnp.where(better, sc, best)
    vs, afs = [], []
    for j in range(n):
        v, a = sel[j], aff[j]
        for g in range(1, N_GROUPS):
            v = jnp.where(grp == g, sel[g * n + j], v)
            a = jnp.where(grp == g, aff[g * n + j], a)
        vs.append(v)
        afs.append(a)
    first = jnp.zeros_like(grp)
    fv = vs[0]
    for j in range(1, n):
        better = vs[j] > fv
        first = jnp.where(better, j, first)
        fv = jnp.where(better, vs[j], fv)
    second = jnp.full_like(grp, -1)
    sv = jnp.full_like(fv, -jnp.inf)
    for j in range(n):
        better = (first != j) & ((second < 0) | (vs[j] > sv))
        second = jnp.where(better, j, second)
        sv = jnp.where(better, vs[j], sv)
    lo = jnp.minimum(first, second)
    hi = jnp.maximum(first, second)
    a_lo, a_hi = afs[0], afs[0]
    for j in range(1, n):
        a_lo = jnp.where(lo == j, afs[j], a_lo)
        a_hi = jnp.where(hi == j, afs[j], a_hi)
    pair = jnp.zeros_like(grp)
    for idx, (a, b) in enumerate(PAIRS):
        pair = jnp.where((lo == a) & (hi == b), idx, pair)
    tot = a_lo + a_hi
    cls_ref[...] = grp * len(PAIRS) + pair
    wa_ref[...] = a_lo / tot
    wb_ref[...] = a_hi / tot


def _route_call(lg3, bias):
    _, rows, lanes = lg3.shape
    blk = pl.BlockSpec((ROUTE_ROWS, lanes), lambda i: (i, 0))
    return pl.pallas_call(
        _route_kernel,
        grid=(rows // ROUTE_ROWS,),
        in_specs=[pl.BlockSpec((N_EXPERTS, ROUTE_ROWS, lanes), lambda i: (0, i, 0)),
                  pl.BlockSpec(memory_space=pltpu.SMEM)],
        out_specs=[blk, blk, blk],
        out_shape=[jax.ShapeDtypeStruct((rows, lanes), jnp.int32),
                   jax.ShapeDtypeStruct((rows, lanes), F32), jax.ShapeDtypeStruct((rows, lanes), F32)],
        compiler_params=_cparams(("arbitrary",)),
        name="route",
    )(lg3, bias)


GATHER_UNROLL = 8


def _moe_kernel(ta_ref, tb_ref, nused_ref, valid_ref, tok_ref,
                h_hbm, wa_ref, wb_ref, upa_ref, upb_ref, dna_ref, dnb_ref,
                m_hbm, xbuf, ybuf, gsem, ssem):
    i = pl.program_id(0)
    n_used = nused_ref[0]
    slot = i % 2
    rows = MOE_TILE * SUBLANES

    def row_copy_in(tile, s, r):
        tok = tok_ref[tile * MOE_TILE + r]
        return pltpu.make_async_copy(
            h_hbm.at[pl.ds(pl.multiple_of(tok * SUBLANES, SUBLANES), SUBLANES), :],
            xbuf.at[s, pl.ds(pl.multiple_of(r * SUBLANES, SUBLANES), SUBLANES), :],
            gsem.at[s])

    def row_copy_out(tile, s, r):
        tok = tok_ref[tile * MOE_TILE + r]
        return pltpu.make_async_copy(
            ybuf.at[s, pl.ds(pl.multiple_of(r * SUBLANES, SUBLANES), SUBLANES), :],
            m_hbm.at[pl.ds(pl.multiple_of(tok * SUBLANES, SUBLANES), SUBLANES), :],
            ssem.at[s])

    def start_rows(make, tile, s, count):
        def body8(c, carry):
            for u in range(GATHER_UNROLL):
                make(tile, s, c * GATHER_UNROLL + u).start()
            return carry

        def body1(r, carry):
            make(tile, s, r).start()
            return carry

        full = count // GATHER_UNROLL
        lax.fori_loop(0, full, body8, 0)
        lax.fori_loop(full * GATHER_UNROLL, count, body1, 0)

    def wait_gather(s):
        pltpu.make_async_copy(h_hbm.at[pl.ds(0, rows), :], xbuf.at[s], gsem.at[s]).wait()

    def wait_scatter(tile, s):
        n = pl.multiple_of(valid_ref[tile] * SUBLANES, SUBLANES)
        pltpu.make_async_copy(ybuf.at[s, pl.ds(0, n), :], m_hbm.at[pl.ds(0, n), :], ssem.at[s]).wait()

    @pl.when(i == 0)
    def _():
        start_rows(row_copy_in, 0, 0, MOE_TILE)

    @pl.when(i + 1 < n_used)
    def _():
        start_rows(row_copy_in, i + 1, 1 - slot, MOE_TILE)

    @pl.when(i < n_used)
    def _():
        wait_gather(slot)
        x = _load_token_major(xbuf.at[slot], MOE_TILE, upa_ref.shape[1]).astype(BF16)

        def ffn(up_ref, dn_ref):
            gu = jnp.dot(x, up_ref[0], preferred_element_type=F32)
            hid = jax.nn.silu(gu[:, :D_EXPERT]) * gu[:, D_EXPERT:]
            return jnp.dot(hid.astype(BF16), dn_ref[0], preferred_element_type=F32)

        y = ffn(upa_ref, dna_ref) * wa_ref[...] + ffn(upb_ref, dnb_ref) * wb_ref[...]

        @pl.when(i >= 2)
        def _():
            wait_scatter(i - 2, slot)

        _store_token_major(ybuf.at[slot], y)
        start_rows(row_copy_out, i, slot, valid_ref[i])

        @pl.when(i == n_used - 1)
        def _():
            @pl.when(i >= 1)
            def _():
                wait_scatter(i - 1, 1 - slot)
            wait_scatter(i, slot)


def _moe_call(tile_a, tile_b, n_used, tile_valid, slot_tok, h1_tm, slot_wa, slot_wb, w_up, w_down, n_tiles, t):
    d = w_up.shape[1]
    wspec = pl.BlockSpec((MOE_TILE, 1), lambda i, *_: (i, 0))
    grid_spec = pltpu.PrefetchScalarGridSpec(
        num_scalar_prefetch=5,
        grid=(n_tiles,),
        in_specs=[pl.BlockSpec(memory_space=pl.ANY), wspec, wspec,
                  pl.BlockSpec((1, d, 2 * D_EXPERT), lambda i, ta, tb, *_: (ta[i], 0, 0)),
                  pl.BlockSpec((1, d, 2 * D_EXPERT), lambda i, ta, tb, *_: (tb[i], 0, 0)),
                  pl.BlockSpec((1, D_EXPERT, d), lambda i, ta, tb, *_: (ta[i], 0, 0)),
                  pl.BlockSpec((1, D_EXPERT, d), lambda i, ta, tb, *_: (tb[i], 0, 0))],
        out_specs=pl.BlockSpec(memory_space=pl.ANY),
        scratch_shapes=[pltpu.VMEM((2, MOE_TILE * SUBLANES, LANES), F32),
                        pltpu.VMEM((2, MOE_TILE * SUBLANES, LANES), F32),
                        pltpu.SemaphoreType.DMA((2,)), pltpu.SemaphoreType.DMA((2,))])
    return pl.pallas_call(
        _moe_kernel,
        grid_spec=grid_spec,
        out_shape=jax.ShapeDtypeStruct((t * SUBLANES, LANES), F32),
        compiler_params=_cparams(("arbitrary",)),
        name="moe",
    )(tile_a, tile_b, n_used, tile_valid, slot_tok, h1_tm, slot_wa, slot_wb, w_up, w_up, w_down, w_down)


def _final_kernel(alpha, h1_ref, m_ref, e_ref, g_ref, b_ref, o_ref):
    rows, d = o_ref.shape
    h1 = _load_token_major(h1_ref, rows, d)
    m = _load_token_major(m_ref, rows, d)
    o_ref[...] = _layer_norm(alpha * h1 + m + e_ref[...].astype(F32), g_ref[...], b_ref[...])


def _final_call(alpha, h1_tm, m_tm, e, g, b):
    t, d = e.shape
    tm = TM_FINAL
    row = lambda i: (i, 0)
    tmaj = pl.BlockSpec((tm * SUBLANES, LANES), row)
    return pl.pallas_call(
        functools.partial(_final_kernel, alpha),
        grid=(t // tm,),
        in_specs=[tmaj, tmaj, pl.BlockSpec((tm, d), row), _full(g.shape), _full(b.shape)],
        out_specs=pl.BlockSpec((tm, d), row),
        out_shape=jax.ShapeDtypeStruct((t, d), F32),
        compiler_params=_cparams(("arbitrary",)),
        name="final_ln",
    )(h1_tm, m_tm, e, g, b)


def _pack_w_in(w_in):
    off_q = OFF_CONV + 2 * D_CONV
    off_kr = off_q + Q_LORA + KV_LORA
    off_gate = off_kr + QK_ROPE
    kr = jnp.pad(w_in[..., off_kr:off_gate], ((0, 0), (0, 0), (ROPE_LANE0, HEAD_BLOCK - ROPE_LANE0 - QK_ROPE)))
    return jnp.concatenate([w_in[..., :off_kr], kr, w_in[..., off_gate:]], axis=-1).astype(BF16)


def _pack_heads(w, lo, width, stride, lane0):
    blocks = []
    for h in range(N_HEADS):
        l0 = lane0(h)
        blocks.append(jnp.pad(w[..., h * stride + lo:h * stride + lo + width],
                              ((0, 0), (0, 0), (l0, HEAD_BLOCK - l0 - width))))
    return jnp.concatenate(blocks, axis=-1).astype(BF16)


def _rope_tables(positions):
    inv_freq = jnp.power(ROPE_THETA, -jnp.arange(0, QK_ROPE, 2, dtype=F32) / QK_ROPE)
    ang = positions.astype(F32).reshape(-1, 1) * inv_freq
    cos, sin = jnp.cos(ang), jnp.sin(ang)
    t = ang.shape[0]
    ones_lo = jnp.ones((t, ROPE_LANE0), F32)
    ones_hi = jnp.ones((t, HEAD_BLOCK - ROPE_LANE0 - QK_ROPE), F32)
    zeros_lo = jnp.zeros((t, ROPE_LANE0), F32)
    zeros_half = jnp.zeros((t, HALF_ROPE), F32)
    zeros_hi = jnp.zeros((t, HEAD_BLOCK - ROPE_LANE0 - QK_ROPE), F32)
    c = jnp.concatenate([ones_lo, cos, cos, ones_hi], axis=-1)
    s1 = jnp.concatenate([zeros_lo, -sin, zeros_half, zeros_hi], axis=-1)
    s2 = jnp.concatenate([zeros_lo, zeros_half, sin, zeros_hi], axis=-1)
    return c, s1, s2


def _routing_plan(cls, wa, wb, n_tiles):
    t = cls.shape[0]
    n_slots = n_tiles * MOE_TILE
    order = jnp.argsort(cls, stable=True).astype(jnp.int32)
    s_cls = cls[order]
    counts = jnp.zeros((N_CLASSES,), jnp.int32).at[cls].add(1)
    start = jnp.cumsum(counts) - counts
    padded = (counts + MOE_TILE - 1) // MOE_TILE * MOE_TILE
    pad_end = jnp.cumsum(padded)
    pad_start = pad_end - padded
    dest = pad_start[s_cls] + jnp.arange(t, dtype=jnp.int32) - start[s_cls]
    slot_tok = jnp.zeros((n_slots,), jnp.int32).at[dest].set(order)
    slot_wa = jnp.zeros((n_slots,), F32).at[dest].set(wa[order])
    slot_wb = jnp.zeros((n_slots,), F32).at[dest].set(wb[order])
    tile_row0 = jnp.arange(n_tiles, dtype=jnp.int32) * MOE_TILE
    tile_cls = jnp.minimum(jnp.searchsorted(pad_end, tile_row0, side='right'), N_CLASSES - 1).astype(jnp.int32)
    tile_valid = jnp.clip(pad_start[tile_cls] + counts[tile_cls] - tile_row0, 0, MOE_TILE).astype(jnp.int32)
    pair_lo = jnp.array([p[0] for p in PAIRS], jnp.int32)
    pair_hi = jnp.array([p[1] for p in PAIRS], jnp.int32)
    grp = tile_cls // len(PAIRS)
    tile_a = grp * EXPERTS_PER_GROUP + pair_lo[tile_cls % len(PAIRS)]
    tile_b = grp * EXPERTS_PER_GROUP + pair_hi[tile_cls % len(PAIRS)]
    n_used = (pad_end[-1] // MOE_TILE).astype(jnp.int32).reshape(1)
    return tile_a, tile_b, n_used, tile_valid, slot_tok, slot_wa.reshape(-1, 1), slot_wb.reshape(-1, 1)


def kernel(x, p, positions, ln_in_g, ln_in_b, w_in, b_gate, pool_w, pool_scale, pool_proj, conv_dw, conv_b, conv_ln_g, conv_ln_b, conv_proj, q_norm_g, w_uq, kv_norm_g, w_ukv, mla_proj, w_out, ln1_g, ln1_b, w_router, router_bias, exp_w_up, exp_w_down, ple_proj, ple_gate, ln2_g, ln2_b):
    batch, seq, d = x.shape
    depth = w_in.shape[0]
    t = batch * seq
    alpha = float((2 * depth) ** 0.25)
    n_tiles = (t + N_CLASSES * (MOE_TILE - 1)) // MOE_TILE + 1

    w_in_p = _pack_w_in(w_in)
    pool_w_b = pool_w.astype(BF16)
    pool_proj_b, conv_proj_b, mla_proj_b = pool_proj.astype(BF16), conv_proj.astype(BF16), mla_proj.astype(BF16)
    qk = QK_NOPE + QK_ROPE
    wq_p = (_pack_heads(w_uq, 0, QK_NOPE, qk, lambda h: 0)
            + _pack_heads(w_uq, QK_NOPE, QK_ROPE, qk, lambda h: ROPE_LANE0))
    wqt_p = jnp.swapaxes(wq_p, 1, 2)
    wk_p = _pack_heads(w_ukv, 0, QK_NOPE, QK_NOPE + V_HEAD, lambda h: 0)
    kv_w = QK_NOPE + V_HEAD
    wvt_p = jnp.swapaxes(jnp.concatenate(
        [jnp.pad(w_ukv[..., h * kv_w + QK_NOPE:(h + 1) * kv_w], ((0, 0), (0, 0), (0, BF16_ROWS)))
         for h in range(N_HEADS)], axis=-1), 1, 2).astype(BF16)
    ones_rows = jnp.tile(jnp.concatenate([jnp.zeros((V_HEAD, 1), F32), jnp.ones((BF16_ROWS, 1), F32)]),
                         (N_HEADS, 1))
    w_out_b, ple_proj_b, ple_gate_b = w_out.astype(BF16), ple_proj.astype(BF16), ple_gate.astype(BF16)
    w_up_b, w_down_b = exp_w_up.astype(BF16), exp_w_down.astype(BF16)
    wr_t = w_router.T
    wr_hi = wr_t.astype(BF16)
    wr_lo = (wr_t - wr_hi.astype(F32)).astype(BF16)
    rope_tabs = _rope_tables(positions)
    rope_tabs_t = tuple(a.T for a in rope_tabs)
    p2 = p.reshape(depth, t, -1)

    h = _ln_call(x.reshape(t, d), ln_in_g.reshape(1, d), ln_in_b.reshape(1, d))
    for i in range(depth):
        u_pool, u_conv, cqkv, glog = _in_proj_call(h, w_in_p[i])
        a_pool = _pool_call(u_pool, pool_w_b[i], pool_scale[i].reshape(1, -1), batch, seq)
        a_conv = _conv_call(u_conv, conv_dw[i], conv_b[i].reshape(1, -1), conv_ln_g[i].reshape(1, -1),
                            conv_ln_b[i].reshape(1, -1), batch, seq)
        qt, k, vt = _mla_prep_call(cqkv, rope_tabs, rope_tabs_t, q_norm_g[i].reshape(1, -1),
                                   kv_norm_g[i].reshape(1, -1), wqt_p[i], wk_p[i], wvt_p[i], ones_rows)
        a_attn = _attn_call(qt, k, vt, batch, seq)
        h1_tm, e, logits = _merge_call(alpha, a_pool, a_conv, a_attn, glog, h, p2[i],
                                       pool_proj_b[i], conv_proj_b[i], mla_proj_b[i], b_gate[i], w_out_b[i],
                                       ln1_g[i].reshape(1, d), ln1_b[i].reshape(1, d),
                                       ple_proj_b[i], ple_gate_b[i], wr_hi, wr_lo)
        cls, wa, wb = _route_call(logits.reshape(N_EXPERTS, t // LANES, LANES), router_bias)
        plan = _routing_plan(cls.reshape(t), wa.reshape(t), wb.reshape(t), n_tiles)
        m_tm = _moe_call(*plan[:5], h1_tm, plan[5], plan[6], w_up_b[i], w_down_b[i], n_tiles, t)
        h = _final_call(alpha, h1_tm, m_tm, e, ln2_g[i].reshape(1, d), ln2_b[i].reshape(1, d))
    return h.reshape(batch, seq, d)
```

```python
import functools

import jax
import jax.numpy as jnp
from jax import lax
from jax.experimental import pallas as pl
from jax.experimental.pallas import tpu as pltpu

F32 = jnp.float32
BF16 = jnp.bfloat16

CHUNK = 64
POOL_WINDOWS = (2, 4, 8, 16)
POOL_GW = 128
D_POOL = 512
D_CONV = 512
CONV_WIDTH = 31
N_HEADS = 8
QK_NOPE = 64
QK_ROPE = 32
V_HEAD = 64
Q_LORA = 384
KV_LORA = 256
ROPE_THETA = 10000.0
N_EXPERTS = 16
N_GROUPS = 4
EXPERTS_PER_GROUP = 4
D_EXPERT = 512
LN_EPS = 1e-5
RMS_EPS = 1e-6

LANES = 128
SUBLANES = 8
HEAD_BLOCK = LANES
ROPE_LANE0 = QK_NOPE
HALF_ROPE = QK_ROPE // 2
BF16_ROWS = 16
VT_BLOCK = V_HEAD + BF16_ROWS
LOG2_E = 1.4426950408889634

TM_PROJ = 512
TM_PREP = 512
TM_MERGE = 512
MERGE_SUB = 256
TM_FINAL = 512
TQ = 256
TK = 256
ATTN_LOOKAHEAD = 3
POOL_CHUNK = 256
POOL_HALO = 16
CONV_CHUNK = 64
CONV_HALO = 32
MOE_TILE = 256
ROUTE_ROWS = 8

PAIRS = ((0, 1), (0, 2), (0, 3), (1, 2), (1, 3), (2, 3))
N_CLASSES = N_GROUPS * len(PAIRS)

VMEM_LIMIT = 56 * 1024 * 1024


def _cparams(sem):
    return pltpu.CompilerParams(dimension_semantics=sem, vmem_limit_bytes=VMEM_LIMIT)


def _layer_norm(x, g, b):
    mu = jnp.mean(x, axis=-1, keepdims=True)
    xc = x - mu
    var = jnp.mean(xc * xc, axis=-1, keepdims=True)
    return xc * lax.rsqrt(var + LN_EPS) * g + b


def _rms_norm(x, g):
    ms = jnp.mean(x * x, axis=-1, keepdims=True)
    return x * lax.rsqrt(ms + RMS_EPS) * g


def _full(shape):
    n = len(shape)
    return pl.BlockSpec(shape, lambda *_: (0,) * n)


def _layer_full(stacked, layer):
    n = stacked.ndim - 1
    return pl.BlockSpec((None,) + stacked.shape[1:], lambda *_: (layer,) + (0,) * n)


def _ln_kernel(x_ref, g_ref, b_ref, o_ref):
    o_ref[...] = _layer_norm(x_ref[...], g_ref[...], b_ref[...])


def _ln_call(x, g, b):
    t, d = x.shape
    return pl.pallas_call(
        _ln_kernel,
        grid=(t // TM_FINAL,),
        in_specs=[pl.BlockSpec((TM_FINAL, d), lambda i: (i, 0)), _full((1, d)), _full((1, d))],
        out_specs=pl.BlockSpec((TM_FINAL, d), lambda i: (i, 0)),
        out_shape=jax.ShapeDtypeStruct((t, d), F32),
        compiler_params=_cparams(("arbitrary",)),
        name="ln_in",
    )(x, g, b)


OFF_CONV = D_POOL
OFF_CQKV = OFF_CONV + 2 * D_CONV
W_CQKV = Q_LORA + KV_LORA + HEAD_BLOCK
OFF_GATE = OFF_CQKV + W_CQKV
N_COLS_CHUNK = 512


def _in_proj_kernel(h_ref, w_ref, pool_ref, conv_ref, cqkv_ref, gate_ref):
    x = h_ref[...].astype(BF16)

    def mm(lo, hi):
        return jnp.dot(x, w_ref[:, lo:hi], preferred_element_type=F32).astype(BF16)

    pool_ref[...] = mm(0, OFF_CONV)
    for c in range(2 * D_CONV // N_COLS_CHUNK):
        conv_ref[:, c * N_COLS_CHUNK:(c + 1) * N_COLS_CHUNK] = mm(
            OFF_CONV + c * N_COLS_CHUNK, OFF_CONV + (c + 1) * N_COLS_CHUNK)
    cqkv_ref[...] = mm(OFF_CQKV, OFF_GATE)
    for c in range(gate_ref.shape[1] // N_COLS_CHUNK):
        gate_ref[:, c * N_COLS_CHUNK:(c + 1) * N_COLS_CHUNK] = mm(
            OFF_GATE + c * N_COLS_CHUNK, OFF_GATE + (c + 1) * N_COLS_CHUNK)


def _in_proj_call(h, w, layer):
    t, d = h.shape
    n = w.shape[-1]
    n_gate = n - OFF_GATE
    row = lambda i: (i, 0)
    return pl.pallas_call(
        _in_proj_kernel,
        grid=(t // TM_PROJ,),
        in_specs=[pl.BlockSpec((TM_PROJ, d), row), _layer_full(w, layer)],
        out_specs=[pl.BlockSpec((TM_PROJ, D_POOL), row), pl.BlockSpec((TM_PROJ, 2 * D_CONV), row),
                   pl.BlockSpec((TM_PROJ, W_CQKV), row), pl.BlockSpec((TM_PROJ, n_gate), row)],
        out_shape=[jax.ShapeDtypeStruct((t, D_POOL), BF16), jax.ShapeDtypeStruct((t, 2 * D_CONV), BF16),
                   jax.ShapeDtypeStruct((t, W_CQKV), BF16), jax.ShapeDtypeStruct((t, n_gate), BF16)],
        compiler_params=_cparams(("arbitrary",)),
        name="in_proj",
    )(h, w)


def _pool_kernel(u_ref, w_ref, scale_ref, o_ref, buf):
    s = o_ref.shape[0]
    buf[0:POOL_HALO, :] = jnp.zeros((POOL_HALO, D_POOL), F32)

    def to_f32(c, carry):
        r = pl.multiple_of(c * POOL_CHUNK, POOL_CHUNK)
        buf[pl.ds(POOL_HALO + r, POOL_CHUNK), :] = u_ref[pl.ds(r, POOL_CHUNK), :].astype(F32)
        return carry

    lax.fori_loop(0, s // POOL_CHUNK, to_f32, 0)

    def body(c, carry):
        r0 = pl.multiple_of(c * POOL_CHUNK, POOL_CHUNK)
        t = r0 + lax.broadcasted_iota(jnp.int32, (POOL_CHUNK, 1), 0)
        for g, w in enumerate(POOL_WINDOWS):
            cols = slice(g * POOL_GW, (g + 1) * POOL_GW)
            xw = buf[pl.ds(r0, POOL_CHUNK + POOL_HALO), cols]
            acc = xw
            k = 1
            while k < w:
                acc = acc + pltpu.roll(acc, k, axis=0)
                k *= 2
            cnt = jnp.minimum(t + 1, w).astype(F32)
            mixed = acc[POOL_HALO:] / cnt - xw[POOL_HALO:]
            y = jnp.dot(mixed.astype(BF16), w_ref[g], preferred_element_type=F32) * scale_ref[:, cols]
            o_ref[pl.ds(r0, POOL_CHUNK), cols] = y.astype(BF16)
        return carry

    lax.fori_loop(0, s // POOL_CHUNK, body, 0)


def _pool_call(u, w, scale, batch, seq):
    return pl.pallas_call(
        _pool_kernel,
        grid=(batch,),
        in_specs=[pl.BlockSpec((seq, D_POOL), lambda b: (b, 0)),
                  _full(w.shape), _full(scale.shape)],
        out_specs=pl.BlockSpec((seq, D_POOL), lambda b: (b, 0)),
        out_shape=jax.ShapeDtypeStruct(u.shape, BF16),
        scratch_shapes=[pltpu.VMEM((seq + POOL_HALO, D_POOL), F32)],
        compiler_params=_cparams(("arbitrary",)),
        name="pool",
    )(u, w, scale)


GLU_CHUNK = 256


def _conv_kernel(u_ref, dw_ref, cb_ref, g_ref, b_ref, o_ref, zs):
    s = o_ref.shape[0]
    zs[0:CONV_HALO, :] = jnp.zeros((CONV_HALO, D_CONV), F32)

    def glu(c, carry):
        r = pl.multiple_of(c * GLU_CHUNK, GLU_CHUNK)
        a = u_ref[pl.ds(r, GLU_CHUNK), 0:D_CONV].astype(F32)
        gate = u_ref[pl.ds(r, GLU_CHUNK), D_CONV:2 * D_CONV].astype(F32)
        zs[pl.ds(CONV_HALO + r, GLU_CHUNK), :] = a * jax.nn.sigmoid(gate)
        return carry

    lax.fori_loop(0, s // GLU_CHUNK, glu, 0)

    def body(c, carry):
        r0 = pl.multiple_of(c * CONV_CHUNK, CONV_CHUNK)
        parts = []
        for cg in range(D_CONV // LANES):
            cols = slice(cg * LANES, (cg + 1) * LANES)
            win = zs[pl.ds(r0, CONV_CHUNK + CONV_HALO), cols]
            acc = jnp.zeros((CONV_CHUNK, LANES), F32) + cb_ref[:, cols]
            for sub in range(SUBLANES):
                shifted = win if sub == 0 else pltpu.roll(win, sub, axis=0)
                for a in range(CONV_HALO // SUBLANES):
                    lag = SUBLANES * a + sub
                    if lag >= CONV_WIDTH:
                        continue
                    k = CONV_WIDTH - 1 - lag
                    lo = CONV_HALO - SUBLANES * a
                    acc = acc + dw_ref[k:k + 1, cols] * shifted[lo:lo + CONV_CHUNK]
            parts.append(acc)
        y = jnp.concatenate(parts, axis=-1)
        y = _layer_norm(y, g_ref[...], b_ref[...])
        y = y * jax.nn.sigmoid(y)
        o_ref[pl.ds(r0, CONV_CHUNK), :] = y.astype(BF16)
        return carry

    lax.fori_loop(0, s // CONV_CHUNK, body, 0)


def _conv_call(u, dw, cb, g, b, batch, seq):
    return pl.pallas_call(
        _conv_kernel,
        grid=(batch,),
        in_specs=[pl.BlockSpec((seq, 2 * D_CONV), lambda i: (i, 0)),
                  _full(dw.shape), _full(cb.shape), _full(g.shape), _full(b.shape)],
        out_specs=pl.BlockSpec((seq, D_CONV), lambda i: (i, 0)),
        out_shape=jax.ShapeDtypeStruct((u.shape[0], D_CONV), BF16),
        scratch_shapes=[pltpu.VMEM((seq + CONV_HALO, D_CONV), F32)],
        compiler_params=_cparams(("arbitrary",)),
        name="conv",
    )(u, dw, cb, g, b)


def _rope_block(x, c, s1, s2):
    return x * c + pltpu.roll(x, LANES - HALF_ROPE, axis=1) * s1 + pltpu.roll(x, HALF_ROPE, axis=1) * s2


def _rope_block_t(x, c, s1, s2):
    return x * c + pltpu.roll(x, HEAD_BLOCK - HALF_ROPE, axis=0) * s1 + pltpu.roll(x, HALF_ROPE, axis=0) * s2


def _mla_prep_kernel(cqkv_ref, c_ref, s1_ref, s2_ref, ct_ref, s1t_ref, s2t_ref, qg_ref, kvg_ref,
                     wqt_ref, wk_ref, wvt_ref, ones_ref, qt_ref, k_ref, vt_ref):
    cq = _rms_norm(cqkv_ref[:, 0:Q_LORA].astype(F32), qg_ref[...]).astype(BF16)
    ckv = _rms_norm(cqkv_ref[:, Q_LORA:Q_LORA + KV_LORA].astype(F32), kvg_ref[...]).astype(BF16)
    kr = _rope_block(cqkv_ref[:, Q_LORA + KV_LORA:W_CQKV].astype(F32), c_ref[...], s1_ref[...], s2_ref[...])
    scale = float((QK_NOPE + QK_ROPE) ** -0.5 * LOG2_E)
    nt = (((1,), (1,)), ((), ()))
    ct, s1t, s2t = ct_ref[...], s1t_ref[...], s2t_ref[...]
    for h in range(N_HEADS):
        cols = slice(h * HEAD_BLOCK, (h + 1) * HEAD_BLOCK)
        qt = lax.dot_general(wqt_ref[cols, :], cq, nt, preferred_element_type=F32) * scale
        qt_ref[cols, :] = _rope_block_t(qt, ct, s1t, s2t).astype(BF16)
        k = jnp.dot(ckv, wk_ref[:, cols], preferred_element_type=F32) + kr
        k_ref[:, cols] = k.astype(BF16)
    vt = lax.dot_general(wvt_ref[...], ckv, nt, preferred_element_type=F32) + ones_ref[...]
    vt_ref[...] = vt.astype(BF16)


def _mla_prep_call(cqkv, tabs, tabs_t, qg, kvg, wqt, wk, wvt, ones_rows):
    t = cqkv.shape[0]
    n = N_HEADS * HEAD_BLOCK
    nv = N_HEADS * VT_BLOCK
    row = lambda i: (i, 0)
    col = lambda i: (0, i)
    tab = pl.BlockSpec((TM_PREP, LANES), row)
    tab_t = pl.BlockSpec((LANES, TM_PREP), col)
    return pl.pallas_call(
        _mla_prep_kernel,
        grid=(t // TM_PREP,),
        in_specs=[pl.BlockSpec((TM_PREP, W_CQKV), row), tab, tab, tab, tab_t, tab_t, tab_t,
                  _full(qg.shape), _full(kvg.shape), _full(wqt.shape), _full(wk.shape), _full(wvt.shape),
                  _full(ones_rows.shape)],
        out_specs=[pl.BlockSpec((n, TM_PREP), col), pl.BlockSpec((TM_PREP, n), row),
                   pl.BlockSpec((nv, TM_PREP), col)],
        out_shape=[jax.ShapeDtypeStruct((n, t), BF16), jax.ShapeDtypeStruct((t, n), BF16),
                   jax.ShapeDtypeStruct((nv, t), BF16)],
        compiler_params=_cparams(("arbitrary",)),
        name="mla_prep",
    )(cqkv, *tabs, *tabs_t, qg, kvg, wqt, wk, wvt, ones_rows)


def _attn_kernel(qt_ref, k_ref, vt_ref, o_ref, m_ref, acc_ref):
    i = pl.program_id(1)
    key_chunk = lax.broadcasted_iota(jnp.int32, (TK, TQ), 0) // CHUNK
    qry_chunk = lax.broadcasted_iota(jnp.int32, (TK, TQ), 1) // CHUNK
    diag_mask = key_chunk <= qry_chunk

    def scores(h, kt):
        k0 = pl.multiple_of(kt * TK, TK)
        kk = k_ref[pl.ds(k0, TK), h * HEAD_BLOCK:(h + 1) * HEAD_BLOCK]
        return jnp.dot(kk, qt_ref[h * HEAD_BLOCK:(h + 1) * HEAD_BLOCK, :], preferred_element_type=F32)

    def values_t(h, kt):
        k0 = pl.multiple_of(kt * TK, TK)
        return vt_ref[h * VT_BLOCK:(h + 1) * VT_BLOCK, pl.ds(k0, TK)]

    def first_update(h, kt, s):
        s = jnp.where(diag_mask, s, -jnp.inf)
        m = jnp.max(s, axis=0, keepdims=True)
        p = jnp.exp2(s - m)
        m_ref[h] = jnp.broadcast_to(m, (SUBLANES, TQ))
        acc_ref[h] = jnp.dot(values_t(h, kt), p.astype(BF16), preferred_element_type=F32)

    def update(h, kt, s):
        m_old = m_ref[h]
        m_new = jnp.maximum(m_old, jnp.max(s, axis=0, keepdims=True))
        alpha = jnp.exp2(m_old - m_new)
        p = jnp.exp2(s - m_new[0:1, :])
        m_ref[h] = m_new
        acc_ref[h] = alpha[0:1, :] * acc_ref[h] + jnp.dot(values_t(h, kt), p.astype(BF16),
                                                         preferred_element_type=F32)

    def all_heads(kt, upd):
        pending = {}
        for idx in range(N_HEADS + ATTN_LOOKAHEAD):
            if idx < N_HEADS:
                pending[idx] = scores(idx, kt)
            if idx >= ATTN_LOOKAHEAD:
                upd(idx - ATTN_LOOKAHEAD, kt, pending.pop(idx - ATTN_LOOKAHEAD))

    all_heads(i, first_update)

    def step(kt, carry):
        all_heads(kt, update)
        return carry

    lax.fori_loop(0, i, step, 0)

    outs = [acc_ref[h, 0:V_HEAD, :] * (1.0 / acc_ref[h, V_HEAD:V_HEAD + 1, :]) for h in range(N_HEADS)]
    o_ref[...] = jnp.concatenate(outs, axis=0).T.astype(BF16)


def _attn_call(qt, k, vt, batch, seq):
    n = N_HEADS * HEAD_BLOCK
    nq = seq // TQ
    return pl.pallas_call(
        _attn_kernel,
        grid=(batch, nq),
        in_specs=[pl.BlockSpec((n, TQ), lambda b, i: (0, b * nq + i)),
                  pl.BlockSpec((seq, n), lambda b, i: (b, 0)),
                  pl.BlockSpec((N_HEADS * VT_BLOCK, seq), lambda b, i: (0, b))],
        out_specs=pl.BlockSpec((TQ, N_HEADS * V_HEAD), lambda b, i: (b * nq + i, 0)),
        out_shape=jax.ShapeDtypeStruct((k.shape[0], N_HEADS * V_HEAD), BF16),
        scratch_shapes=[pltpu.VMEM((N_HEADS, SUBLANES, TQ), F32),
                        pltpu.VMEM((N_HEADS, VT_BLOCK, TQ), F32)],
        compiler_params=_cparams(("arbitrary", "arbitrary")),
        name="attn",
    )(qt, k, vt)


def _store_token_major(ref, x):
    rows = x.shape[0]
    for j in range(x.shape[1] // LANES):
        ref[pl.ds(j, rows, stride=SUBLANES), :] = x[:, j * LANES:(j + 1) * LANES]


def _load_token_major(ref, rows, d):
    return jnp.concatenate([ref[pl.ds(j, rows, stride=SUBLANES), :] for j in range(d // LANES)], axis=-1)


def _merge_kernel(alpha, ap_ref, ac_ref, at_ref, gl_ref, h_ref, p_ref, pp_ref, cp_ref, mp_ref, bg_ref,
                  wo_ref, g_ref, b_ref, pproj_ref, pgate_ref, wrh_ref, wrl_ref,
                  h1_ref, e_ref, lg_ref):
    d = h_ref.shape[1]
    sub = MERGE_SUB
    parts = [pl.ds(k * sub, sub) for k in range(h_ref.shape[0] // sub)]
    nt = (((1,), (1,)), ((), ()))

    def merged_of(rows):
        merged = None
        for br, (a_ref, w_ref) in enumerate(((ap_ref, pp_ref), (ac_ref, cp_ref), (at_ref, mp_ref))):
            y = jnp.dot(a_ref[rows, :], w_ref[...], preferred_element_type=F32)
            gate = jax.nn.sigmoid(gl_ref[rows, br * d:(br + 1) * d].astype(F32) + bg_ref[br:br + 1, :])
            merged = gate * y if merged is None else merged + gate * y
        return merged.astype(BF16)

    merged = [merged_of(rows) for rows in parts]
    ys = [jnp.dot(m, wo_ref[...], preferred_element_type=F32) for m in merged]
    es = [jnp.dot(p_ref[rows, :].astype(BF16), pproj_ref[...], preferred_element_type=F32) for rows in parts]
    h1s = [_layer_norm(alpha * h_ref[rows, :] + y, g_ref[...], b_ref[...]) for rows, y in zip(parts, ys)]
    for k, (rows, h1, e) in enumerate(zip(parts, h1s, es)):
        h1b = h1.astype(BF16)
        e = e * jax.nn.sigmoid(jnp.dot(h1b, pgate_ref[...], preferred_element_type=F32))
        e_ref[rows, :] = e.astype(BF16)
        h1l = (h1 - h1b.astype(F32)).astype(BF16)
        lg = lax.dot_general(wrh_ref[...], h1b, nt, preferred_element_type=F32)
        lg = lg + lax.dot_general(wrh_ref[...], h1l, nt, preferred_element_type=F32)
        lg = lg + lax.dot_general(wrl_ref[...], h1b, nt, preferred_element_type=F32)
        lg_ref[:, k * sub:(k + 1) * sub] = lg
        _store_token_major(h1_ref.at[pl.ds(k * sub * SUBLANES, sub * SUBLANES), :], h1)


def _merge_call(alpha, layer, ap, ac, at, gl, h, p, pp, cp, mp, bg, wo, g, b, pproj, pgate, wrh, wrl):
    t, d = h.shape
    tm = TM_MERGE
    row = lambda i: (i, 0)
    ins = [ap, ac, at, gl, h]
    in_specs = [pl.BlockSpec((tm, a.shape[1]), row) for a in ins]
    in_specs.append(pl.BlockSpec((None, tm, p.shape[-1]), lambda i: (layer, i, 0)))
    stacked = {id(a) for a in (pp, cp, mp, wo, pproj, pgate)}
    consts = [pp, cp, mp, bg, wo, g, b, pproj, pgate, wrh, wrl]
    in_specs += [_layer_full(a, layer) if id(a) in stacked else _full(a.shape) for a in consts]
    return pl.pallas_call(
        functools.partial(_merge_kernel, alpha),
        grid=(t // tm,),
        in_specs=in_specs,
        out_specs=[pl.BlockSpec((tm * SUBLANES, LANES), row), pl.BlockSpec((tm, d), row),
                   pl.BlockSpec((N_EXPERTS, tm), lambda i: (0, i))],
        out_shape=[jax.ShapeDtypeStruct((t * SUBLANES, LANES), F32), jax.ShapeDtypeStruct((t, d), BF16),
                   jax.ShapeDtypeStruct((N_EXPERTS, t), F32)],
        compiler_params=_cparams(("arbitrary",)),
        name="merge",
    )(*ins, p, *consts)


def _route_kernel(lg_ref, bias_ref, cls_ref, wa_ref, wb_ref):
    aff = [jax.nn.sigmoid(lg_ref[e]) for e in range(N_EXPERTS)]
    sel = [aff[e] + bias_ref[e] for e in range(N_EXPERTS)]
    n = EXPERTS_PER_GROUP

    def top2_sum(vals):
        best = None
        for a, b in PAIRS:
            s = vals[a] + vals[b]
            best = s if best is None else jnp.maximum(best, s)
        return best

    grp = jnp.zeros(aff[0].shape, jnp.int32)
    best = top2_sum(sel[0:n])
    for g in range(1, N_GROUPS):
        sc = top2_sum(sel[g * n:(g + 1) * n])
        better = sc > best
        grp = jnp.where(better, g, grp)
        best = jnp.where(better, sc, best)
    vs, afs = [], []
    for j in range(n):
        v, a = sel[j], aff[j]
        for g in range(1, N_GROUPS):
            v = jnp.where(grp == g, sel[g * n + j], v)
            a = jnp.where(grp == g, aff[g * n + j], a)
        vs.append(v)
        afs.append(a)
    first = jnp.zeros_like(grp)
    fv = vs[0]
    for j in range(1, n):
        better = vs[j] > fv
        first = jnp.where(better, j, first)
        fv = jnp.where(better, vs[j], fv)
    second = jnp.full_like(grp, -1)
    sv = jnp.full_like(fv, -jnp.inf)
    for j in range(n):
        better = (first != j) & ((second < 0) | (vs[j] > sv))
        second = jnp.where(better, j, second)
        sv = jnp.where(better, vs[j], sv)
    lo = jnp.minimum(first, second)
    hi = jnp.maximum(first, second)
    a_lo, a_hi = afs[0], afs[0]
    for j in range(1, n):
        a_lo = jnp.where(lo == j, afs[j], a_lo)
        a_hi = jnp.where(hi == j, afs[j], a_hi)
    pair = jnp.zeros_like(grp)
    for idx, (a, b) in enumerate(PAIRS):
        pair = jnp.where((lo == a) & (hi == b), idx, pair)
    tot = a_lo + a_hi
    cls_ref[...] = grp * len(PAIRS) + pair
    wa_ref[...] = a_lo / tot
    wb_ref[...] = a_hi / tot


def _route_call(lg3, bias):
    _, rows, lanes = lg3.shape
    blk = pl.BlockSpec((ROUTE_ROWS, lanes), lambda i: (i, 0))
    return pl.pallas_call(
        _route_kernel,
        grid=(rows // ROUTE_ROWS,),
        in_specs=[pl.BlockSpec((N_EXPERTS, ROUTE_ROWS, lanes), lambda i: (0, i, 0)),
                  pl.BlockSpec(memory_space=pltpu.SMEM)],
        out_specs=[blk, blk, blk],
        out_shape=[jax.ShapeDtypeStruct((rows, lanes), jnp.int32),
                   jax.ShapeDtypeStruct((rows, lanes), F32), jax.ShapeDtypeStruct((rows, lanes), F32)],
        compiler_params=_cparams(("arbitrary",)),
        name="route",
    )(lg3, bias)


GATHER_UNROLL = 8


def _moe_kernel(ta_ref, tb_ref, nused_ref, valid_ref,
                h_hbm, tok_ref, tok_next_ref, wa_ref, wb_ref, upa_ref, upb_ref, dna_ref, dnb_ref,
                m_hbm, xbuf, ybuf, gsem, ssem):
    i = pl.program_id(0)
    n_used = nused_ref[0]
    slot = i % 2
    rows = MOE_TILE * SUBLANES

    def row_copy_in(toks, s, r):
        tok = toks[0, 0, r]
        return pltpu.make_async_copy(
            h_hbm.at[pl.ds(pl.multiple_of(tok * SUBLANES, SUBLANES), SUBLANES), :],
            xbuf.at[s, pl.ds(pl.multiple_of(r * SUBLANES, SUBLANES), SUBLANES), :],
            gsem.at[s])

    def row_copy_out(toks, s, r):
        tok = toks[0, 0, r]
        return pltpu.make_async_copy(
            ybuf.at[s, pl.ds(pl.multiple_of(r * SUBLANES, SUBLANES), SUBLANES), :],
            m_hbm.at[pl.ds(pl.multiple_of(tok * SUBLANES, SUBLANES), SUBLANES), :],
            ssem.at[s])

    def start_rows(make, toks, s, count):
        def body8(c, carry):
            for u in range(GATHER_UNROLL):
                make(toks, s, c * GATHER_UNROLL + u).start(priority=u % 2)
            return carry

        def body1(r, carry):
            make(toks, s, r).start()
            return carry

        full = count // GATHER_UNROLL
        lax.fori_loop(0, full, body8, 0)
        lax.fori_loop(full * GATHER_UNROLL, count, body1, 0)

    def wait_gather(s):
        pltpu.make_async_copy(h_hbm.at[pl.ds(0, rows), :], xbuf.at[s], gsem.at[s]).wait()

    def wait_scatter(tile, s):
        n = pl.multiple_of(valid_ref[tile] * SUBLANES, SUBLANES)
        pltpu.make_async_copy(ybuf.at[s, pl.ds(0, n), :], m_hbm.at[pl.ds(0, n), :], ssem.at[s]).wait()

    @pl.when(i == 0)
    def _():
        start_rows(row_copy_in, tok_ref, 0, MOE_TILE)

    @pl.when(i + 1 < n_used)
    def _():
        start_rows(row_copy_in, tok_next_ref, 1 - slot, MOE_TILE)

    @pl.when(i < n_used)
    def _():
        wait_gather(slot)
        x = _load_token_major(xbuf.at[slot], MOE_TILE, upa_ref.shape[1]).astype(BF16)

        def ffn(up_ref, dn_ref):
            gu = jnp.dot(x, up_ref[0], preferred_element_type=F32)
            hid = jax.nn.silu(gu[:, :D_EXPERT]) * gu[:, D_EXPERT:]
            return jnp.dot(hid.astype(BF16), dn_ref[0], preferred_element_type=F32)

        y = ffn(upa_ref, dna_ref) * wa_ref[...] + ffn(upb_ref, dnb_ref) * wb_ref[...]

        @pl.when(i >= 2)
        def _():
            wait_scatter(i - 2, slot)

        _store_token_major(ybuf.at[slot], y)
        start_rows(row_copy_out, tok_ref, slot, valid_ref[i])

        @pl.when(i == n_used - 1)
        def _():
            @pl.when(i >= 1)
            def _():
                wait_scatter(i - 1, 1 - slot)
            wait_scatter(i, slot)


def _moe_call(layer, tile_a, tile_b, n_used, tile_valid, slot_tok, h1_tm, slot_wa, slot_wb, w_up, w_down,
              n_tiles, t):
    d = w_up.shape[1]
    e0 = layer * N_EXPERTS
    wspec = pl.BlockSpec((MOE_TILE, 1), lambda i, *_: (i, 0))
    tok_blk = (1, 1, MOE_TILE)
    grid_spec = pltpu.PrefetchScalarGridSpec(
        num_scalar_prefetch=4,
        grid=(n_tiles,),
        in_specs=[pl.BlockSpec(memory_space=pl.ANY),
                  pl.BlockSpec(tok_blk, lambda i, *_: (i, 0, 0), memory_space=pltpu.SMEM),
                  pl.BlockSpec(tok_blk, lambda i, *_: (jnp.minimum(i + 1, n_tiles - 1), 0, 0),
                               memory_space=pltpu.SMEM),
                  wspec, wspec,
                  pl.BlockSpec((1, d, 2 * D_EXPERT), lambda i, ta, tb, *_: (e0 + ta[i], 0, 0)),
                  pl.BlockSpec((1, d, 2 * D_EXPERT), lambda i, ta, tb, *_: (e0 + tb[i], 0, 0)),
                  pl.BlockSpec((1, D_EXPERT, d), lambda i, ta, tb, *_: (e0 + ta[i], 0, 0)),
                  pl.BlockSpec((1, D_EXPERT, d), lambda i, ta, tb, *_: (e0 + tb[i], 0, 0))],
        out_specs=pl.BlockSpec(memory_space=pl.ANY),
        scratch_shapes=[pltpu.VMEM((2, MOE_TILE * SUBLANES, LANES), F32),
                        pltpu.VMEM((2, MOE_TILE * SUBLANES, LANES), F32),
                        pltpu.SemaphoreType.DMA((2,)), pltpu.SemaphoreType.DMA((2,))])
    return pl.pallas_call(
        _moe_kernel,
        grid_spec=grid_spec,
        out_shape=jax.ShapeDtypeStruct((t * SUBLANES, LANES), F32),
        compiler_params=_cparams(("arbitrary",)),
        name="moe",
    )(tile_a, tile_b, n_used, tile_valid, h1_tm, slot_tok, slot_tok, slot_wa, slot_wb, w_up, w_up, w_down, w_down)


def _final_kernel(alpha, h1_ref, m_ref, e_ref, g_ref, b_ref, o_ref):
    rows, d = o_ref.shape
    h1 = _load_token_major(h1_ref, rows, d)
    m = _load_token_major(m_ref, rows, d)
    o_ref[...] = _layer_norm(alpha * h1 + m + e_ref[...].astype(F32), g_ref[...], b_ref[...])


def _final_call(alpha, h1_tm, m_tm, e, g, b):
    t, d = e.shape
    tm = TM_FINAL
    row = lambda i: (i, 0)
    tmaj = pl.BlockSpec((tm * SUBLANES, LANES), row)
    return pl.pallas_call(
        functools.partial(_final_kernel, alpha),
        grid=(t // tm,),
        in_specs=[tmaj, tmaj, pl.BlockSpec((tm, d), row), _full(g.shape), _full(b.shape)],
        out_specs=pl.BlockSpec((tm, d), row),
        out_shape=jax.ShapeDtypeStruct((t, d), F32),
        compiler_params=_cparams(("arbitrary",)),
        name="final_ln",
    )(h1_tm, m_tm, e, g, b)


def _pack_w_in(w_in):
    off_q = OFF_CONV + 2 * D_CONV
    off_kr = off_q + Q_LORA + KV_LORA
    off_gate = off_kr + QK_ROPE
    kr = jnp.pad(w_in[..., off_kr:off_gate], ((0, 0), (0, 0), (ROPE_LANE0, HEAD_BLOCK - ROPE_LANE0 - QK_ROPE)))
    return jnp.concatenate([w_in[..., :off_kr], kr, w_in[..., off_gate:]], axis=-1).astype(BF16)


def _pack_heads(w, lo, width, stride, lane0):
    blocks = []
    for h in range(N_HEADS):
        l0 = lane0(h)
        blocks.append(jnp.pad(w[..., h * stride + lo:h * stride + lo + width],
                              ((0, 0), (0, 0), (l0, HEAD_BLOCK - l0 - width))))
    return jnp.concatenate(blocks, axis=-1).astype(BF16)


def _rope_tables(positions):
    inv_freq = jnp.power(ROPE_THETA, -jnp.arange(0, QK_ROPE, 2, dtype=F32) / QK_ROPE)
    ang = positions.astype(F32).reshape(-1, 1) * inv_freq
    cos, sin = jnp.cos(ang), jnp.sin(ang)
    t = ang.shape[0]
    ones_lo = jnp.ones((t, ROPE_LANE0), F32)
    ones_hi = jnp.ones((t, HEAD_BLOCK - ROPE_LANE0 - QK_ROPE), F32)
    zeros_lo = jnp.zeros((t, ROPE_LANE0), F32)
    zeros_half = jnp.zeros((t, HALF_ROPE), F32)
    zeros_hi = jnp.zeros((t, HEAD_BLOCK - ROPE_LANE0 - QK_ROPE), F32)
    c = jnp.concatenate([ones_lo, cos, cos, ones_hi], axis=-1)
    s1 = jnp.concatenate([zeros_lo, -sin, zeros_half, zeros_hi], axis=-1)
    s2 = jnp.concatenate([zeros_lo, zeros_half, sin, zeros_hi], axis=-1)
    return c, s1, s2


def _routing_plan(cls, wa, wb, n_tiles):
    t = cls.shape[0]
    classes = jnp.arange(N_CLASSES, dtype=jnp.int32)
    _, order, wa_s, wb_s = lax.sort((cls, jnp.arange(t, dtype=jnp.int32), wa, wb), num_keys=1)
    counts = jnp.sum((cls[None, :] == classes[:, None]).astype(jnp.int32), axis=1)
    start = jnp.cumsum(counts) - counts
    padded = (counts + MOE_TILE - 1) // MOE_TILE * MOE_TILE
    pad_end = jnp.cumsum(padded)
    pad_start = pad_end - padded
    tile_row0 = jnp.arange(n_tiles, dtype=jnp.int32) * MOE_TILE
    tile_cls = jnp.minimum(jnp.sum((tile_row0[:, None] >= pad_end[None, :]).astype(jnp.int32), axis=1),
                           N_CLASSES - 1)
    onehot = (tile_cls[:, None] == classes[None, :]).astype(jnp.int32)
    pick = lambda table: jnp.sum(onehot * table[None, :], axis=1)
    tile_off = tile_row0 - pick(pad_start)
    tile_valid = jnp.clip(pick(counts) - tile_off, 0, MOE_TILE).astype(jnp.int32)
    r = jnp.arange(MOE_TILE, dtype=jnp.int32)[None, :]
    valid = r < tile_valid[:, None]
    sorted_pos = jnp.clip((pick(start) + tile_off)[:, None] + r, 0, t - 1)
    slot_tok = jnp.where(valid, order[sorted_pos], 0).reshape(n_tiles, 1, MOE_TILE)
    slot_wa = jnp.where(valid, wa_s[sorted_pos], 0.0).reshape(-1, 1)
    slot_wb = jnp.where(valid, wb_s[sorted_pos], 0.0).reshape(-1, 1)
    pair_lo = jnp.array([p[0] for p in PAIRS], jnp.int32)
    pair_hi = jnp.array([p[1] for p in PAIRS], jnp.int32)
    grp = tile_cls // len(PAIRS)
    pair_onehot = ((tile_cls % len(PAIRS))[:, None] == jnp.arange(len(PAIRS), dtype=jnp.int32)[None, :])
    tile_a = grp * EXPERTS_PER_GROUP + jnp.sum(pair_onehot * pair_lo[None, :], axis=1)
    tile_b = grp * EXPERTS_PER_GROUP + jnp.sum(pair_onehot * pair_hi[None, :], axis=1)
    n_used = (pad_end[-1] // MOE_TILE).astype(jnp.int32).reshape(1)
    return tile_a, tile_b, n_used, tile_valid, slot_tok, slot_wa, slot_wb


def kernel(x, p, positions, ln_in_g, ln_in_b, w_in, b_gate, pool_w, pool_scale, pool_proj, conv_dw, conv_b, conv_ln_g, conv_ln_b, conv_proj, q_norm_g, w_uq, kv_norm_g, w_ukv, mla_proj, w_out, ln1_g, ln1_b, w_router, router_bias, exp_w_up, exp_w_down, ple_proj, ple_gate, ln2_g, ln2_b):
    batch, seq, d = x.shape
    depth = w_in.shape[0]
    t = batch * seq
    alpha = float((2 * depth) ** 0.25)
    n_tiles = (t + N_CLASSES * (MOE_TILE - 1)) // MOE_TILE + 1

    w_in_p = _pack_w_in(w_in)
    pool_w_b = pool_w.astype(BF16)
    pool_proj_b, conv_proj_b, mla_proj_b = pool_proj.astype(BF16), conv_proj.astype(BF16), mla_proj.astype(BF16)
    qk = QK_NOPE + QK_ROPE
    wq_p = (_pack_heads(w_uq, 0, QK_NOPE, qk, lambda h: 0)
            + _pack_heads(w_uq, QK_NOPE, QK_ROPE, qk, lambda h: ROPE_LANE0))
    wqt_p = jnp.swapaxes(wq_p, 1, 2)
    wk_p = _pack_heads(w_ukv, 0, QK_NOPE, QK_NOPE + V_HEAD, lambda h: 0)
    kv_w = QK_NOPE + V_HEAD
    wvt_p = jnp.swapaxes(jnp.concatenate(
        [jnp.pad(w_ukv[..., h * kv_w + QK_NOPE:(h + 1) * kv_w], ((0, 0), (0, 0), (0, BF16_ROWS)))
         for h in range(N_HEADS)], axis=-1), 1, 2).astype(BF16)
    ones_rows = jnp.tile(jnp.concatenate([jnp.zeros((V_HEAD, 1), F32), jnp.ones((BF16_ROWS, 1), F32)]),
                         (N_HEADS, 1))
    w_out_b, ple_proj_b, ple_gate_b = w_out.astype(BF16), ple_proj.astype(BF16), ple_gate.astype(BF16)
    w_up_b = exp_w_up.astype(BF16).reshape((depth * N_EXPERTS,) + exp_w_up.shape[2:])
    w_down_b = exp_w_down.astype(BF16).reshape((depth * N_EXPERTS,) + exp_w_down.shape[2:])
    wr_t = w_router.T
    wr_hi = wr_t.astype(BF16)
    wr_lo = (wr_t - wr_hi.astype(F32)).astype(BF16)
    rope_tabs = _rope_tables(positions)
    rope_tabs_t = tuple(a.T for a in rope_tabs)
    p2 = p.reshape(depth, t, -1)

    h = _ln_call(x.reshape(t, d), ln_in_g.reshape(1, d), ln_in_b.reshape(1, d))
    for i in range(depth):
        u_pool, u_conv, cqkv, glog = _in_proj_call(h, w_in_p, i)
        a_pool = _pool_call(u_pool, pool_w_b[i], pool_scale[i].reshape(1, -1), batch, seq)
        a_conv = _conv_call(u_conv, conv_dw[i], conv_b[i].reshape(1, -1), conv_ln_g[i].reshape(1, -1),
                            conv_ln_b[i].reshape(1, -1), batch, seq)
        qt, k, vt = _mla_prep_call(cqkv, rope_tabs, rope_tabs_t, q_norm_g[i].reshape(1, -1),
                                   kv_norm_g[i].reshape(1, -1), wqt_p[i], wk_p[i], wvt_p[i], ones_rows)
        a_attn = _attn_call(qt, k, vt, batch, seq)
        h1_tm, e, logits = _merge_call(alpha, i, a_pool, a_conv, a_attn, glog, h, p2,
                                       pool_proj_b, conv_proj_b, mla_proj_b, b_gate[i], w_out_b,
                                       ln1_g[i].reshape(1, d), ln1_b[i].reshape(1, d),
                                       ple_proj_b, ple_gate_b, wr_hi, wr_lo)
        cls, wa, wb = _route_call(logits.reshape(N_EXPERTS, t // LANES, LANES), router_bias)
        plan = _routing_plan(cls.reshape(t), wa.reshape(t), wb.reshape(t), n_tiles)
        m_tm = _moe_call(i, *plan[:5], h1_tm, plan[5], plan[6], w_up_b, w_down_b, n_tiles, t)
        h = _final_call(alpha, h1_tm, m_tm, e, ln2_g[i].reshape(1, d), ln2_b[i].reshape(1, d))
    return h.reshape(batch, seq, d)
```

```python
import functools

import jax
import jax.numpy as jnp
from jax import lax
from jax.experimental import pallas as pl
from jax.experimental.pallas import tpu as pltpu

F32 = jnp.float32
BF16 = jnp.bfloat16

CHUNK = 64
POOL_WINDOWS = (2, 4, 8, 16)
POOL_GW = 128
D_POOL = 512
D_CONV = 512
CONV_WIDTH = 31
N_HEADS = 8
QK_NOPE = 64
QK_ROPE = 32
V_HEAD = 64
Q_LORA = 384
KV_LORA = 256
ROPE_THETA = 10000.0
N_EXPERTS = 16
N_GROUPS = 4
EXPERTS_PER_GROUP = 4
D_EXPERT = 512
LN_EPS = 1e-5
RMS_EPS = 1e-6

LANES = 128
SUBLANES = 8
HEAD_BLOCK = LANES
ROPE_LANE0 = QK_NOPE
HALF_ROPE = QK_ROPE // 2
BF16_ROWS = 16
VT_BLOCK = V_HEAD + BF16_ROWS
LOG2_E = 1.4426950408889634

TM_PROJ = 512
TM_PREP = 512
TM_MERGE = 512
MERGE_SUB = 256
TM_FINAL = 512
TQ = 256
TK = 256
ATTN_LOOKAHEAD = 3
POOL_CHUNK = 256
POOL_HALO = 16
CONV_CHUNK = 64
CONV_HALO = 32
MOE_TILE = 256
ROUTE_ROWS = 8

PAIRS = ((0, 1), (0, 2), (0, 3), (1, 2), (1, 3), (2, 3))
N_CLASSES = N_GROUPS * len(PAIRS)

VMEM_LIMIT = 56 * 1024 * 1024


def _cparams(sem):
    return pltpu.CompilerParams(dimension_semantics=sem, vmem_limit_bytes=VMEM_LIMIT)


def _layer_norm(x, g, b):
    mu = jnp.mean(x, axis=-1, keepdims=True)
    xc = x - mu
    var = jnp.mean(xc * xc, axis=-1, keepdims=True)
    return xc * lax.rsqrt(var + LN_EPS) * g + b


def _rms_norm(x, g):
    ms = jnp.mean(x * x, axis=-1, keepdims=True)
    return x * lax.rsqrt(ms + RMS_EPS) * g


def _full(shape):
    n = len(shape)
    return pl.BlockSpec(shape, lambda *_: (0,) * n)


def _layer_full(stacked, layer):
    n = stacked.ndim - 1
    return pl.BlockSpec((None,) + stacked.shape[1:], lambda *_: (layer,) + (0,) * n)


def _ln_kernel(x_ref, g_ref, b_ref, o_ref):
    o_ref[...] = _layer_norm(x_ref[...], g_ref[...], b_ref[...])


def _ln_call(x, g, b):
    t, d = x.shape
    return pl.pallas_call(
        _ln_kernel,
        grid=(t // TM_FINAL,),
        in_specs=[pl.BlockSpec((TM_FINAL, d), lambda i: (i, 0)), _full((1, d)), _full((1, d))],
        out_specs=pl.BlockSpec((TM_FINAL, d), lambda i: (i, 0)),
        out_shape=jax.ShapeDtypeStruct((t, d), F32),
        compiler_params=_cparams(("arbitrary",)),
        name="ln_in",
    )(x, g, b)


OFF_CONV = D_POOL
OFF_CQKV = OFF_CONV + 2 * D_CONV
W_CQKV = Q_LORA + KV_LORA + HEAD_BLOCK
OFF_GATE = OFF_CQKV + W_CQKV
N_COLS_CHUNK = 512


def _in_proj_kernel(h_ref, w_ref, pool_ref, conv_ref, cqkv_ref, gate_ref):
    x = h_ref[...].astype(BF16)

    def mm(lo, hi):
        return jnp.dot(x, w_ref[:, lo:hi], preferred_element_type=F32).astype(BF16)

    pool_ref[...] = mm(0, OFF_CONV)
    for c in range(2 * D_CONV // N_COLS_CHUNK):
        conv_ref[:, c * N_COLS_CHUNK:(c + 1) * N_COLS_CHUNK] = mm(
            OFF_CONV + c * N_COLS_CHUNK, OFF_CONV + (c + 1) * N_COLS_CHUNK)
    cqkv_ref[...] = mm(OFF_CQKV, OFF_GATE)
    for c in range(gate_ref.shape[1] // N_COLS_CHUNK):
        gate_ref[:, c * N_COLS_CHUNK:(c + 1) * N_COLS_CHUNK] = mm(
            OFF_GATE + c * N_COLS_CHUNK, OFF_GATE + (c + 1) * N_COLS_CHUNK)


def _in_proj_call(h, w, layer):
    t, d = h.shape
    n = w.shape[-1]
    n_gate = n - OFF_GATE
    row = lambda i: (i, 0)
    return pl.pallas_call(
        _in_proj_kernel,
        grid=(t // TM_PROJ,),
        in_specs=[pl.BlockSpec((TM_PROJ, d), row), _layer_full(w, layer)],
        out_specs=[pl.BlockSpec((TM_PROJ, D_POOL), row), pl.BlockSpec((TM_PROJ, 2 * D_CONV), row),
                   pl.BlockSpec((TM_PROJ, W_CQKV), row), pl.BlockSpec((TM_PROJ, n_gate), row)],
        out_shape=[jax.ShapeDtypeStruct((t, D_POOL), BF16), jax.ShapeDtypeStruct((t, 2 * D_CONV), BF16),
                   jax.ShapeDtypeStruct((t, W_CQKV), BF16), jax.ShapeDtypeStruct((t, n_gate), BF16)],
        compiler_params=_cparams(("arbitrary",)),
        name="in_proj",
    )(h, w)


def _pool_kernel(u_ref, w_ref, scale_ref, o_ref, buf):
    s = o_ref.shape[0]
    buf[0:POOL_HALO, :] = jnp.zeros((POOL_HALO, D_POOL), F32)

    def to_f32(c, carry):
        r = pl.multiple_of(c * POOL_CHUNK, POOL_CHUNK)
        buf[pl.ds(POOL_HALO + r, POOL_CHUNK), :] = u_ref[pl.ds(r, POOL_CHUNK), :].astype(F32)
        return carry

    lax.fori_loop(0, s // POOL_CHUNK, to_f32, 0)

    def body(c, carry):
        r0 = pl.multiple_of(c * POOL_CHUNK, POOL_CHUNK)
        t = r0 + lax.broadcasted_iota(jnp.int32, (POOL_CHUNK, 1), 0)
        for g, w in enumerate(POOL_WINDOWS):
            cols = slice(g * POOL_GW, (g + 1) * POOL_GW)
            xw = buf[pl.ds(r0, POOL_CHUNK + POOL_HALO), cols]
            acc = xw
            k = 1
            while k < w:
                acc = acc + pltpu.roll(acc, k, axis=0)
                k *= 2
            cnt = jnp.minimum(t + 1, w).astype(F32)
            mixed = acc[POOL_HALO:] / cnt - xw[POOL_HALO:]
            y = jnp.dot(mixed.astype(BF16), w_ref[g], preferred_element_type=F32) * scale_ref[:, cols]
            o_ref[pl.ds(r0, POOL_CHUNK), cols] = y.astype(BF16)
        return carry

    lax.fori_loop(0, s // POOL_CHUNK, body, 0)


def _pool_call(u, w, scale, batch, seq):
    return pl.pallas_call(
        _pool_kernel,
        grid=(batch,),
        in_specs=[pl.BlockSpec((seq, D_POOL), lambda b: (b, 0)),
                  _full(w.shape), _full(scale.shape)],
        out_specs=pl.BlockSpec((seq, D_POOL), lambda b: (b, 0)),
        out_shape=jax.ShapeDtypeStruct(u.shape, BF16),
        scratch_shapes=[pltpu.VMEM((seq + POOL_HALO, D_POOL), F32)],
        compiler_params=_cparams(("arbitrary",)),
        name="pool",
    )(u, w, scale)


GLU_CHUNK = 256


def _conv_kernel(u_ref, dw_ref, cb_ref, g_ref, b_ref, o_ref, zs):
    s = o_ref.shape[0]
    zs[0:CONV_HALO, :] = jnp.zeros((CONV_HALO, D_CONV), F32)

    def glu(c, carry):
        r = pl.multiple_of(c * GLU_CHUNK, GLU_CHUNK)
        a = u_ref[pl.ds(r, GLU_CHUNK), 0:D_CONV].astype(F32)
        gate = u_ref[pl.ds(r, GLU_CHUNK), D_CONV:2 * D_CONV].astype(F32)
        zs[pl.ds(CONV_HALO + r, GLU_CHUNK), :] = a * jax.nn.sigmoid(gate)
        return carry

    lax.fori_loop(0, s // GLU_CHUNK, glu, 0)

    def body(c, carry):
        r0 = pl.multiple_of(c * CONV_CHUNK, CONV_CHUNK)
        parts = []
        for cg in range(D_CONV // LANES):
            cols = slice(cg * LANES, (cg + 1) * LANES)
            win = zs[pl.ds(r0, CONV_CHUNK + CONV_HALO), cols]
            acc = jnp.zeros((CONV_CHUNK, LANES), F32) + cb_ref[:, cols]
            for sub in range(SUBLANES):
                shifted = win if sub == 0 else pltpu.roll(win, sub, axis=0)
                for a in range(CONV_HALO // SUBLANES):
                    lag = SUBLANES * a + sub
                    if lag >= CONV_WIDTH:
                        continue
                    k = CONV_WIDTH - 1 - lag
                    lo = CONV_HALO - SUBLANES * a
                    acc = acc + dw_ref[k:k + 1, cols] * shifted[lo:lo + CONV_CHUNK]
            parts.append(acc)
        y = jnp.concatenate(parts, axis=-1)
        y = _layer_norm(y, g_ref[...], b_ref[...])
        y = y * jax.nn.sigmoid(y)
        o_ref[pl.ds(r0, CONV_CHUNK), :] = y.astype(BF16)
        return carry

    lax.fori_loop(0, s // CONV_CHUNK, body, 0)


def _conv_call(u, dw, cb, g, b, batch, seq):
    return pl.pallas_call(
        _conv_kernel,
        grid=(batch,),
        in_specs=[pl.BlockSpec((seq, 2 * D_CONV), lambda i: (i, 0)),
                  _full(dw.shape), _full(cb.shape), _full(g.shape), _full(b.shape)],
        out_specs=pl.BlockSpec((seq, D_CONV), lambda i: (i, 0)),
        out_shape=jax.ShapeDtypeStruct((u.shape[0], D_CONV), BF16),
        scratch_shapes=[pltpu.VMEM((seq + CONV_HALO, D_CONV), F32)],
        compiler_params=_cparams(("arbitrary",)),
        name="conv",
    )(u, dw, cb, g, b)


def _rope_block(x, c, s1, s2):
    return x * c + pltpu.roll(x, LANES - HALF_ROPE, axis=1) * s1 + pltpu.roll(x, HALF_ROPE, axis=1) * s2


def _rope_block_t(x, c, s1, s2):
    return x * c + pltpu.roll(x, HEAD_BLOCK - HALF_ROPE, axis=0) * s1 + pltpu.roll(x, HALF_ROPE, axis=0) * s2


def _mla_prep_kernel(cqkv_ref, c_ref, s1_ref, s2_ref, ct_ref, s1t_ref, s2t_ref, qg_ref, kvg_ref,
                     wqt_ref, wk_ref, wvt_ref, ones_ref, qt_ref, k_ref, vt_ref):
    cq = _rms_norm(cqkv_ref[:, 0:Q_LORA].astype(F32), qg_ref[...]).astype(BF16)
    ckv = _rms_norm(cqkv_ref[:, Q_LORA:Q_LORA + KV_LORA].astype(F32), kvg_ref[...]).astype(BF16)
    kr = _rope_block(cqkv_ref[:, Q_LORA + KV_LORA:W_CQKV].astype(F32), c_ref[...], s1_ref[...], s2_ref[...])
    scale = float((QK_NOPE + QK_ROPE) ** -0.5 * LOG2_E)
    nt = (((1,), (1,)), ((), ()))
    ct, s1t, s2t = ct_ref[...], s1t_ref[...], s2t_ref[...]
    for h in range(N_HEADS):
        cols = slice(h * HEAD_BLOCK, (h + 1) * HEAD_BLOCK)
        qt = lax.dot_general(wqt_ref[cols, :], cq, nt, preferred_element_type=F32) * scale
        qt_ref[cols, :] = _rope_block_t(qt, ct, s1t, s2t).astype(BF16)
        k = jnp.dot(ckv, wk_ref[:, cols], preferred_element_type=F32) + kr
        k_ref[:, cols] = k.astype(BF16)
    vt = lax.dot_general(wvt_ref[...], ckv, nt, preferred_element_type=F32) + ones_ref[...]
    vt_ref[...] = vt.astype(BF16)


def _mla_prep_call(cqkv, tabs, tabs_t, qg, kvg, wqt, wk, wvt, ones_rows):
    t = cqkv.shape[0]
    n = N_HEADS * HEAD_BLOCK
    nv = N_HEADS * VT_BLOCK
    row = lambda i: (i, 0)
    col = lambda i: (0, i)
    tab = pl.BlockSpec((TM_PREP, LANES), row)
    tab_t = pl.BlockSpec((LANES, TM_PREP), col)
    return pl.pallas_call(
        _mla_prep_kernel,
        grid=(t // TM_PREP,),
        in_specs=[pl.BlockSpec((TM_PREP, W_CQKV), row), tab, tab, tab, tab_t, tab_t, tab_t,
                  _full(qg.shape), _full(kvg.shape), _full(wqt.shape), _full(wk.shape), _full(wvt.shape),
                  _full(ones_rows.shape)],
        out_specs=[pl.BlockSpec((n, TM_PREP), col), pl.BlockSpec((TM_PREP, n), row),
                   pl.BlockSpec((nv, TM_PREP), col)],
        out_shape=[jax.ShapeDtypeStruct((n, t), BF16), jax.ShapeDtypeStruct((t, n), BF16),
                   jax.ShapeDtypeStruct((nv, t), BF16)],
        compiler_params=_cparams(("arbitrary",)),
        name="mla_prep",
    )(cqkv, *tabs, *tabs_t, qg, kvg, wqt, wk, wvt, ones_rows)


def _attn_kernel(qt_ref, k_ref, vt_ref, o_ref, m_ref, acc_ref):
    i = pl.program_id(1)
    key_chunk = lax.broadcasted_iota(jnp.int32, (TK, TQ), 0) // CHUNK
    qry_chunk = lax.broadcasted_iota(jnp.int32, (TK, TQ), 1) // CHUNK
    diag_mask = key_chunk <= qry_chunk

    def scores(h, kt):
        k0 = pl.multiple_of(kt * TK, TK)
        kk = k_ref[pl.ds(k0, TK), h * HEAD_BLOCK:(h + 1) * HEAD_BLOCK]
        return jnp.dot(kk, qt_ref[h * HEAD_BLOCK:(h + 1) * HEAD_BLOCK, :], preferred_element_type=F32)

    def values_t(h, kt):
        k0 = pl.multiple_of(kt * TK, TK)
        return vt_ref[h * VT_BLOCK:(h + 1) * VT_BLOCK, pl.ds(k0, TK)]

    def first_update(h, kt, s):
        s = jnp.where(diag_mask, s, -jnp.inf)
        m = jnp.max(s, axis=0, keepdims=True)
        p = jnp.exp2(s - m)
        m_ref[h] = jnp.broadcast_to(m, (SUBLANES, TQ))
        acc_ref[h] = jnp.dot(values_t(h, kt), p.astype(BF16), preferred_element_type=F32)

    def update(h, kt, s):
        m_old = m_ref[h]
        m_new = jnp.maximum(m_old, jnp.max(s, axis=0, keepdims=True))
        alpha = jnp.exp2(m_old - m_new)
        p = jnp.exp2(s - m_new[0:1, :])
        m_ref[h] = m_new
        acc_ref[h] = alpha[0:1, :] * acc_ref[h] + jnp.dot(values_t(h, kt), p.astype(BF16),
                                                         preferred_element_type=F32)

    def all_heads(kt, upd):
        pending = {}
        for idx in range(N_HEADS + ATTN_LOOKAHEAD):
            if idx < N_HEADS:
                pending[idx] = scores(idx, kt)
            if idx >= ATTN_LOOKAHEAD:
                upd(idx - ATTN_LOOKAHEAD, kt, pending.pop(idx - ATTN_LOOKAHEAD))

    all_heads(i, first_update)

    def step(kt, carry):
        all_heads(kt, update)
        return carry

    lax.fori_loop(0, i, step, 0)

    outs = [acc_ref[h, 0:V_HEAD, :] * (1.0 / acc_ref[h, V_HEAD:V_HEAD + 1, :]) for h in range(N_HEADS)]
    o_ref[...] = jnp.concatenate(outs, axis=0).T.astype(BF16)


def _attn_call(qt, k, vt, batch, seq):
    n = N_HEADS * HEAD_BLOCK
    nq = seq // TQ
    return pl.pallas_call(
        _attn_kernel,
        grid=(batch, nq),
        in_specs=[pl.BlockSpec((n, TQ), lambda b, i: (0, b * nq + i)),
                  pl.BlockSpec((seq, n), lambda b, i: (b, 0)),
                  pl.BlockSpec((N_HEADS * VT_BLOCK, seq), lambda b, i: (0, b))],
        out_specs=pl.BlockSpec((TQ, N_HEADS * V_HEAD), lambda b, i: (b * nq + i, 0)),
        out_shape=jax.ShapeDtypeStruct((k.shape[0], N_HEADS * V_HEAD), BF16),
        scratch_shapes=[pltpu.VMEM((N_HEADS, SUBLANES, TQ), F32),
                        pltpu.VMEM((N_HEADS, VT_BLOCK, TQ), F32)],
        compiler_params=_cparams(("arbitrary", "arbitrary")),
        name="attn",
    )(qt, k, vt)


def _store_token_major(ref, x):
    rows = x.shape[0]
    for j in range(x.shape[1] // LANES):
        ref[pl.ds(j, rows, stride=SUBLANES), :] = x[:, j * LANES:(j + 1) * LANES]


def _load_token_major(ref, rows, d):
    return jnp.concatenate([ref[pl.ds(j, rows, stride=SUBLANES), :] for j in range(d // LANES)], axis=-1)


def _merge_kernel(alpha, ap_ref, ac_ref, at_ref, gl_ref, h_ref, p_ref, pp_ref, cp_ref, mp_ref, bg_ref,
                  wo_ref, g_ref, b_ref, pproj_ref, pgate_ref, wrh_ref, wrl_ref,
                  h1_ref, e_ref, lg_ref):
    d = h_ref.shape[1]
    sub = MERGE_SUB
    parts = [pl.ds(k * sub, sub) for k in range(h_ref.shape[0] // sub)]
    nt = (((1,), (1,)), ((), ()))

    def merged_of(rows):
        merged = None
        for br, (a_ref, w_ref) in enumerate(((ap_ref, pp_ref), (ac_ref, cp_ref), (at_ref, mp_ref))):
            y = jnp.dot(a_ref[rows, :], w_ref[...], preferred_element_type=F32)
            gate = jax.nn.sigmoid(gl_ref[rows, br * d:(br + 1) * d].astype(F32) + bg_ref[br:br + 1, :])
            merged = gate * y if merged is None else merged + gate * y
        return merged.astype(BF16)

    merged = [merged_of(rows) for rows in parts]
    ys = [jnp.dot(m, wo_ref[...], preferred_element_type=F32) for m in merged]
    es = [jnp.dot(p_ref[rows, :].astype(BF16), pproj_ref[...], preferred_element_type=F32) for rows in parts]
    h1s = [_layer_norm(alpha * h_ref[rows, :] + y, g_ref[...], b_ref[...]) for rows, y in zip(parts, ys)]
    for k, (rows, h1, e) in enumerate(zip(parts, h1s, es)):
        h1b = h1.astype(BF16)
        e = e * jax.nn.sigmoid(jnp.dot(h1b, pgate_ref[...], preferred_element_type=F32))
        e_ref[rows, :] = e.astype(BF16)
        h1l = (h1 - h1b.astype(F32)).astype(BF16)
        lg = lax.dot_general(wrh_ref[...], h1b, nt, preferred_element_type=F32)
        lg = lg + lax.dot_general(wrh_ref[...], h1l, nt, preferred_element_type=F32)
        lg = lg + lax.dot_general(wrl_ref[...], h1b, nt, preferred_element_type=F32)
        lg_ref[:, k * sub:(k + 1) * sub] = lg
        _store_token_major(h1_ref.at[pl.ds(k * sub * SUBLANES, sub * SUBLANES), :], h1)


def _merge_call(alpha, layer, ap, ac, at, gl, h, p, pp, cp, mp, bg, wo, g, b, pproj, pgate, wrh, wrl):
    t, d = h.shape
    tm = TM_MERGE
    row = lambda i: (i, 0)
    ins = [ap, ac, at, gl, h]
    in_specs = [pl.BlockSpec((tm, a.shape[1]), row) for a in ins]
    in_specs.append(pl.BlockSpec((None, tm, p.shape[-1]), lambda i: (layer, i, 0)))
    stacked = {id(a) for a in (pp, cp, mp, wo, pproj, pgate)}
    consts = [pp, cp, mp, bg, wo, g, b, pproj, pgate, wrh, wrl]
    in_specs += [_layer_full(a, layer) if id(a) in stacked else _full(a.shape) for a in consts]
    return pl.pallas_call(
        functools.partial(_merge_kernel, alpha),
        grid=(t // tm,),
        in_specs=in_specs,
        out_specs=[pl.BlockSpec((tm * SUBLANES, LANES), row), pl.BlockSpec((tm, d), row),
                   pl.BlockSpec((N_EXPERTS, tm), lambda i: (0, i))],
        out_shape=[jax.ShapeDtypeStruct((t * SUBLANES, LANES), F32), jax.ShapeDtypeStruct((t, d), BF16),
                   jax.ShapeDtypeStruct((N_EXPERTS, t), F32)],
        compiler_params=_cparams(("arbitrary",)),
        name="merge",
    )(*ins, p, *consts)


def _route_kernel(lg_ref, bias_ref, cls_ref, wa_ref, wb_ref):
    aff = [jax.nn.sigmoid(lg_ref[e]) for e in range(N_EXPERTS)]
    sel = [aff[e] + bias_ref[e] for e in range(N_EXPERTS)]
    n = EXPERTS_PER_GROUP

    def top2_sum(vals):
        best = None
        for a, b in PAIRS:
            s = vals[a] + vals[b]
            best = s if best is None else jnp.maximum(best, s)
        return best

    grp = jnp.zeros(aff[0].shape, jnp.int32)
    best = top2_sum(sel[0:n])
    for g in range(1, N_GROUPS):
        sc = top2_sum(sel[g * n:(g + 1) * n])
        better = sc > best
        grp = jnp.where(better, g, grp)
        best = jnp.where(better, sc, best)
    vs, afs = [], []
    for j in range(n):
        v, a = sel[j], aff[j]
        for g in range(1, N_GROUPS):
            v = jnp.where(grp == g, sel[g * n + j], v)
            a = jnp.where(grp == g, aff[g * n + j], a)
        vs.append(v)
        afs.append(a)
    first = jnp.zeros_like(grp)
    fv = vs[0]
    for j in range(1, n):
        better = vs[j] > fv
        first = jnp.where(better, j, first)
        fv = jnp.where(better, vs[j], fv)
    second = jnp.full_like(grp, -1)
    sv = jnp.full_like(fv, -jnp.inf)
    for j in range(n):
        better = (first != j) & ((second < 0) | (vs[j] > sv))
        second = jnp.where(better, j, second)
        sv = jnp.where(better, vs[j], sv)
    lo = jnp.minimum(first, second)
    hi = jnp.maximum(first, second)
    a_lo, a_hi = afs[0], afs[0]
    for j in range(1, n):
        a_lo = jnp.where(lo == j, afs[j], a_lo)
        a_hi = jnp.where(hi == j, afs[j], a_hi)
    pair = jnp.zeros_like(grp)
    for idx, (a, b) in enumerate(PAIRS):
        pair = jnp.where((lo == a) & (hi == b), idx, pair)
    tot = a_lo + a_hi
    cls_ref[...] = grp * len(PAIRS) + pair
    wa_ref[...] = a_lo / tot
    wb_ref[...] = a_hi / tot


def _route_call(lg3, bias):
    _, rows, lanes = lg3.shape
    blk = pl.BlockSpec((ROUTE_ROWS, lanes), lambda i: (i, 0))
    return pl.pallas_call(
        _route_kernel,
        grid=(rows // ROUTE_ROWS,),
        in_specs=[pl.BlockSpec((N_EXPERTS, ROUTE_ROWS, lanes), lambda i: (0, i, 0)),
                  pl.BlockSpec(memory_space=pltpu.SMEM)],
        out_specs=[blk, blk, blk],
        out_shape=[jax.ShapeDtypeStruct((rows, lanes), jnp.int32),
                   jax.ShapeDtypeStruct((rows, lanes), F32), jax.ShapeDtypeStruct((rows, lanes), F32)],
        compiler_params=_cparams(("arbitrary",)),
        name="route",
    )(lg3, bias)


GATHER_UNROLL = 8


MOE_N_CHUNK = 256
MOE_DUMP_ROWS = N_CLASSES * MOE_TILE


def _moe_kernel(ta_ref, tb_ref, nused_ref,
                h_hbm, src_ref, src_next_ref, dst_ref, dst_prev_ref, wa_ref, wb_ref,
                upa_ref, upb_ref, dna_ref, dnb_ref, m_hbm, xbuf, ybuf, gsem, ssem, zsem):
    i = pl.program_id(0)
    n_used = nused_ref[0]
    slot = i % 2
    rows = MOE_TILE * SUBLANES
    d = upa_ref.shape[1]

    def row_copy_in(toks, s, r):
        tok = toks[0, 0, r]
        return pltpu.make_async_copy(
            h_hbm.at[pl.ds(pl.multiple_of(tok * SUBLANES, SUBLANES), SUBLANES), :],
            xbuf.at[s, pl.ds(pl.multiple_of(r * SUBLANES, SUBLANES), SUBLANES), :],
            gsem.at[s])

    def row_copy_out(toks, s, r):
        tok = toks[0, 0, r]
        return pltpu.make_async_copy(
            ybuf.at[s, pl.ds(pl.multiple_of(r * SUBLANES, SUBLANES), SUBLANES), :],
            m_hbm.at[pl.ds(pl.multiple_of(tok * SUBLANES, SUBLANES), SUBLANES), :],
            ssem.at[s])

    def start_rows_loop(make, toks, s):
        def body(c, carry):
            for u in range(GATHER_UNROLL):
                make(toks, s, c * GATHER_UNROLL + u).start()
            return carry
        lax.fori_loop(0, MOE_TILE // GATHER_UNROLL, body, 0)

    def wait_gather(s):
        pltpu.make_async_copy(h_hbm.at[pl.ds(0, rows), :], xbuf.at[s], gsem.at[s]).wait()

    def wait_scatter(s):
        pltpu.make_async_copy(ybuf.at[s], m_hbm.at[pl.ds(0, rows), :], ssem.at[s]).wait()

    def compute(issue_list):
        x = _load_token_major(xbuf.at[slot], MOE_TILE, d).astype(BF16)
        n_up = D_EXPERT // MOE_N_CHUNK
        n_gaps = 2 * (2 * n_up + n_up)
        per_gap = -(-len(issue_list) // n_gaps)
        queue = list(issue_list)

        def burst():
            for _ in range(min(per_gap, len(queue))):
                queue.pop(0)()

        y = None
        for up_ref, dn_ref, w_ref in ((upa_ref, dna_ref, wa_ref), (upb_ref, dnb_ref, wb_ref)):
            hid = []
            for j in range(n_up):
                c0 = j * MOE_N_CHUNK
                gate = jnp.dot(x, up_ref[0, :, c0:c0 + MOE_N_CHUNK], preferred_element_type=F32)
                burst()
                up = jnp.dot(x, up_ref[0, :, D_EXPERT + c0:D_EXPERT + c0 + MOE_N_CHUNK],
                             preferred_element_type=F32)
                burst()
                hid.append((jax.nn.silu(gate) * up).astype(BF16))
            ye = None
            for j in range(n_up):
                part = jnp.dot(hid[j], dn_ref[0, j * MOE_N_CHUNK:(j + 1) * MOE_N_CHUNK, :],
                               preferred_element_type=F32)
                burst()
                ye = part if ye is None else ye + part
            y = ye * w_ref[...] if y is None else y + ye * w_ref[...]
        while queue:
            queue.pop(0)()
        return y

    def gather_next_rows():
        return [functools.partial(lambda r: row_copy_in(src_next_ref, 1 - slot, r).start(), r)
                for r in range(MOE_TILE)]

    def scatter_prev_rows():
        return [functools.partial(lambda r: row_copy_out(dst_prev_ref, 1 - slot, r).start(), r)
                for r in range(MOE_TILE)]

    def finish(y):
        @pl.when(i >= 2)
        def _():
            wait_scatter(slot)

        _store_token_major(ybuf.at[slot], y)

        @pl.when(i == n_used - 1)
        def _():
            start_rows_loop(row_copy_out, dst_ref, slot)

            @pl.when(i >= 1)
            def _():
                wait_scatter(1 - slot)
            wait_scatter(slot)
            wait_gather(1 - slot)

    @pl.when(i == 0)
    def _():
        ybuf[1] = jnp.zeros((rows, LANES), F32)
        t_rows = m_hbm.shape[0] - MOE_DUMP_ROWS * SUBLANES
        chunks = [pltpu.make_async_copy(ybuf.at[1], m_hbm.at[pl.ds(t_rows + c * rows, rows), :], zsem)
                  for c in range(MOE_DUMP_ROWS // MOE_TILE)]
        for cp in chunks:
            cp.start()
        start_rows_loop(row_copy_in, src_ref, 0)
        for cp in chunks:
            cp.wait()
        wait_gather(0)
        finish(compute(gather_next_rows()))

    @pl.when((i > 0) & (i < n_used))
    def _():
        wait_gather(slot)
        issue = []
        for g, sc in zip(gather_next_rows(), scatter_prev_rows()):
            issue += [g, sc]
        finish(compute(issue))


def _moe_call(layer, tile_a, tile_b, n_used, slot_src, slot_dst, h1_tm, slot_wa, slot_wb, w_up, w_down,
              n_tiles, t):
    d = w_up.shape[1]
    e0 = layer * N_EXPERTS
    wspec = pl.BlockSpec((MOE_TILE, 1), lambda i, *_: (i, 0))
    tok_blk = (1, 1, MOE_TILE)
    grid_spec = pltpu.PrefetchScalarGridSpec(
        num_scalar_prefetch=3,
        grid=(n_tiles,),
        in_specs=[pl.BlockSpec(memory_space=pl.ANY),
                  pl.BlockSpec(tok_blk, lambda i, *_: (i, 0, 0), memory_space=pltpu.SMEM),
                  pl.BlockSpec(tok_blk, lambda i, *_: (jnp.minimum(i + 1, n_tiles - 1), 0, 0),
                               memory_space=pltpu.SMEM),
                  pl.BlockSpec(tok_blk, lambda i, *_: (i, 0, 0), memory_space=pltpu.SMEM),
                  pl.BlockSpec(tok_blk, lambda i, *_: (jnp.maximum(i - 1, 0), 0, 0),
                               memory_space=pltpu.SMEM),
                  wspec, wspec,
                  pl.BlockSpec((1, d, 2 * D_EXPERT), lambda i, ta, tb, *_: (e0 + ta[i], 0, 0)),
                  pl.BlockSpec((1, d, 2 * D_EXPERT), lambda i, ta, tb, *_: (e0 + tb[i], 0, 0)),
                  pl.BlockSpec((1, D_EXPERT, d), lambda i, ta, tb, *_: (e0 + ta[i], 0, 0)),
                  pl.BlockSpec((1, D_EXPERT, d), lambda i, ta, tb, *_: (e0 + tb[i], 0, 0))],
        out_specs=pl.BlockSpec(memory_space=pl.ANY),
        scratch_shapes=[pltpu.VMEM((2, MOE_TILE * SUBLANES, LANES), F32),
                        pltpu.VMEM((2, MOE_TILE * SUBLANES, LANES), F32),
                        pltpu.SemaphoreType.DMA((2,)), pltpu.SemaphoreType.DMA((2,)),
                        pltpu.SemaphoreType.DMA])
    return pl.pallas_call(
        _moe_kernel,
        grid_spec=grid_spec,
        out_shape=jax.ShapeDtypeStruct(((t + MOE_DUMP_ROWS) * SUBLANES, LANES), F32),
        compiler_params=_cparams(("arbitrary",)),
        name="moe",
    )(tile_a, tile_b, n_used, h1_tm, slot_src, slot_src, slot_dst, slot_dst, slot_wa, slot_wb,
      w_up, w_up, w_down, w_down)


def _final_kernel(alpha, h1_ref, m_ref, e_ref, g_ref, b_ref, o_ref):
    rows, d = o_ref.shape
    h1 = _load_token_major(h1_ref, rows, d)
    m = _load_token_major(m_ref, rows, d)
    o_ref[...] = _layer_norm(alpha * h1 + m + e_ref[...].astype(F32), g_ref[...], b_ref[...])


def _final_call(alpha, h1_tm, m_tm, e, g, b):
    t, d = e.shape
    tm = TM_FINAL
    row = lambda i: (i, 0)
    tmaj = pl.BlockSpec((tm * SUBLANES, LANES), row)
    return pl.pallas_call(
        functools.partial(_final_kernel, alpha),
        grid=(t // tm,),
        in_specs=[tmaj, tmaj, pl.BlockSpec((tm, d), row), _full(g.shape), _full(b.shape)],
        out_specs=pl.BlockSpec((tm, d), row),
        out_shape=jax.ShapeDtypeStruct((t, d), F32),
        compiler_params=_cparams(("arbitrary",)),
        name="final_ln",
    )(h1_tm, m_tm, e, g, b)


def _pack_w_in(w_in):
    off_q = OFF_CONV + 2 * D_CONV
    off_kr = off_q + Q_LORA + KV_LORA
    off_gate = off_kr + QK_ROPE
    kr = jnp.pad(w_in[..., off_kr:off_gate], ((0, 0), (0, 0), (ROPE_LANE0, HEAD_BLOCK - ROPE_LANE0 - QK_ROPE)))
    return jnp.concatenate([w_in[..., :off_kr], kr, w_in[..., off_gate:]], axis=-1).astype(BF16)


def _pack_heads(w, lo, width, stride, lane0):
    blocks = []
    for h in range(N_HEADS):
        l0 = lane0(h)
        blocks.append(jnp.pad(w[..., h * stride + lo:h * stride + lo + width],
                              ((0, 0), (0, 0), (l0, HEAD_BLOCK - l0 - width))))
    return jnp.concatenate(blocks, axis=-1).astype(BF16)


def _rope_tables(positions):
    inv_freq = jnp.power(ROPE_THETA, -jnp.arange(0, QK_ROPE, 2, dtype=F32) / QK_ROPE)
    ang = positions.astype(F32).reshape(-1, 1) * inv_freq
    cos, sin = jnp.cos(ang), jnp.sin(ang)
    t = ang.shape[0]
    ones_lo = jnp.ones((t, ROPE_LANE0), F32)
    ones_hi = jnp.ones((t, HEAD_BLOCK - ROPE_LANE0 - QK_ROPE), F32)
    zeros_lo = jnp.zeros((t, ROPE_LANE0), F32)
    zeros_half = jnp.zeros((t, HALF_ROPE), F32)
    zeros_hi = jnp.zeros((t, HEAD_BLOCK - ROPE_LANE0 - QK_ROPE), F32)
    c = jnp.concatenate([ones_lo, cos, cos, ones_hi], axis=-1)
    s1 = jnp.concatenate([zeros_lo, -sin, zeros_half, zeros_hi], axis=-1)
    s2 = jnp.concatenate([zeros_lo, zeros_half, sin, zeros_hi], axis=-1)
    return c, s1, s2


def _routing_plan(cls, wa, wb, n_tiles):
    t = cls.shape[0]
    n_pad = n_tiles * MOE_TILE - t
    assert n_pad == MOE_DUMP_ROWS, (n_pad, MOE_DUMP_ROWS)
    classes = jnp.arange(N_CLASSES, dtype=jnp.int32)
    counts = jnp.sum((cls[None, :] == classes[:, None]).astype(jnp.int32), axis=1)
    padded = (counts + MOE_TILE - 1) // MOE_TILE * MOE_TILE
    pad_end = jnp.cumsum(padded)
    pad_need = padded - counts
    pad_base = jnp.cumsum(pad_need) - pad_need
    j = jnp.arange(MOE_TILE, dtype=jnp.int32)[None, :]
    pad_used = j < pad_need[:, None]
    pad_key = jnp.where(pad_used, 2 * classes[:, None] + 1, 2 * N_CLASSES).reshape(-1)
    pad_dst = (t + jnp.where(pad_used, pad_base[:, None] + j, 0)).reshape(-1)
    tok = jnp.arange(t, dtype=jnp.int32)
    zeros_i = jnp.zeros((n_pad,), jnp.int32)
    zeros_f = jnp.zeros((n_pad,), F32)
    _, slot_src, slot_dst, slot_wa, slot_wb = lax.sort(
        (jnp.concatenate([2 * cls, pad_key]), jnp.concatenate([tok, zeros_i]),
         jnp.concatenate([tok, pad_dst]), jnp.concatenate([wa, zeros_f]), jnp.concatenate([wb, zeros_f])),
        num_keys=1)
    slot_src = slot_src.reshape(n_tiles, 1, MOE_TILE)
    slot_dst = slot_dst.reshape(n_tiles, 1, MOE_TILE)
    slot_wa = slot_wa.reshape(-1, 1)
    slot_wb = slot_wb.reshape(-1, 1)
    tile_row0 = jnp.arange(n_tiles, dtype=jnp.int32) * MOE_TILE
    tile_cls = jnp.minimum(jnp.sum((tile_row0[:, None] >= pad_end[None, :]).astype(jnp.int32), axis=1),
                           N_CLASSES - 1)
    pair_lo = jnp.array([p[0] for p in PAIRS], jnp.int32)
    pair_hi = jnp.array([p[1] for p in PAIRS], jnp.int32)
    grp = tile_cls // len(PAIRS)
    pair_onehot = ((tile_cls % len(PAIRS))[:, None] == jnp.arange(len(PAIRS), dtype=jnp.int32)[None, :])
    tile_a = grp * EXPERTS_PER_GROUP + jnp.sum(pair_onehot * pair_lo[None, :], axis=1)
    tile_b = grp * EXPERTS_PER_GROUP + jnp.sum(pair_onehot * pair_hi[None, :], axis=1)
    n_used = (pad_end[-1] // MOE_TILE).astype(jnp.int32).reshape(1)
    return tile_a, tile_b, n_used, slot_src, slot_dst, slot_wa, slot_wb


def kernel(x, p, positions, ln_in_g, ln_in_b, w_in, b_gate, pool_w, pool_scale, pool_proj, conv_dw, conv_b, conv_ln_g, conv_ln_b, conv_proj, q_norm_g, w_uq, kv_norm_g, w_ukv, mla_proj, w_out, ln1_g, ln1_b, w_router, router_bias, exp_w_up, exp_w_down, ple_proj, ple_gate, ln2_g, ln2_b):
    batch, seq, d = x.shape
    depth = w_in.shape[0]
    t = batch * seq
    alpha = float((2 * depth) ** 0.25)
    n_tiles = (t + N_CLASSES * (MOE_TILE - 1)) // MOE_TILE + 1

    w_in_p = _pack_w_in(w_in)
    pool_w_b = pool_w.astype(BF16)
    pool_proj_b, conv_proj_b, mla_proj_b = pool_proj.astype(BF16), conv_proj.astype(BF16), mla_proj.astype(BF16)
    qk = QK_NOPE + QK_ROPE
    wq_p = (_pack_heads(w_uq, 0, QK_NOPE, qk, lambda h: 0)
            + _pack_heads(w_uq, QK_NOPE, QK_ROPE, qk, lambda h: ROPE_LANE0))
    wqt_p = jnp.swapaxes(wq_p, 1, 2)
    wk_p = _pack_heads(w_ukv, 0, QK_NOPE, QK_NOPE + V_HEAD, lambda h: 0)
    kv_w = QK_NOPE + V_HEAD
    wvt_p = jnp.swapaxes(jnp.concatenate(
        [jnp.pad(w_ukv[..., h * kv_w + QK_NOPE:(h + 1) * kv_w], ((0, 0), (0, 0), (0, BF16_ROWS)))
         for h in range(N_HEADS)], axis=-1), 1, 2).astype(BF16)
    ones_rows = jnp.tile(jnp.concatenate([jnp.zeros((V_HEAD, 1), F32), jnp.ones((BF16_ROWS, 1), F32)]),
                         (N_HEADS, 1))
    w_out_b, ple_proj_b, ple_gate_b = w_out.astype(BF16), ple_proj.astype(BF16), ple_gate.astype(BF16)
    w_up_b = exp_w_up.astype(BF16).reshape((depth * N_EXPERTS,) + exp_w_up.shape[2:])
    w_down_b = exp_w_down.astype(BF16).reshape((depth * N_EXPERTS,) + exp_w_down.shape[2:])
    wr_t = w_router.T
    wr_hi = wr_t.astype(BF16)
    wr_lo = (wr_t - wr_hi.astype(F32)).astype(BF16)
    rope_tabs = _rope_tables(positions)
    rope_tabs_t = tuple(a.T for a in rope_tabs)
    p2 = p.reshape(depth, t, -1)

    h = _ln_call(x.reshape(t, d), ln_in_g.reshape(1, d), ln_in_b.reshape(1, d))
    for i in range(depth):
        u_pool, u_conv, cqkv, glog = _in_proj_call(h, w_in_p, i)
        a_pool = _pool_call(u_pool, pool_w_b[i], pool_scale[i].reshape(1, -1), batch, seq)
        a_conv = _conv_call(u_conv, conv_dw[i], conv_b[i].reshape(1, -1), conv_ln_g[i].reshape(1, -1),
                            conv_ln_b[i].reshape(1, -1), batch, seq)
        qt, k, vt = _mla_prep_call(cqkv, rope_tabs, rope_tabs_t, q_norm_g[i].reshape(1, -1),
                                   kv_norm_g[i].reshape(1, -1), wqt_p[i], wk_p[i], wvt_p[i], ones_rows)
        a_attn = _attn_call(qt, k, vt, batch, seq)
        h1_tm, e, logits = _merge_call(alpha, i, a_pool, a_conv, a_attn, glog, h, p2,
                                       pool_proj_b, conv_proj_b, mla_proj_b, b_gate[i], w_out_b,
                                       ln1_g[i].reshape(1, d), ln1_b[i].reshape(1, d),
                                       ple_proj_b, ple_gate_b, wr_hi, wr_lo)
        cls, wa, wb = _route_call(logits.reshape(N_EXPERTS, t // LANES, LANES), router_bias)
        plan = _routing_plan(cls.reshape(t), wa.reshape(t), wb.reshape(t), n_tiles)
        m_tm = _moe_call(i, *plan[:5], h1_tm, plan[5], plan[6], w_up_b, w_down_b, n_tiles, t)
        h = _final_call(alpha, h1_tm, m_tm, e, ln2_g[i].reshape(1, d), ln2_b[i].reshape(1, d))
    return h.reshape(batch, seq, d)
```

```python
import functools

import jax
import jax.numpy as jnp
from jax import lax
from jax.experimental import pallas as pl
from jax.experimental.pallas import tpu as pltpu

F32 = jnp.float32
BF16 = jnp.bfloat16

CHUNK = 64
POOL_WINDOWS = (2, 4, 8, 16)
POOL_GW = 128
D_POOL = 512
D_CONV = 512
CONV_WIDTH = 31
N_HEADS = 8
QK_NOPE = 64
QK_ROPE = 32
V_HEAD = 64
Q_LORA = 384
KV_LORA = 256
ROPE_THETA = 10000.0
N_EXPERTS = 16
N_GROUPS = 4
EXPERTS_PER_GROUP = 4
D_EXPERT = 512
LN_EPS = 1e-5
RMS_EPS = 1e-6

LANES = 128
SUBLANES = 8
HEAD_BLOCK = LANES
ROPE_LANE0 = QK_NOPE
HALF_ROPE = QK_ROPE // 2
BF16_ROWS = 16
VT_BLOCK = V_HEAD + BF16_ROWS
LOG2_E = 1.4426950408889634

TM_PROJ = 512
TM_PREP = 512
TM_MERGE = 512
MERGE_SUB = 256
TM_FINAL = 512
TQ = 256
TK = 256
Q_SUBS = 2
ATTN_LOOKAHEAD = 3
POOL_CHUNK = 256
POOL_HALO = 16
CONV_CHUNK = 64
CONV_HALO = 32
MOE_TILE = 256
ROUTE_ROWS = 8

PAIRS = ((0, 1), (0, 2), (0, 3), (1, 2), (1, 3), (2, 3))
N_CLASSES = N_GROUPS * len(PAIRS)

VMEM_LIMIT = 56 * 1024 * 1024


def _cparams(sem):
    return pltpu.CompilerParams(dimension_semantics=sem, vmem_limit_bytes=VMEM_LIMIT)


def _layer_norm(x, g, b):
    mu = jnp.mean(x, axis=-1, keepdims=True)
    xc = x - mu
    var = jnp.mean(xc * xc, axis=-1, keepdims=True)
    return xc * lax.rsqrt(var + LN_EPS) * g + b


def _rms_norm(x, g):
    ms = jnp.mean(x * x, axis=-1, keepdims=True)
    return x * lax.rsqrt(ms + RMS_EPS) * g


def _full(shape):
    n = len(shape)
    return pl.BlockSpec(shape, lambda *_: (0,) * n)


def _layer_full(stacked, layer):
    n = stacked.ndim - 1
    return pl.BlockSpec((None,) + stacked.shape[1:], lambda *_: (layer,) + (0,) * n)


OFF_CONV = D_POOL
OFF_CQKV = OFF_CONV + 2 * D_CONV
W_CQKV = Q_LORA + KV_LORA + HEAD_BLOCK
OFF_GATE = OFF_CQKV + W_CQKV
N_COLS_CHUNK = 512


def _in_proj_kernel(alpha, residual, *refs):
    if residual:
        h1_ref, m_ref, e_ref, g_ref, b_ref, w_ref, h_ref, pool_ref, conv_ref, cqkv_ref, gate_ref = refs
        rows, d = h_ref.shape
        pre = alpha * _load_token_major(h1_ref, rows, d) + _load_token_major(m_ref, rows, d)
        pre = pre + e_ref[...].astype(F32)
    else:
        x_ref, g_ref, b_ref, w_ref, h_ref, pool_ref, conv_ref, cqkv_ref, gate_ref = refs
        pre = x_ref[...]
    h = _layer_norm(pre, g_ref[...], b_ref[...])
    h_ref[...] = h
    x = h.astype(BF16)

    def mm(lo, hi):
        return jnp.dot(x, w_ref[:, lo:hi], preferred_element_type=F32).astype(BF16)

    pool_ref[...] = mm(0, OFF_CONV)
    for c in range(2 * D_CONV // N_COLS_CHUNK):
        conv_ref[:, c * N_COLS_CHUNK:(c + 1) * N_COLS_CHUNK] = mm(
            OFF_CONV + c * N_COLS_CHUNK, OFF_CONV + (c + 1) * N_COLS_CHUNK)
    cqkv_ref[...] = mm(OFF_CQKV, OFF_GATE)
    for c in range(gate_ref.shape[1] // N_COLS_CHUNK):
        gate_ref[:, c * N_COLS_CHUNK:(c + 1) * N_COLS_CHUNK] = mm(
            OFF_GATE + c * N_COLS_CHUNK, OFF_GATE + (c + 1) * N_COLS_CHUNK)


def _in_proj_call(alpha, pre, g, b, w, layer):
    residual = len(pre) == 3
    t, d = pre[-1].shape
    n = w.shape[-1]
    n_gate = n - OFF_GATE
    row = lambda i: (i, 0)
    tmaj = pl.BlockSpec((TM_PROJ * SUBLANES, LANES), row)
    pre_specs = [tmaj, tmaj, pl.BlockSpec((TM_PROJ, d), row)] if residual else [pl.BlockSpec((TM_PROJ, d), row)]
    return pl.pallas_call(
        functools.partial(_in_proj_kernel, alpha, residual),
        grid=(t // TM_PROJ,),
        in_specs=pre_specs + [_full(g.shape), _full(b.shape), _layer_full(w, layer)],
        out_specs=[pl.BlockSpec((TM_PROJ, d), row),
                   pl.BlockSpec((TM_PROJ, D_POOL), row), pl.BlockSpec((TM_PROJ, 2 * D_CONV), row),
                   pl.BlockSpec((TM_PROJ, W_CQKV), row), pl.BlockSpec((TM_PROJ, n_gate), row)],
        out_shape=[jax.ShapeDtypeStruct((t, d), F32),
                   jax.ShapeDtypeStruct((t, D_POOL), BF16), jax.ShapeDtypeStruct((t, 2 * D_CONV), BF16),
                   jax.ShapeDtypeStruct((t, W_CQKV), BF16), jax.ShapeDtypeStruct((t, n_gate), BF16)],
        compiler_params=_cparams(("arbitrary",)),
        name="in_proj",
    )(*pre, g, b, w)


def _pool_kernel(u_ref, w_ref, scale_ref, o_ref, buf):
    s = o_ref.shape[0]
    buf[0:POOL_HALO, :] = jnp.zeros((POOL_HALO, D_POOL), F32)

    def to_f32(c, carry):
        r = pl.multiple_of(c * POOL_CHUNK, POOL_CHUNK)
        buf[pl.ds(POOL_HALO + r, POOL_CHUNK), :] = u_ref[pl.ds(r, POOL_CHUNK), :].astype(F32)
        return carry

    lax.fori_loop(0, s // POOL_CHUNK, to_f32, 0)

    def body(c, carry):
        r0 = pl.multiple_of(c * POOL_CHUNK, POOL_CHUNK)
        t = r0 + lax.broadcasted_iota(jnp.int32, (POOL_CHUNK, 1), 0)
        for g, w in enumerate(POOL_WINDOWS):
            cols = slice(g * POOL_GW, (g + 1) * POOL_GW)
            xw = buf[pl.ds(r0, POOL_CHUNK + POOL_HALO), cols]
            acc = xw
            k = 1
            while k < w:
                acc = acc + pltpu.roll(acc, k, axis=0)
                k *= 2
            cnt = jnp.minimum(t + 1, w).astype(F32)
            mixed = acc[POOL_HALO:] / cnt - xw[POOL_HALO:]
            y = jnp.dot(mixed.astype(BF16), w_ref[g], preferred_element_type=F32) * scale_ref[:, cols]
            o_ref[pl.ds(r0, POOL_CHUNK), cols] = y.astype(BF16)
        return carry

    lax.fori_loop(0, s // POOL_CHUNK, body, 0)


def _pool_call(u, w, scale, batch, seq):
    return pl.pallas_call(
        _pool_kernel,
        grid=(batch,),
        in_specs=[pl.BlockSpec((seq, D_POOL), lambda b: (b, 0)),
                  _full(w.shape), _full(scale.shape)],
        out_specs=pl.BlockSpec((seq, D_POOL), lambda b: (b, 0)),
        out_shape=jax.ShapeDtypeStruct(u.shape, BF16),
        scratch_shapes=[pltpu.VMEM((seq + POOL_HALO, D_POOL), F32)],
        compiler_params=_cparams(("arbitrary",)),
        name="pool",
    )(u, w, scale)


GLU_CHUNK = 256


def _conv_kernel(u_ref, dw_ref, cb_ref, g_ref, b_ref, o_ref, zs):
    s = o_ref.shape[0]
    zs[0:CONV_HALO, :] = jnp.zeros((CONV_HALO, D_CONV), F32)

    def glu(c, carry):
        r = pl.multiple_of(c * GLU_CHUNK, GLU_CHUNK)
        a = u_ref[pl.ds(r, GLU_CHUNK), 0:D_CONV].astype(F32)
        gate = u_ref[pl.ds(r, GLU_CHUNK), D_CONV:2 * D_CONV].astype(F32)
        zs[pl.ds(CONV_HALO + r, GLU_CHUNK), :] = a * jax.nn.sigmoid(gate)
        return carry

    lax.fori_loop(0, s // GLU_CHUNK, glu, 0)

    def body(c, carry):
        r0 = pl.multiple_of(c * CONV_CHUNK, CONV_CHUNK)
        parts = []
        for cg in range(D_CONV // LANES):
            cols = slice(cg * LANES, (cg + 1) * LANES)
            win = zs[pl.ds(r0, CONV_CHUNK + CONV_HALO), cols]
            acc = jnp.zeros((CONV_CHUNK, LANES), F32) + cb_ref[:, cols]
            for sub in range(SUBLANES):
                shifted = win if sub == 0 else pltpu.roll(win, sub, axis=0)
                for a in range(CONV_HALO // SUBLANES):
                    lag = SUBLANES * a + sub
                    if lag >= CONV_WIDTH:
                        continue
                    k = CONV_WIDTH - 1 - lag
                    lo = CONV_HALO - SUBLANES * a
                    acc = acc + dw_ref[k:k + 1, cols] * shifted[lo:lo + CONV_CHUNK]
            parts.append(acc)
        y = jnp.concatenate(parts, axis=-1)
        y = _layer_norm(y, g_ref[...], b_ref[...])
        y = y * jax.nn.sigmoid(y)
        o_ref[pl.ds(r0, CONV_CHUNK), :] = y.astype(BF16)
        return carry

    lax.fori_loop(0, s // CONV_CHUNK, body, 0)


def _conv_call(u, dw, cb, g, b, batch, seq):
    return pl.pallas_call(
        _conv_kernel,
        grid=(batch,),
        in_specs=[pl.BlockSpec((seq, 2 * D_CONV), lambda i: (i, 0)),
                  _full(dw.shape), _full(cb.shape), _full(g.shape), _full(b.shape)],
        out_specs=pl.BlockSpec((seq, D_CONV), lambda i: (i, 0)),
        out_shape=jax.ShapeDtypeStruct((u.shape[0], D_CONV), BF16),
        scratch_shapes=[pltpu.VMEM((seq + CONV_HALO, D_CONV), F32)],
        compiler_params=_cparams(("arbitrary",)),
        name="conv",
    )(u, dw, cb, g, b)


def _rope_block(x, c, s1, s2):
    return x * c + pltpu.roll(x, LANES - HALF_ROPE, axis=1) * s1 + pltpu.roll(x, HALF_ROPE, axis=1) * s2


def _rope_block_t(x, c, s1, s2):
    return x * c + pltpu.roll(x, HEAD_BLOCK - HALF_ROPE, axis=0) * s1 + pltpu.roll(x, HALF_ROPE, axis=0) * s2


def _mla_prep_kernel(cqkv_ref, c_ref, s1_ref, s2_ref, ct_ref, s1t_ref, s2t_ref, qg_ref, kvg_ref,
                     wqt_ref, wk_ref, wvt_ref, ones_ref, qt_ref, k_ref, vt_ref):
    cq = _rms_norm(cqkv_ref[:, 0:Q_LORA].astype(F32), qg_ref[...]).astype(BF16)
    ckv = _rms_norm(cqkv_ref[:, Q_LORA:Q_LORA + KV_LORA].astype(F32), kvg_ref[...]).astype(BF16)
    kr = _rope_block(cqkv_ref[:, Q_LORA + KV_LORA:W_CQKV].astype(F32), c_ref[...], s1_ref[...], s2_ref[...])
    scale = float((QK_NOPE + QK_ROPE) ** -0.5 * LOG2_E)
    nt = (((1,), (1,)), ((), ()))
    ct, s1t, s2t = ct_ref[...], s1t_ref[...], s2t_ref[...]
    for h in range(N_HEADS):
        cols = slice(h * HEAD_BLOCK, (h + 1) * HEAD_BLOCK)
        qt = lax.dot_general(wqt_ref[cols, :], cq, nt, preferred_element_type=F32) * scale
        qt_ref[cols, :] = _rope_block_t(qt, ct, s1t, s2t).astype(BF16)
        k = jnp.dot(ckv, wk_ref[:, cols], preferred_element_type=F32) + kr
        k_ref[:, cols] = k.astype(BF16)
    vt = lax.dot_general(wvt_ref[...], ckv, nt, preferred_element_type=F32) + ones_ref[...]
    vt_ref[...] = vt.astype(BF16)


def _mla_prep_call(cqkv, tabs, tabs_t, qg, kvg, wqt, wk, wvt, ones_rows):
    t = cqkv.shape[0]
    n = N_HEADS * HEAD_BLOCK
    nv = N_HEADS * VT_BLOCK
    row = lambda i: (i, 0)
    col = lambda i: (0, i)
    tab = pl.BlockSpec((TM_PREP, LANES), row)
    tab_t = pl.BlockSpec((LANES, TM_PREP), col)
    return pl.pallas_call(
        _mla_prep_kernel,
        grid=(t // TM_PREP,),
        in_specs=[pl.BlockSpec((TM_PREP, W_CQKV), row), tab, tab, tab, tab_t, tab_t, tab_t,
                  _full(qg.shape), _full(kvg.shape), _full(wqt.shape), _full(wk.shape), _full(wvt.shape),
                  _full(ones_rows.shape)],
        out_specs=[pl.BlockSpec((n, TM_PREP), col), pl.BlockSpec((TM_PREP, n), row),
                   pl.BlockSpec((nv, TM_PREP), col)],
        out_shape=[jax.ShapeDtypeStruct((n, t), BF16), jax.ShapeDtypeStruct((t, n), BF16),
                   jax.ShapeDtypeStruct((nv, t), BF16)],
        compiler_params=_cparams(("arbitrary",)),
        name="mla_prep",
    )(cqkv, *tabs, *tabs_t, qg, kvg, wqt, wk, wvt, ones_rows)


def _attn_kernel(qt_ref, k_ref, vt_ref, o_ref, m_ref, acc_ref):
    j = pl.program_id(1)
    key_chunk = lax.broadcasted_iota(jnp.int32, (TK, TQ), 0) // CHUNK
    qry_chunk = lax.broadcasted_iota(jnp.int32, (TK, TQ), 1) // CHUNK
    diag_mask = key_chunk <= qry_chunk

    def scores(h, sub, kt):
        k0 = pl.multiple_of(kt * TK, TK)
        kk = k_ref[pl.ds(k0, TK), h * HEAD_BLOCK:(h + 1) * HEAD_BLOCK]
        return jnp.dot(kk, qt_ref[h * HEAD_BLOCK:(h + 1) * HEAD_BLOCK, sub * TQ:(sub + 1) * TQ],
                       preferred_element_type=F32)

    def values_t(h, kt):
        k0 = pl.multiple_of(kt * TK, TK)
        return vt_ref[h * VT_BLOCK:(h + 1) * VT_BLOCK, pl.ds(k0, TK)]

    def first_update(h, sub, kt, masked, s):
        st = h * Q_SUBS + sub
        if masked:
            s = jnp.where(diag_mask, s, -jnp.inf)
        m = jnp.max(s, axis=0, keepdims=True)
        p = jnp.exp2(s - m)
        m_ref[st] = jnp.broadcast_to(m, (SUBLANES, TQ))
        acc_ref[st] = jnp.dot(values_t(h, kt), p.astype(BF16), preferred_element_type=F32)

    def update(h, sub, kt, masked, s):
        st = h * Q_SUBS + sub
        if masked:
            s = jnp.where(diag_mask, s, -jnp.inf)
        m_old = m_ref[st]
        m_new = jnp.maximum(m_old, jnp.max(s, axis=0, keepdims=True))
        alpha = jnp.exp2(m_old - m_new)
        p = jnp.exp2(s - m_new[0:1, :])
        m_ref[st] = m_new
        acc_ref[st] = alpha[0:1, :] * acc_ref[st] + jnp.dot(values_t(h, kt), p.astype(BF16),
                                                           preferred_element_type=F32)

    def run(work):
        pending = {}
        for idx in range(len(work) + ATTN_LOOKAHEAD):
            if idx < len(work):
                h, sub, kt, _, _ = work[idx]
                pending[idx] = scores(h, sub, kt)
            if idx >= ATTN_LOOKAHEAD:
                h, sub, kt, masked, upd = work[idx - ATTN_LOOKAHEAD]
                upd(h, sub, kt, masked, pending.pop(idx - ATTN_LOOKAHEAD))

    tail = []
    for h in range(N_HEADS):
        for sub in range(Q_SUBS):
            for kk in range(sub + 1):
                tail.append((h, sub, Q_SUBS * j + kk, kk == sub, first_update if kk == 0 else update))
    run(tail)

    def step(kt, carry):
        run([(h, sub, kt, False, update) for h in range(N_HEADS) for sub in range(Q_SUBS)])
        return carry

    lax.fori_loop(0, Q_SUBS * j, step, 0)

    for sub in range(Q_SUBS):
        outs = []
        for h in range(N_HEADS):
            st = h * Q_SUBS + sub
            outs.append(acc_ref[st, 0:V_HEAD, :] * (1.0 / acc_ref[st, V_HEAD:V_HEAD + 1, :]))
        o_ref[sub * TQ:(sub + 1) * TQ, :] = jnp.concatenate(outs, axis=0).T.astype(BF16)


def _attn_call(qt, k, vt, batch, seq):
    n = N_HEADS * HEAD_BLOCK
    tq = Q_SUBS * TQ
    nq = seq // tq
    return pl.pallas_call(
        _attn_kernel,
        grid=(batch, nq),
        in_specs=[pl.BlockSpec((n, tq), lambda b, i: (0, b * nq + i)),
                  pl.BlockSpec((seq, n), lambda b, i: (b, 0)),
                  pl.BlockSpec((N_HEADS * VT_BLOCK, seq), lambda b, i: (0, b))],
        out_specs=pl.BlockSpec((tq, N_HEADS * V_HEAD), lambda b, i: (b * nq + i, 0)),
        out_shape=jax.ShapeDtypeStruct((k.shape[0], N_HEADS * V_HEAD), BF16),
        scratch_shapes=[pltpu.VMEM((N_HEADS * Q_SUBS, SUBLANES, TQ), F32),
                        pltpu.VMEM((N_HEADS * Q_SUBS, VT_BLOCK, TQ), F32)],
        compiler_params=_cparams(("arbitrary", "arbitrary")),
        name="attn",
    )(qt, k, vt)


def _store_token_major(ref, x):
    rows = x.shape[0]
    for j in range(x.shape[1] // LANES):
        ref[pl.ds(j, rows, stride=SUBLANES), :] = x[:, j * LANES:(j + 1) * LANES]


def _load_token_major(ref, rows, d):
    return jnp.concatenate([ref[pl.ds(j, rows, stride=SUBLANES), :] for j in range(d // LANES)], axis=-1)


def _merge_kernel(alpha, ap_ref, ac_ref, at_ref, gl_ref, h_ref, p_ref, pp_ref, cp_ref, mp_ref, bg_ref,
                  wo_ref, g_ref, b_ref, pproj_ref, pgate_ref, wrh_ref, wrl_ref,
                  h1_ref, e_ref, lg_ref):
    d = h_ref.shape[1]
    sub = MERGE_SUB
    parts = [pl.ds(k * sub, sub) for k in range(h_ref.shape[0] // sub)]
    nt = (((1,), (1,)), ((), ()))

    def merged_of(rows):
        merged = None
        for br, (a_ref, w_ref) in enumerate(((ap_ref, pp_ref), (ac_ref, cp_ref), (at_ref, mp_ref))):
            y = jnp.dot(a_ref[rows, :], w_ref[...], preferred_element_type=F32)
            gate = jax.nn.sigmoid(gl_ref[rows, br * d:(br + 1) * d].astype(F32) + bg_ref[br:br + 1, :])
            merged = gate * y if merged is None else merged + gate * y
        return merged.astype(BF16)

    merged = [merged_of(rows) for rows in parts]
    ys = [jnp.dot(m, wo_ref[...], preferred_element_type=F32) for m in merged]
    es = [jnp.dot(p_ref[rows, :].astype(BF16), pproj_ref[...], preferred_element_type=F32) for rows in parts]
    h1s = [_layer_norm(alpha * h_ref[rows, :] + y, g_ref[...], b_ref[...]) for rows, y in zip(parts, ys)]
    for k, (rows, h1, e) in enumerate(zip(parts, h1s, es)):
        h1b = h1.astype(BF16)
        e = e * jax.nn.sigmoid(jnp.dot(h1b, pgate_ref[...], preferred_element_type=F32))
        e_ref[rows, :] = e.astype(BF16)
        h1l = (h1 - h1b.astype(F32)).astype(BF16)
        lg = lax.dot_general(wrh_ref[...], h1b, nt, preferred_element_type=F32)
        lg = lg + lax.dot_general(wrh_ref[...], h1l, nt, preferred_element_type=F32)
        lg = lg + lax.dot_general(wrl_ref[...], h1b, nt, preferred_element_type=F32)
        lg_ref[:, k * sub:(k + 1) * sub] = lg
        _store_token_major(h1_ref.at[pl.ds(k * sub * SUBLANES, sub * SUBLANES), :], h1)


def _merge_call(alpha, layer, ap, ac, at, gl, h, p, pp, cp, mp, bg, wo, g, b, pproj, pgate, wrh, wrl):
    t, d = h.shape
    tm = TM_MERGE
    row = lambda i: (i, 0)
    ins = [ap, ac, at, gl, h]
    in_specs = [pl.BlockSpec((tm, a.shape[1]), row) for a in ins]
    in_specs.append(pl.BlockSpec((None, tm, p.shape[-1]), lambda i: (layer, i, 0)))
    stacked = {id(a) for a in (pp, cp, mp, wo, pproj, pgate)}
    consts = [pp, cp, mp, bg, wo, g, b, pproj, pgate, wrh, wrl]
    in_specs += [_layer_full(a, layer) if id(a) in stacked else _full(a.shape) for a in consts]
    return pl.pallas_call(
        functools.partial(_merge_kernel, alpha),
        grid=(t // tm,),
        in_specs=in_specs,
        out_specs=[pl.BlockSpec((tm * SUBLANES, LANES), row), pl.BlockSpec((tm, d), row),
                   pl.BlockSpec((N_EXPERTS, tm), lambda i: (0, i))],
        out_shape=[jax.ShapeDtypeStruct((t * SUBLANES, LANES), F32), jax.ShapeDtypeStruct((t, d), BF16),
                   jax.ShapeDtypeStruct((N_EXPERTS, t), F32)],
        compiler_params=_cparams(("arbitrary",)),
        name="merge",
    )(*ins, p, *consts)


def _route_kernel(lg_ref, bias_ref, cls_ref, wa_ref, wb_ref):
    aff = [jax.nn.sigmoid(lg_ref[e]) for e in range(N_EXPERTS)]
    sel = [aff[e] + bias_ref[e] for e in range(N_EXPERTS)]
    n = EXPERTS_PER_GROUP

    def top2_sum(vals):
        best = None
        for a, b in PAIRS:
            s = vals[a] + vals[b]
            best = s if best is None else jnp.maximum(best, s)
        return best

    grp = jnp.zeros(aff[0].shape, jnp.int32)
    best = top2_sum(sel[0:n])
    for g in range(1, N_GROUPS):
        sc = top2_sum(sel[g * n:(g + 1) * n])
        better = sc > best
        grp = jnp.where(better, g, grp)
        best = jnp.where(better, sc, best)
    vs, afs = [], []
    for j in range(n):
        v, a = sel[j], aff[j]
        for g in range(1, N_GROUPS):
            v = jnp.where(grp == g, sel[g * n + j], v)
            a = jnp.where(grp == g, aff[g * n + j], a)
        vs.append(v)
        afs.append(a)
    first = jnp.zeros_like(grp)
    fv = vs[0]
    for j in range(1, n):
        better = vs[j] > fv
        first = jnp.where(better, j, first)
        fv = jnp.where(better, vs[j], fv)
    second = jnp.full_like(grp, -1)
    sv = jnp.full_like(fv, -jnp.inf)
    for j in range(n):
        better = (first != j) & ((second < 0) | (vs[j] > sv))
        second = jnp.where(better, j, second)
        sv = jnp.where(better, vs[j], sv)
    lo = jnp.minimum(first, second)
    hi = jnp.maximum(first, second)
    a_lo, a_hi = afs[0], afs[0]
    for j in range(1, n):
        a_lo = jnp.where(lo == j, afs[j], a_lo)
        a_hi = jnp.where(hi == j, afs[j], a_hi)
    pair = jnp.zeros_like(grp)
    for idx, (a, b) in enumerate(PAIRS):
        pair = jnp.where((lo == a) & (hi == b), idx, pair)
    tot = a_lo + a_hi
    cls_ref[...] = grp * len(PAIRS) + pair
    wa_ref[...] = a_lo / tot
    wb_ref[...] = a_hi / tot


def _route_call(lg3, bias):
    _, rows, lanes = lg3.shape
    blk = pl.BlockSpec((ROUTE_ROWS, lanes), lambda i: (i, 0))
    return pl.pallas_call(
        _route_kernel,
        grid=(rows // ROUTE_ROWS,),
        in_specs=[pl.BlockSpec((N_EXPERTS, ROUTE_ROWS, lanes), lambda i: (0, i, 0)),
                  pl.BlockSpec(memory_space=pltpu.SMEM)],
        out_specs=[blk, blk, blk],
        out_shape=[jax.ShapeDtypeStruct((rows, lanes), jnp.int32),
                   jax.ShapeDtypeStruct((rows, lanes), F32), jax.ShapeDtypeStruct((rows, lanes), F32)],
        compiler_params=_cparams(("arbitrary",)),
        name="route",
    )(lg3, bias)


GATHER_UNROLL = 8


def _moe_kernel(ta_ref, tb_ref, nused_ref, valid_ref,
                h_hbm, tok_ref, tok_next_ref, wa_ref, wb_ref, upa_ref, upb_ref, dna_ref, dnb_ref,
                m_hbm, xbuf, ybuf, gsem, ssem):
    i = pl.program_id(0)
    n_used = nused_ref[0]
    slot = i % 2
    rows = MOE_TILE * SUBLANES

    def row_copy_in(toks, s, r):
        tok = toks[0, 0, r]
        return pltpu.make_async_copy(
            h_hbm.at[pl.ds(pl.multiple_of(tok * SUBLANES, SUBLANES), SUBLANES), :],
            xbuf.at[s, pl.ds(pl.multiple_of(r * SUBLANES, SUBLANES), SUBLANES), :],
            gsem.at[s])

    def row_copy_out(toks, s, r):
        tok = toks[0, 0, r]
        return pltpu.make_async_copy(
            ybuf.at[s, pl.ds(pl.multiple_of(r * SUBLANES, SUBLANES), SUBLANES), :],
            m_hbm.at[pl.ds(pl.multiple_of(tok * SUBLANES, SUBLANES), SUBLANES), :],
            ssem.at[s])

    def start_rows(make, toks, s, count):
        def body8(c, carry):
            for u in range(GATHER_UNROLL):
                make(toks, s, c * GATHER_UNROLL + u).start()
            return carry

        def body1(r, carry):
            make(toks, s, r).start()
            return carry

        full = count // GATHER_UNROLL
        lax.fori_loop(0, full, body8, 0)
        lax.fori_loop(full * GATHER_UNROLL, count, body1, 0)

    def wait_gather(s):
        pltpu.make_async_copy(h_hbm.at[pl.ds(0, rows), :], xbuf.at[s], gsem.at[s]).wait()

    def wait_scatter(tile, s):
        n = pl.multiple_of(valid_ref[tile] * SUBLANES, SUBLANES)
        pltpu.make_async_copy(ybuf.at[s, pl.ds(0, n), :], m_hbm.at[pl.ds(0, n), :], ssem.at[s]).wait()

    @pl.when(i == 0)
    def _():
        start_rows(row_copy_in, tok_ref, 0, MOE_TILE)

    @pl.when(i + 1 < n_used)
    def _():
        start_rows(row_copy_in, tok_next_ref, 1 - slot, MOE_TILE)

    @pl.when(i < n_used)
    def _():
        wait_gather(slot)
        x = _load_token_major(xbuf.at[slot], MOE_TILE, upa_ref.shape[1]).astype(BF16)

        def ffn(up_ref, dn_ref):
            gu = jnp.dot(x, up_ref[0], preferred_element_type=F32)
            hid = jax.nn.silu(gu[:, :D_EXPERT]) * gu[:, D_EXPERT:]
            return jnp.dot(hid.astype(BF16), dn_ref[0], preferred_element_type=F32)

        y = ffn(upa_ref, dna_ref) * wa_ref[...] + ffn(upb_ref, dnb_ref) * wb_ref[...]

        @pl.when(i >= 2)
        def _():
            wait_scatter(i - 2, slot)

        _store_token_major(ybuf.at[slot], y)
        start_rows(row_copy_out, tok_ref, slot, valid_ref[i])

        @pl.when(i == n_used - 1)
        def _():
            @pl.when(i >= 1)
            def _():
                wait_scatter(i - 1, 1 - slot)
            wait_scatter(i, slot)


def _moe_call(layer, tile_a, tile_b, n_used, tile_valid, slot_tok, h1_tm, slot_wa, slot_wb, w_up, w_down,
              n_tiles, t):
    d = w_up.shape[1]
    e0 = layer * N_EXPERTS
    wspec = pl.BlockSpec((MOE_TILE, 1), lambda i, *_: (i, 0))
    tok_blk = (1, 1, MOE_TILE)
    grid_spec = pltpu.PrefetchScalarGridSpec(
        num_scalar_prefetch=4,
        grid=(n_tiles,),
        in_specs=[pl.BlockSpec(memory_space=pl.ANY),
                  pl.BlockSpec(tok_blk, lambda i, *_: (i, 0, 0), memory_space=pltpu.SMEM),
                  pl.BlockSpec(tok_blk, lambda i, *_: (jnp.minimum(i + 1, n_tiles - 1), 0, 0),
                               memory_space=pltpu.SMEM),
                  wspec, wspec,
                  pl.BlockSpec((1, d, 2 * D_EXPERT), lambda i, ta, tb, *_: (e0 + ta[i], 0, 0)),
                  pl.BlockSpec((1, d, 2 * D_EXPERT), lambda i, ta, tb, *_: (e0 + tb[i], 0, 0)),
                  pl.BlockSpec((1, D_EXPERT, d), lambda i, ta, tb, *_: (e0 + ta[i], 0, 0)),
                  pl.BlockSpec((1, D_EXPERT, d), lambda i, ta, tb, *_: (e0 + tb[i], 0, 0))],
        out_specs=pl.BlockSpec(memory_space=pl.ANY),
        scratch_shapes=[pltpu.VMEM((2, MOE_TILE * SUBLANES, LANES), F32),
                        pltpu.VMEM((2, MOE_TILE * SUBLANES, LANES), F32),
                        pltpu.SemaphoreType.DMA((2,)), pltpu.SemaphoreType.DMA((2,))])
    return pl.pallas_call(
        _moe_kernel,
        grid_spec=grid_spec,
        out_shape=jax.ShapeDtypeStruct((t * SUBLANES, LANES), F32),
        compiler_params=_cparams(("arbitrary",)),
        name="moe",
    )(tile_a, tile_b, n_used, tile_valid, h1_tm, slot_tok, slot_tok, slot_wa, slot_wb, w_up, w_up, w_down, w_down)


def _final_kernel(alpha, h1_ref, m_ref, e_ref, g_ref, b_ref, o_ref):
    rows, d = o_ref.shape
    h1 = _load_token_major(h1_ref, rows, d)
    m = _load_token_major(m_ref, rows, d)
    o_ref[...] = _layer_norm(alpha * h1 + m + e_ref[...].astype(F32), g_ref[...], b_ref[...])


def _final_call(alpha, h1_tm, m_tm, e, g, b):
    t, d = e.shape
    tm = TM_FINAL
    row = lambda i: (i, 0)
    tmaj = pl.BlockSpec((tm * SUBLANES, LANES), row)
    return pl.pallas_call(
        functools.partial(_final_kernel, alpha),
        grid=(t // tm,),
        in_specs=[tmaj, tmaj, pl.BlockSpec((tm, d), row), _full(g.shape), _full(b.shape)],
        out_specs=pl.BlockSpec((tm, d), row),
        out_shape=jax.ShapeDtypeStruct((t, d), F32),
        compiler_params=_cparams(("arbitrary",)),
        name="final_ln",
    )(h1_tm, m_tm, e, g, b)


def _pack_w_in(w_in):
    off_q = OFF_CONV + 2 * D_CONV
    off_kr = off_q + Q_LORA + KV_LORA
    off_gate = off_kr + QK_ROPE
    kr = jnp.pad(w_in[..., off_kr:off_gate], ((0, 0), (0, 0), (ROPE_LANE0, HEAD_BLOCK - ROPE_LANE0 - QK_ROPE)))
    return jnp.concatenate([w_in[..., :off_kr], kr, w_in[..., off_gate:]], axis=-1).astype(BF16)


def _pack_heads(w, lo, width, stride, lane0):
    blocks = []
    for h in range(N_HEADS):
        l0 = lane0(h)
        blocks.append(jnp.pad(w[..., h * stride + lo:h * stride + lo + width],
                              ((0, 0), (0, 0), (l0, HEAD_BLOCK - l0 - width))))
    return jnp.concatenate(blocks, axis=-1).astype(BF16)


def _rope_tables(positions):
    inv_freq = jnp.power(ROPE_THETA, -jnp.arange(0, QK_ROPE, 2, dtype=F32) / QK_ROPE)
    ang = positions.astype(F32).reshape(-1, 1) * inv_freq
    cos, sin = jnp.cos(ang), jnp.sin(ang)
    t = ang.shape[0]
    ones_lo = jnp.ones((t, ROPE_LANE0), F32)
    ones_hi = jnp.ones((t, HEAD_BLOCK - ROPE_LANE0 - QK_ROPE), F32)
    zeros_lo = jnp.zeros((t, ROPE_LANE0), F32)
    zeros_half = jnp.zeros((t, HALF_ROPE), F32)
    zeros_hi = jnp.zeros((t, HEAD_BLOCK - ROPE_LANE0 - QK_ROPE), F32)
    c = jnp.concatenate([ones_lo, cos, cos, ones_hi], axis=-1)
    s1 = jnp.concatenate([zeros_lo, -sin, zeros_half, zeros_hi], axis=-1)
    s2 = jnp.concatenate([zeros_lo, zeros_half, sin, zeros_hi], axis=-1)
    return c, s1, s2


def _routing_plan(cls, wa, wb, n_tiles):
    t = cls.shape[0]
    n_pad = n_tiles * MOE_TILE - t
    assert n_pad == N_CLASSES * MOE_TILE, n_pad
    classes = jnp.arange(N_CLASSES, dtype=jnp.int32)
    counts = jnp.sum((cls[None, :] == classes[:, None]).astype(jnp.int32), axis=1)
    padded = (counts + MOE_TILE - 1) // MOE_TILE * MOE_TILE
    pad_end = jnp.cumsum(padded)
    pad_start = pad_end - padded
    pad_need = padded - counts
    j = jnp.arange(MOE_TILE, dtype=jnp.int32)[None, :]
    pad_key = jnp.where(j < pad_need[:, None], 2 * classes[:, None] + 1, 2 * N_CLASSES).reshape(-1)
    zeros_i = jnp.zeros((n_pad,), jnp.int32)
    zeros_f = jnp.zeros((n_pad,), F32)
    _, slot_tok, slot_wa, slot_wb = lax.sort(
        (jnp.concatenate([2 * cls, pad_key]), jnp.concatenate([jnp.arange(t, dtype=jnp.int32), zeros_i]),
         jnp.concatenate([wa, zeros_f]), jnp.concatenate([wb, zeros_f])), num_keys=1)
    slot_tok = slot_tok.reshape(n_tiles, 1, MOE_TILE)
    slot_wa = slot_wa.reshape(-1, 1)
    slot_wb = slot_wb.reshape(-1, 1)
    tile_row0 = jnp.arange(n_tiles, dtype=jnp.int32) * MOE_TILE
    tile_cls = jnp.minimum(jnp.sum((tile_row0[:, None] >= pad_end[None, :]).astype(jnp.int32), axis=1),
                           N_CLASSES - 1)
    onehot = (tile_cls[:, None] == classes[None, :]).astype(jnp.int32)
    tile_valid = jnp.clip(jnp.sum(onehot * (pad_start + counts)[None, :], axis=1) - tile_row0,
                          0, MOE_TILE).astype(jnp.int32)
    pair_lo = jnp.array([p[0] for p in PAIRS], jnp.int32)
    pair_hi = jnp.array([p[1] for p in PAIRS], jnp.int32)
    grp = tile_cls // len(PAIRS)
    pair_onehot = ((tile_cls % len(PAIRS))[:, None] == jnp.arange(len(PAIRS), dtype=jnp.int32)[None, :])
    tile_a = grp * EXPERTS_PER_GROUP + jnp.sum(pair_onehot * pair_lo[None, :], axis=1)
    tile_b = grp * EXPERTS_PER_GROUP + jnp.sum(pair_onehot * pair_hi[None, :], axis=1)
    n_used = (pad_end[-1] // MOE_TILE).astype(jnp.int32).reshape(1)
    return tile_a, tile_b, n_used, tile_valid, slot_tok, slot_wa, slot_wb


def kernel(x, p, positions, ln_in_g, ln_in_b, w_in, b_gate, pool_w, pool_scale, pool_proj, conv_dw, conv_b, conv_ln_g, conv_ln_b, conv_proj, q_norm_g, w_uq, kv_norm_g, w_ukv, mla_proj, w_out, ln1_g, ln1_b, w_router, router_bias, exp_w_up, exp_w_down, ple_proj, ple_gate, ln2_g, ln2_b):
    batch, seq, d = x.shape
    depth = w_in.shape[0]
    t = batch * seq
    alpha = float((2 * depth) ** 0.25)
    n_tiles = (t + N_CLASSES * (MOE_TILE - 1)) // MOE_TILE + 1

    w_in_p = _pack_w_in(w_in)
    pool_w_b = pool_w.astype(BF16)
    pool_proj_b, conv_proj_b, mla_proj_b = pool_proj.astype(BF16), conv_proj.astype(BF16), mla_proj.astype(BF16)
    qk = QK_NOPE + QK_ROPE
    wq_p = (_pack_heads(w_uq, 0, QK_NOPE, qk, lambda h: 0)
            + _pack_heads(w_uq, QK_NOPE, QK_ROPE, qk, lambda h: ROPE_LANE0))
    wqt_p = jnp.swapaxes(wq_p, 1, 2)
    wk_p = _pack_heads(w_ukv, 0, QK_NOPE, QK_NOPE + V_HEAD, lambda h: 0)
    kv_w = QK_NOPE + V_HEAD
    wvt_p = jnp.swapaxes(jnp.concatenate(
        [jnp.pad(w_ukv[..., h * kv_w + QK_NOPE:(h + 1) * kv_w], ((0, 0), (0, 0), (0, BF16_ROWS)))
         for h in range(N_HEADS)], axis=-1), 1, 2).astype(BF16)
    ones_rows = jnp.tile(jnp.concatenate([jnp.zeros((V_HEAD, 1), F32), jnp.ones((BF16_ROWS, 1), F32)]),
                         (N_HEADS, 1))
    w_out_b, ple_proj_b, ple_gate_b = w_out.astype(BF16), ple_proj.astype(BF16), ple_gate.astype(BF16)
    w_up_b = exp_w_up.astype(BF16).reshape((depth * N_EXPERTS,) + exp_w_up.shape[2:])
    w_down_b = exp_w_down.astype(BF16).reshape((depth * N_EXPERTS,) + exp_w_down.shape[2:])
    wr_t = w_router.T
    wr_hi = wr_t.astype(BF16)
    wr_lo = (wr_t - wr_hi.astype(F32)).astype(BF16)
    rope_tabs = _rope_tables(positions)
    rope_tabs_t = tuple(a.T for a in rope_tabs)
    p2 = p.reshape(depth, t, -1)

    pre, pre_g, pre_b = (x.reshape(t, d),), ln_in_g, ln_in_b
    for i in range(depth):
        h, u_pool, u_conv, cqkv, glog = _in_proj_call(alpha, pre, pre_g.reshape(1, d), pre_b.reshape(1, d),
                                                      w_in_p, i)
        a_pool = _pool_call(u_pool, pool_w_b[i], pool_scale[i].reshape(1, -1), batch, seq)
        a_conv = _conv_call(u_conv, conv_dw[i], conv_b[i].reshape(1, -1), conv_ln_g[i].reshape(1, -1),
                            conv_ln_b[i].reshape(1, -1), batch, seq)
        qt, k, vt = _mla_prep_call(cqkv, rope_tabs, rope_tabs_t, q_norm_g[i].reshape(1, -1),
                                   kv_norm_g[i].reshape(1, -1), wqt_p[i], wk_p[i], wvt_p[i], ones_rows)
        a_attn = _attn_call(qt, k, vt, batch, seq)
        h1_tm, e, logits = _merge_call(alpha, i, a_pool, a_conv, a_attn, glog, h, p2,
                                       pool_proj_b, conv_proj_b, mla_proj_b, b_gate[i], w_out_b,
                                       ln1_g[i].reshape(1, d), ln1_b[i].reshape(1, d),
                                       ple_proj_b, ple_gate_b, wr_hi, wr_lo)
        cls, wa, wb = _route_call(logits.reshape(N_EXPERTS, t // LANES, LANES), router_bias)
        plan = _routing_plan(cls.reshape(t), wa.reshape(t), wb.reshape(t), n_tiles)
        m_tm = _moe_call(i, *plan[:5], h1_tm, plan[5], plan[6], w_up_b, w_down_b, n_tiles, t)
        pre, pre_g, pre_b = (h1_tm, m_tm, e), ln2_g[i], ln2_b[i]
    h = _final_call(alpha, *pre, pre_g.reshape(1, d), pre_b.reshape(1, d))
    return h.reshape(batch, seq, d)
```

```python
import functools

import jax
import jax.numpy as jnp
from jax import lax
from jax.experimental import pallas as pl
from jax.experimental.pallas import tpu as pltpu

F32 = jnp.float32
BF16 = jnp.bfloat16

CHUNK = 64
POOL_WINDOWS = (2, 4, 8, 16)
POOL_GW = 128
D_POOL = 512
D_CONV = 512
CONV_WIDTH = 31
N_HEADS = 8
QK_NOPE = 64
QK_ROPE = 32
V_HEAD = 64
Q_LORA = 384
KV_LORA = 256
ROPE_THETA = 10000.0
N_EXPERTS = 16
N_GROUPS = 4
EXPERTS_PER_GROUP = 4
D_EXPERT = 512
LN_EPS = 1e-5
RMS_EPS = 1e-6

LANES = 128
SUBLANES = 8
HEAD_BLOCK = LANES
ROPE_LANE0 = QK_NOPE
HALF_ROPE = QK_ROPE // 2
BF16_ROWS = 16
VT_BLOCK = V_HEAD + BF16_ROWS
LOG2_E = 1.4426950408889634

TM_PROJ = 512
TM_PREP = 512
TM_MERGE = 512
MERGE_SUB = 256
TM_FINAL = 512
TQ = 256
TK = 256
Q_SUBS = 2
ATTN_LOOKAHEAD = 3
POOL_CHUNK = 256
POOL_HALO = 16
CONV_CHUNK = 128
CONV_HALO = 32
MOE_TILE = 256
ROUTE_ROWS = 8

PAIRS = ((0, 1), (0, 2), (0, 3), (1, 2), (1, 3), (2, 3))
N_CLASSES = N_GROUPS * len(PAIRS)

VMEM_LIMIT = 56 * 1024 * 1024


def _cparams(sem):
    return pltpu.CompilerParams(dimension_semantics=sem, vmem_limit_bytes=VMEM_LIMIT)


def _layer_norm(x, g, b):
    mu = jnp.mean(x, axis=-1, keepdims=True)
    xc = x - mu
    var = jnp.mean(xc * xc, axis=-1, keepdims=True)
    return xc * lax.rsqrt(var + LN_EPS) * g + b


def _rms_norm(x, g):
    ms = jnp.mean(x * x, axis=-1, keepdims=True)
    return x * lax.rsqrt(ms + RMS_EPS) * g


def _full(shape):
    n = len(shape)
    return pl.BlockSpec(shape, lambda *_: (0,) * n)


def _layer_full(stacked, layer):
    n = stacked.ndim - 1
    return pl.BlockSpec((None,) + stacked.shape[1:], lambda *_: (layer,) + (0,) * n)


OFF_CONV = D_POOL
OFF_CQKV = OFF_CONV + 2 * D_CONV
W_CQKV = Q_LORA + KV_LORA + HEAD_BLOCK
OFF_GATE = OFF_CQKV + W_CQKV
N_COLS_CHUNK = 512


def _in_proj_kernel(alpha, residual, *refs):
    if residual:
        h1_ref, m_ref, e_ref, g_ref, b_ref, w_ref, h_ref, pool_ref, conv_ref, cqkv_ref, gate_ref = refs
        rows, d = h_ref.shape
        pre = alpha * _load_token_major(h1_ref, rows, d) + _load_token_major(m_ref, rows, d)
        pre = pre + e_ref[...].astype(F32)
    else:
        x_ref, g_ref, b_ref, w_ref, h_ref, pool_ref, conv_ref, cqkv_ref, gate_ref = refs
        pre = x_ref[...]
    h = _layer_norm(pre, g_ref[...], b_ref[...])
    h_ref[...] = h
    x = h.astype(BF16)

    def mm(lo, hi):
        return jnp.dot(x, w_ref[:, lo:hi], preferred_element_type=F32).astype(BF16)

    pool_ref[...] = mm(0, OFF_CONV)
    for c in range(2 * D_CONV // N_COLS_CHUNK):
        conv_ref[:, c * N_COLS_CHUNK:(c + 1) * N_COLS_CHUNK] = mm(
            OFF_CONV + c * N_COLS_CHUNK, OFF_CONV + (c + 1) * N_COLS_CHUNK)
    cqkv_ref[...] = mm(OFF_CQKV, OFF_GATE)
    for c in range(gate_ref.shape[1] // N_COLS_CHUNK):
        gate_ref[:, c * N_COLS_CHUNK:(c + 1) * N_COLS_CHUNK] = mm(
            OFF_GATE + c * N_COLS_CHUNK, OFF_GATE + (c + 1) * N_COLS_CHUNK)


def _in_proj_call(alpha, pre, g, b, w, layer):
    residual = len(pre) == 3
    t, d = pre[-1].shape
    n = w.shape[-1]
    n_gate = n - OFF_GATE
    row = lambda i: (i, 0)
    tmaj = pl.BlockSpec((TM_PROJ * SUBLANES, LANES), row)
    pre_specs = [tmaj, tmaj, pl.BlockSpec((TM_PROJ, d), row)] if residual else [pl.BlockSpec((TM_PROJ, d), row)]
    return pl.pallas_call(
        functools.partial(_in_proj_kernel, alpha, residual),
        grid=(t // TM_PROJ,),
        in_specs=pre_specs + [_full(g.shape), _full(b.shape), _layer_full(w, layer)],
        out_specs=[pl.BlockSpec((TM_PROJ, d), row),
                   pl.BlockSpec((TM_PROJ, D_POOL), row), pl.BlockSpec((TM_PROJ, 2 * D_CONV), row),
                   pl.BlockSpec((TM_PROJ, W_CQKV), row), pl.BlockSpec((TM_PROJ, n_gate), row)],
        out_shape=[jax.ShapeDtypeStruct((t, d), F32),
                   jax.ShapeDtypeStruct((t, D_POOL), BF16), jax.ShapeDtypeStruct((t, 2 * D_CONV), BF16),
                   jax.ShapeDtypeStruct((t, W_CQKV), BF16), jax.ShapeDtypeStruct((t, n_gate), BF16)],
        compiler_params=_cparams(("arbitrary",)),
        name="in_proj",
    )(*pre, g, b, w)


def _pool_kernel(u_ref, w_ref, scale_ref, o_ref, buf):
    s = o_ref.shape[0]
    buf[0:POOL_HALO, :] = jnp.zeros((POOL_HALO, D_POOL), F32)

    def to_f32(c, carry):
        r = pl.multiple_of(c * POOL_CHUNK, POOL_CHUNK)
        buf[pl.ds(POOL_HALO + r, POOL_CHUNK), :] = u_ref[pl.ds(r, POOL_CHUNK), :].astype(F32)
        return carry

    lax.fori_loop(0, s // POOL_CHUNK, to_f32, 0)

    def body(c, carry):
        r0 = pl.multiple_of(c * POOL_CHUNK, POOL_CHUNK)
        t = r0 + lax.broadcasted_iota(jnp.int32, (POOL_CHUNK, 1), 0)
        for g, w in enumerate(POOL_WINDOWS):
            cols = slice(g * POOL_GW, (g + 1) * POOL_GW)
            xw = buf[pl.ds(r0, POOL_CHUNK + POOL_HALO), cols]
            acc = xw
            k = 1
            while k < w:
                acc = acc + pltpu.roll(acc, k, axis=0)
                k *= 2
            cnt = jnp.minimum(t + 1, w).astype(F32)
            mixed = acc[POOL_HALO:] / cnt - xw[POOL_HALO:]
            y = jnp.dot(mixed.astype(BF16), w_ref[g], preferred_element_type=F32) * scale_ref[:, cols]
            o_ref[pl.ds(r0, POOL_CHUNK), cols] = y.astype(BF16)
        return carry

    lax.fori_loop(0, s // POOL_CHUNK, body, 0)


def _pool_call(u, w, scale, batch, seq):
    return pl.pallas_call(
        _pool_kernel,
        grid=(batch,),
        in_specs=[pl.BlockSpec((seq, D_POOL), lambda b: (b, 0)),
                  _full(w.shape), _full(scale.shape)],
        out_specs=pl.BlockSpec((seq, D_POOL), lambda b: (b, 0)),
        out_shape=jax.ShapeDtypeStruct(u.shape, BF16),
        scratch_shapes=[pltpu.VMEM((seq + POOL_HALO, D_POOL), F32)],
        compiler_params=_cparams(("arbitrary",)),
        name="pool",
    )(u, w, scale)


GLU_CHUNK = 256


def _conv_kernel(u_ref, dw_ref, cb_ref, g_ref, b_ref, o_ref, zs):
    s = o_ref.shape[0]
    zs[0:CONV_HALO, :] = jnp.zeros((CONV_HALO, D_CONV), F32)

    def glu(c, carry):
        r = pl.multiple_of(c * GLU_CHUNK, GLU_CHUNK)
        a = u_ref[pl.ds(r, GLU_CHUNK), 0:D_CONV].astype(F32)
        gate = u_ref[pl.ds(r, GLU_CHUNK), D_CONV:2 * D_CONV].astype(F32)
        zs[pl.ds(CONV_HALO + r, GLU_CHUNK), :] = a * jax.nn.sigmoid(gate)
        return carry

    lax.fori_loop(0, s // GLU_CHUNK, glu, 0)

    def body(c, carry):
        r0 = pl.multiple_of(c * CONV_CHUNK, CONV_CHUNK)
        parts = []
        for cg in range(D_CONV // LANES):
            cols = slice(cg * LANES, (cg + 1) * LANES)
            win = zs[pl.ds(r0, CONV_CHUNK + CONV_HALO), cols]
            acc = jnp.zeros((CONV_CHUNK, LANES), F32) + cb_ref[:, cols]
            for sub in range(SUBLANES):
                shifted = win if sub == 0 else pltpu.roll(win, sub, axis=0)
                for a in range(CONV_HALO // SUBLANES):
                    lag = SUBLANES * a + sub
                    if lag >= CONV_WIDTH:
                        continue
                    k = CONV_WIDTH - 1 - lag
                    lo = CONV_HALO - SUBLANES * a
                    acc = acc + dw_ref[k:k + 1, cols] * shifted[lo:lo + CONV_CHUNK]
            parts.append(acc)
        y = jnp.concatenate(parts, axis=-1)
        y = _layer_norm(y, g_ref[...], b_ref[...])
        y = y * jax.nn.sigmoid(y)
        o_ref[pl.ds(r0, CONV_CHUNK), :] = y.astype(BF16)
        return carry

    lax.fori_loop(0, s // CONV_CHUNK, body, 0)


def _conv_call(u, dw, cb, g, b, batch, seq):
    return pl.pallas_call(
        _conv_kernel,
        grid=(batch,),
        in_specs=[pl.BlockSpec((seq, 2 * D_CONV), lambda i: (i, 0)),
                  _full(dw.shape), _full(cb.shape), _full(g.shape), _full(b.shape)],
        out_specs=pl.BlockSpec((seq, D_CONV), lambda i: (i, 0)),
        out_shape=jax.ShapeDtypeStruct((u.shape[0], D_CONV), BF16),
        scratch_shapes=[pltpu.VMEM((seq + CONV_HALO, D_CONV), F32)],
        compiler_params=_cparams(("arbitrary",)),
        name="conv",
    )(u, dw, cb, g, b)


def _rope_block(x, c, s1, s2):
    return x * c + pltpu.roll(x, LANES - HALF_ROPE, axis=1) * s1 + pltpu.roll(x, HALF_ROPE, axis=1) * s2


def _rope_block_t(x, c, s1, s2):
    return x * c + pltpu.roll(x, HEAD_BLOCK - HALF_ROPE, axis=0) * s1 + pltpu.roll(x, HALF_ROPE, axis=0) * s2


def _mla_prep_kernel(cqkv_ref, c_ref, s1_ref, s2_ref, ct_ref, s1t_ref, s2t_ref, qg_ref, kvg_ref,
                     wqt_ref, wk_ref, wvt_ref, ones_ref, qt_ref, k_ref, vt_ref):
    cq = _rms_norm(cqkv_ref[:, 0:Q_LORA].astype(F32), qg_ref[...]).astype(BF16)
    ckv = _rms_norm(cqkv_ref[:, Q_LORA:Q_LORA + KV_LORA].astype(F32), kvg_ref[...]).astype(BF16)
    kr = _rope_block(cqkv_ref[:, Q_LORA + KV_LORA:W_CQKV].astype(F32), c_ref[...], s1_ref[...], s2_ref[...])
    scale = float((QK_NOPE + QK_ROPE) ** -0.5 * LOG2_E)
    nt = (((1,), (1,)), ((), ()))
    ct, s1t, s2t = ct_ref[...], s1t_ref[...], s2t_ref[...]
    for h in range(N_HEADS):
        cols = slice(h * HEAD_BLOCK, (h + 1) * HEAD_BLOCK)
        qt = lax.dot_general(wqt_ref[cols, :], cq, nt, preferred_element_type=F32) * scale
        qt_ref[cols, :] = _rope_block_t(qt, ct, s1t, s2t).astype(BF16)
        k = jnp.dot(ckv, wk_ref[:, cols], preferred_element_type=F32) + kr
        k_ref[:, cols] = k.astype(BF16)
    vt = lax.dot_general(wvt_ref[...], ckv, nt, preferred_element_type=F32) + ones_ref[...]
    vt_ref[...] = vt.astype(BF16)


def _mla_prep_call(cqkv, tabs, tabs_t, qg, kvg, wqt, wk, wvt, ones_rows):
    t = cqkv.shape[0]
    n = N_HEADS * HEAD_BLOCK
    nv = N_HEADS * VT_BLOCK
    row = lambda i: (i, 0)
    col = lambda i: (0, i)
    tab = pl.BlockSpec((TM_PREP, LANES), row)
    tab_t = pl.BlockSpec((LANES, TM_PREP), col)
    return pl.pallas_call(
        _mla_prep_kernel,
        grid=(t // TM_PREP,),
        in_specs=[pl.BlockSpec((TM_PREP, W_CQKV), row), tab, tab, tab, tab_t, tab_t, tab_t,
                  _full(qg.shape), _full(kvg.shape), _full(wqt.shape), _full(wk.shape), _full(wvt.shape),
                  _full(ones_rows.shape)],
        out_specs=[pl.BlockSpec((n, TM_PREP), col), pl.BlockSpec((TM_PREP, n), row),
                   pl.BlockSpec((nv, TM_PREP), col)],
        out_shape=[jax.ShapeDtypeStruct((n, t), BF16), jax.ShapeDtypeStruct((t, n), BF16),
                   jax.ShapeDtypeStruct((nv, t), BF16)],
        compiler_params=_cparams(("arbitrary",)),
        name="mla_prep",
    )(cqkv, *tabs, *tabs_t, qg, kvg, wqt, wk, wvt, ones_rows)


def _attn_kernel(qt_ref, k_ref, vt_ref, o_ref, m_ref, acc_ref):
    j = pl.program_id(1)
    key_chunk = lax.broadcasted_iota(jnp.int32, (TK, TQ), 0) // CHUNK
    qry_chunk = lax.broadcasted_iota(jnp.int32, (TK, TQ), 1) // CHUNK
    diag_mask = key_chunk <= qry_chunk

    def scores(h, sub, kt):
        k0 = pl.multiple_of(kt * TK, TK)
        kk = k_ref[pl.ds(k0, TK), h * HEAD_BLOCK:(h + 1) * HEAD_BLOCK]
        return jnp.dot(kk, qt_ref[h * HEAD_BLOCK:(h + 1) * HEAD_BLOCK, sub * TQ:(sub + 1) * TQ],
                       preferred_element_type=F32)

    def values_t(h, kt):
        k0 = pl.multiple_of(kt * TK, TK)
        return vt_ref[h * VT_BLOCK:(h + 1) * VT_BLOCK, pl.ds(k0, TK)]

    def first_update(h, sub, kt, masked, s):
        st = h * Q_SUBS + sub
        if masked:
            s = jnp.where(diag_mask, s, -jnp.inf)
        m = jnp.max(s, axis=0, keepdims=True)
        p = jnp.exp2(s - m)
        m_ref[st] = jnp.broadcast_to(m, (SUBLANES, TQ))
        acc_ref[st] = jnp.dot(values_t(h, kt), p.astype(BF16), preferred_element_type=F32)

    def update(h, sub, kt, masked, s):
        st = h * Q_SUBS + sub
        if masked:
            s = jnp.where(diag_mask, s, -jnp.inf)
        m_old = m_ref[st]
        m_new = jnp.maximum(m_old, jnp.max(s, axis=0, keepdims=True))
        alpha = jnp.exp2(m_old - m_new)
        p = jnp.exp2(s - m_new[0:1, :])
        m_ref[st] = m_new
        acc_ref[st] = alpha[0:1, :] * acc_ref[st] + jnp.dot(values_t(h, kt), p.astype(BF16),
                                                           preferred_element_type=F32)

    def run(work):
        pending = {}
        for idx in range(len(work) + ATTN_LOOKAHEAD):
            if idx < len(work):
                h, sub, kt, _, _ = work[idx]
                pending[idx] = scores(h, sub, kt)
            if idx >= ATTN_LOOKAHEAD:
                h, sub, kt, masked, upd = work[idx - ATTN_LOOKAHEAD]
                upd(h, sub, kt, masked, pending.pop(idx - ATTN_LOOKAHEAD))

    tail = []
    for h in range(N_HEADS):
        for sub in range(Q_SUBS):
            for kk in range(sub + 1):
                tail.append((h, sub, Q_SUBS * j + kk, kk == sub, first_update if kk == 0 else update))
    run(tail)

    def step(kt, carry):
        run([(h, sub, kt, False, update) for h in range(N_HEADS) for sub in range(Q_SUBS)])
        return carry

    lax.fori_loop(0, Q_SUBS * j, step, 0)

    for sub in range(Q_SUBS):
        outs = []
        for h in range(N_HEADS):
            st = h * Q_SUBS + sub
            outs.append(acc_ref[st, 0:V_HEAD, :] * (1.0 / acc_ref[st, V_HEAD:V_HEAD + 1, :]))
        o_ref[sub * TQ:(sub + 1) * TQ, :] = jnp.concatenate(outs, axis=0).T.astype(BF16)


def _attn_call(qt, k, vt, batch, seq):
    n = N_HEADS * HEAD_BLOCK
    tq = Q_SUBS * TQ
    nq = seq // tq
    return pl.pallas_call(
        _attn_kernel,
        grid=(batch, nq),
        in_specs=[pl.BlockSpec((n, tq), lambda b, i: (0, b * nq + i)),
                  pl.BlockSpec((seq, n), lambda b, i: (b, 0)),
                  pl.BlockSpec((N_HEADS * VT_BLOCK, seq), lambda b, i: (0, b))],
        out_specs=pl.BlockSpec((tq, N_HEADS * V_HEAD), lambda b, i: (b * nq + i, 0)),
        out_shape=jax.ShapeDtypeStruct((k.shape[0], N_HEADS * V_HEAD), BF16),
        scratch_shapes=[pltpu.VMEM((N_HEADS * Q_SUBS, SUBLANES, TQ), F32),
                        pltpu.VMEM((N_HEADS * Q_SUBS, VT_BLOCK, TQ), F32)],
        compiler_params=_cparams(("arbitrary", "arbitrary")),
        name="attn",
    )(qt, k, vt)


def _store_token_major(ref, x):
    rows = x.shape[0]
    for j in range(x.shape[1] // LANES):
        ref[pl.ds(j, rows, stride=SUBLANES), :] = x[:, j * LANES:(j + 1) * LANES]


def _load_token_major(ref, rows, d):
    return jnp.concatenate([ref[pl.ds(j, rows, stride=SUBLANES), :] for j in range(d // LANES)], axis=-1)


def _merge_kernel(alpha, ap_ref, ac_ref, at_ref, gl_ref, h_ref, p_ref, pp_ref, cp_ref, mp_ref, bg_ref,
                  wo_ref, g_ref, b_ref, pproj_ref, pgate_ref, wrh_ref, wrl_ref,
                  h1_ref, e_ref, lg_ref):
    d = h_ref.shape[1]
    sub = MERGE_SUB
    parts = [pl.ds(k * sub, sub) for k in range(h_ref.shape[0] // sub)]
    nt = (((1,), (1,)), ((), ()))

    def merged_of(rows):
        merged = None
        for br, (a_ref, w_ref) in enumerate(((ap_ref, pp_ref), (ac_ref, cp_ref), (at_ref, mp_ref))):
            y = jnp.dot(a_ref[rows, :], w_ref[...], preferred_element_type=F32)
            gate = jax.nn.sigmoid(gl_ref[rows, br * d:(br + 1) * d].astype(F32) + bg_ref[br:br + 1, :])
            merged = gate * y if merged is None else merged + gate * y
        return merged.astype(BF16)

    merged = [merged_of(rows) for rows in parts]
    ys = [jnp.dot(m, wo_ref[...], preferred_element_type=F32) for m in merged]
    es = [jnp.dot(p_ref[rows, :].astype(BF16), pproj_ref[...], preferred_element_type=F32) for rows in parts]
    h1s = [_layer_norm(alpha * h_ref[rows, :] + y, g_ref[...], b_ref[...]) for rows, y in zip(parts, ys)]
    for k, (rows, h1, e) in enumerate(zip(parts, h1s, es)):
        h1b = h1.astype(BF16)
        e = e * jax.nn.sigmoid(jnp.dot(h1b, pgate_ref[...], preferred_element_type=F32))
        e_ref[rows, :] = e.astype(BF16)
        h1l = (h1 - h1b.astype(F32)).astype(BF16)
        lg = lax.dot_general(wrh_ref[...], h1b, nt, preferred_element_type=F32)
        lg = lg + lax.dot_general(wrh_ref[...], h1l, nt, preferred_element_type=F32)
        lg = lg + lax.dot_general(wrl_ref[...], h1b, nt, preferred_element_type=F32)
        lg_ref[:, k * sub:(k + 1) * sub] = lg
        _store_token_major(h1_ref.at[pl.ds(k * sub * SUBLANES, sub * SUBLANES), :], h1)


def _merge_call(alpha, layer, ap, ac, at, gl, h, p, pp, cp, mp, bg, wo, g, b, pproj, pgate, wrh, wrl):
    t, d = h.shape
    tm = TM_MERGE
    row = lambda i: (i, 0)
    ins = [ap, ac, at, gl, h]
    in_specs = [pl.BlockSpec((tm, a.shape[1]), row) for a in ins]
    in_specs.append(pl.BlockSpec((None, tm, p.shape[-1]), lambda i: (layer, i, 0)))
    stacked = {id(a) for a in (pp, cp, mp, wo, pproj, pgate)}
    consts = [pp, cp, mp, bg, wo, g, b, pproj, pgate, wrh, wrl]
    in_specs += [_layer_full(a, layer) if id(a) in stacked else _full(a.shape) for a in consts]
    return pl.pallas_call(
        functools.partial(_merge_kernel, alpha),
        grid=(t // tm,),
        in_specs=in_specs,
        out_specs=[pl.BlockSpec((tm * SUBLANES, LANES), row), pl.BlockSpec((tm, d), row),
                   pl.BlockSpec((N_EXPERTS, tm), lambda i: (0, i))],
        out_shape=[jax.ShapeDtypeStruct((t * SUBLANES, LANES), F32), jax.ShapeDtypeStruct((t, d), BF16),
                   jax.ShapeDtypeStruct((N_EXPERTS, t), F32)],
        compiler_params=_cparams(("arbitrary",)),
        name="merge",
    )(*ins, p, *consts)


def _route_kernel(lg_ref, bias_ref, cls_ref, wa_ref, wb_ref):
    aff = [jax.nn.sigmoid(lg_ref[e]) for e in range(N_EXPERTS)]
    sel = [aff[e] + bias_ref[e] for e in range(N_EXPERTS)]
    n = EXPERTS_PER_GROUP

    def top2_sum(vals):
        best = None
        for a, b in PAIRS:
            s = vals[a] + vals[b]
            best = s if best is None else jnp.maximum(best, s)
        return best

    grp = jnp.zeros(aff[0].shape, jnp.int32)
    best = top2_sum(sel[0:n])
    for g in range(1, N_GROUPS):
        sc = top2_sum(sel[g * n:(g + 1) * n])
        better = sc > best
        grp = jnp.where(better, g, grp)
        best = jnp.where(better, sc, best)
    vs, afs = [], []
    for j in range(n):
        v, a = sel[j], aff[j]
        for g in range(1, N_GROUPS):
            v = jnp.where(grp == g, sel[g * n + j], v)
            a = jnp.where(grp == g, aff[g * n + j], a)
        vs.append(v)
        afs.append(a)
    first = jnp.zeros_like(grp)
    fv = vs[0]
    for j in range(1, n):
        better = vs[j] > fv
        first = jnp.where(better, j, first)
        fv = jnp.where(better, vs[j], fv)
    second = jnp.full_like(grp, -1)
    sv = jnp.full_like(fv, -jnp.inf)
    for j in range(n):
        better = (first != j) & ((second < 0) | (vs[j] > sv))
        second = jnp.where(better, j, second)
        sv = jnp.where(better, vs[j], sv)
    lo = jnp.minimum(first, second)
    hi = jnp.maximum(first, second)
    a_lo, a_hi = afs[0], afs[0]
    for j in range(1, n):
        a_lo = jnp.where(lo == j, afs[j], a_lo)
        a_hi = jnp.where(hi == j, afs[j], a_hi)
    pair = jnp.zeros_like(grp)
    for idx, (a, b) in enumerate(PAIRS):
        pair = jnp.where((lo == a) & (hi == b), idx, pair)
    tot = a_lo + a_hi
    cls_ref[...] = grp * len(PAIRS) + pair
    wa_ref[...] = a_lo / tot
    wb_ref[...] = a_hi / tot


def _route_call(lg3, bias):
    _, rows, lanes = lg3.shape
    blk = pl.BlockSpec((ROUTE_ROWS, lanes), lambda i: (i, 0))
    return pl.pallas_call(
        _route_kernel,
        grid=(rows // ROUTE_ROWS,),
        in_specs=[pl.BlockSpec((N_EXPERTS, ROUTE_ROWS, lanes), lambda i: (0, i, 0)),
                  pl.BlockSpec(memory_space=pltpu.SMEM)],
        out_specs=[blk, blk, blk],
        out_shape=[jax.ShapeDtypeStruct((rows, lanes), jnp.int32),
                   jax.ShapeDtypeStruct((rows, lanes), F32), jax.ShapeDtypeStruct((rows, lanes), F32)],
        compiler_params=_cparams(("arbitrary",)),
        name="route",
    )(lg3, bias)


GATHER_UNROLL = 8
CAST_ROWS = 128


def _moe_kernel(ta_ref, tb_ref, nused_ref, valid_ref,
                h_hbm, tok_ref, tok_next_ref, wa_ref, wb_ref, upa_ref, upb_ref, dna_ref, dnb_ref,
                m_hbm, xbuf, ybuf, w_up, w_dn, gsem, ssem):
    i = pl.program_id(0)
    n_used = nused_ref[0]
    slot = i % 2
    rows = MOE_TILE * SUBLANES

    def row_copy_in(toks, s, r):
        tok = toks[0, 0, r]
        return pltpu.make_async_copy(
            h_hbm.at[pl.ds(pl.multiple_of(tok * SUBLANES, SUBLANES), SUBLANES), :],
            xbuf.at[s, pl.ds(pl.multiple_of(r * SUBLANES, SUBLANES), SUBLANES), :],
            gsem.at[s])

    def row_copy_out(toks, s, r):
        tok = toks[0, 0, r]
        return pltpu.make_async_copy(
            ybuf.at[s, pl.ds(pl.multiple_of(r * SUBLANES, SUBLANES), SUBLANES), :],
            m_hbm.at[pl.ds(pl.multiple_of(tok * SUBLANES, SUBLANES), SUBLANES), :],
            ssem.at[s])

    def start_rows(make, toks, s, count):
        def body8(c, carry):
            for u in range(GATHER_UNROLL):
                make(toks, s, c * GATHER_UNROLL + u).start()
            return carry

        def body1(r, carry):
            make(toks, s, r).start()
            return carry

        full = count // GATHER_UNROLL
        lax.fori_loop(0, full, body8, 0)
        lax.fori_loop(full * GATHER_UNROLL, count, body1, 0)

    def wait_gather(s):
        pltpu.make_async_copy(h_hbm.at[pl.ds(0, rows), :], xbuf.at[s], gsem.at[s]).wait()

    def wait_scatter(tile, s):
        n = pl.multiple_of(valid_ref[tile] * SUBLANES, SUBLANES)
        pltpu.make_async_copy(ybuf.at[s, pl.ds(0, n), :], m_hbm.at[pl.ds(0, n), :], ssem.at[s]).wait()

    @pl.when(i == 0)
    def _():
        start_rows(row_copy_in, tok_ref, 0, MOE_TILE)

    @pl.when(i + 1 < n_used)
    def _():
        start_rows(row_copy_in, tok_next_ref, 1 - slot, MOE_TILE)

    def refresh(which, ids_ref, up_ref, dn_ref):
        prev = ids_ref[jnp.maximum(i - 1, 0)]

        @pl.when((i == 0) | (ids_ref[i] != prev))
        def _():
            def cast_up(c, carry):
                r = pl.multiple_of(c * CAST_ROWS, CAST_ROWS)
                w_up[which, pl.ds(r, CAST_ROWS), :] = up_ref[0, pl.ds(r, CAST_ROWS), :].astype(BF16)
                return carry

            def cast_dn(c, carry):
                r = pl.multiple_of(c * CAST_ROWS, CAST_ROWS)
                w_dn[which, pl.ds(r, CAST_ROWS), :] = dn_ref[0, pl.ds(r, CAST_ROWS), :].astype(BF16)
                return carry

            lax.fori_loop(0, up_ref.shape[1] // CAST_ROWS, cast_up, 0)
            lax.fori_loop(0, dn_ref.shape[1] // CAST_ROWS, cast_dn, 0)

    @pl.when(i < n_used)
    def _():
        refresh(0, ta_ref, upa_ref, dna_ref)
        refresh(1, tb_ref, upb_ref, dnb_ref)
        wait_gather(slot)
        x = _load_token_major(xbuf.at[slot], MOE_TILE, upa_ref.shape[1]).astype(BF16)

        def ffn(which):
            gu = jnp.dot(x, w_up[which], preferred_element_type=F32)
            hid = jax.nn.silu(gu[:, :D_EXPERT]) * gu[:, D_EXPERT:]
            return jnp.dot(hid.astype(BF16), w_dn[which], preferred_element_type=F32)

        y = ffn(0) * wa_ref[...] + ffn(1) * wb_ref[...]

        @pl.when(i >= 2)
        def _():
            wait_scatter(i - 2, slot)

        _store_token_major(ybuf.at[slot], y)
        start_rows(row_copy_out, tok_ref, slot, valid_ref[i])

        @pl.when(i == n_used - 1)
        def _():
            @pl.when(i >= 1)
            def _():
                wait_scatter(i - 1, 1 - slot)
            wait_scatter(i, slot)


def _moe_call(layer, tile_a, tile_b, n_used, tile_valid, slot_tok, h1_tm, slot_wa, slot_wb, w_up, w_down,
              n_tiles, t):
    d = w_up.shape[1]
    e0 = layer * N_EXPERTS
    wspec = pl.BlockSpec((MOE_TILE, 1), lambda i, *_: (i, 0))
    tok_blk = (1, 1, MOE_TILE)
    grid_spec = pltpu.PrefetchScalarGridSpec(
        num_scalar_prefetch=4,
        grid=(n_tiles,),
        in_specs=[pl.BlockSpec(memory_space=pl.ANY),
                  pl.BlockSpec(tok_blk, lambda i, *_: (i, 0, 0), memory_space=pltpu.SMEM),
                  pl.BlockSpec(tok_blk, lambda i, *_: (jnp.minimum(i + 1, n_tiles - 1), 0, 0),
                               memory_space=pltpu.SMEM),
                  wspec, wspec,
                  pl.BlockSpec((1, d, 2 * D_EXPERT), lambda i, ta, tb, *_: (e0 + ta[i], 0, 0)),
                  pl.BlockSpec((1, d, 2 * D_EXPERT), lambda i, ta, tb, *_: (e0 + tb[i], 0, 0)),
                  pl.BlockSpec((1, D_EXPERT, d), lambda i, ta, tb, *_: (e0 + ta[i], 0, 0)),
                  pl.BlockSpec((1, D_EXPERT, d), lambda i, ta, tb, *_: (e0 + tb[i], 0, 0))],
        out_specs=pl.BlockSpec(memory_space=pl.ANY),
        scratch_shapes=[pltpu.VMEM((2, MOE_TILE * SUBLANES, LANES), F32),
                        pltpu.VMEM((2, MOE_TILE * SUBLANES, LANES), F32),
                        pltpu.VMEM((2, d, 2 * D_EXPERT), BF16), pltpu.VMEM((2, D_EXPERT, d), BF16),
                        pltpu.SemaphoreType.DMA((2,)), pltpu.SemaphoreType.DMA((2,))])
    return pl.pallas_call(
        _moe_kernel,
        grid_spec=grid_spec,
        out_shape=jax.ShapeDtypeStruct((t * SUBLANES, LANES), F32),
        compiler_params=_cparams(("arbitrary",)),
        name="moe",
    )(tile_a, tile_b, n_used, tile_valid, h1_tm, slot_tok, slot_tok, slot_wa, slot_wb, w_up, w_up, w_down, w_down)


def _final_kernel(alpha, h1_ref, m_ref, e_ref, g_ref, b_ref, o_ref):
    rows, d = o_ref.shape
    h1 = _load_token_major(h1_ref, rows, d)
    m = _load_token_major(m_ref, rows, d)
    o_ref[...] = _layer_norm(alpha * h1 + m + e_ref[...].astype(F32), g_ref[...], b_ref[...])


def _final_call(alpha, h1_tm, m_tm, e, g, b):
    t, d = e.shape
    tm = TM_FINAL
    row = lambda i: (i, 0)
    tmaj = pl.BlockSpec((tm * SUBLANES, LANES), row)
    return pl.pallas_call(
        functools.partial(_final_kernel, alpha),
        grid=(t // tm,),
        in_specs=[tmaj, tmaj, pl.BlockSpec((tm, d), row), _full(g.shape), _full(b.shape)],
        out_specs=pl.BlockSpec((tm, d), row),
        out_shape=jax.ShapeDtypeStruct((t, d), F32),
        compiler_params=_cparams(("arbitrary",)),
        name="final_ln",
    )(h1_tm, m_tm, e, g, b)


def _pack_w_in(w_in):
    off_q = OFF_CONV + 2 * D_CONV
    off_kr = off_q + Q_LORA + KV_LORA
    off_gate = off_kr + QK_ROPE
    kr = jnp.pad(w_in[..., off_kr:off_gate], ((0, 0), (0, 0), (ROPE_LANE0, HEAD_BLOCK - ROPE_LANE0 - QK_ROPE)))
    return jnp.concatenate([w_in[..., :off_kr], kr, w_in[..., off_gate:]], axis=-1).astype(BF16)


def _pack_heads(w, lo, width, stride, lane0):
    blocks = []
    for h in range(N_HEADS):
        l0 = lane0(h)
        blocks.append(jnp.pad(w[..., h * stride + lo:h * stride + lo + width],
                              ((0, 0), (0, 0), (l0, HEAD_BLOCK - l0 - width))))
    return jnp.concatenate(blocks, axis=-1).astype(BF16)


def _rope_tables(positions):
    inv_freq = jnp.power(ROPE_THETA, -jnp.arange(0, QK_ROPE, 2, dtype=F32) / QK_ROPE)
    ang = positions.astype(F32).reshape(-1, 1) * inv_freq
    cos, sin = jnp.cos(ang), jnp.sin(ang)
    t = ang.shape[0]
    ones_lo = jnp.ones((t, ROPE_LANE0), F32)
    ones_hi = jnp.ones((t, HEAD_BLOCK - ROPE_LANE0 - QK_ROPE), F32)
    zeros_lo = jnp.zeros((t, ROPE_LANE0), F32)
    zeros_half = jnp.zeros((t, HALF_ROPE), F32)
    zeros_hi = jnp.zeros((t, HEAD_BLOCK - ROPE_LANE0 - QK_ROPE), F32)
    c = jnp.concatenate([ones_lo, cos, cos, ones_hi], axis=-1)
    s1 = jnp.concatenate([zeros_lo, -sin, zeros_half, zeros_hi], axis=-1)
    s2 = jnp.concatenate([zeros_lo, zeros_half, sin, zeros_hi], axis=-1)
    return c, s1, s2


def _routing_plan(cls, wa, wb, n_tiles):
    t = cls.shape[0]
    n_pad = n_tiles * MOE_TILE - t
    assert n_pad == N_CLASSES * MOE_TILE, n_pad
    classes = jnp.arange(N_CLASSES, dtype=jnp.int32)
    counts = jnp.sum((cls[None, :] == classes[:, None]).astype(jnp.int32), axis=1)
    padded = (counts + MOE_TILE - 1) // MOE_TILE * MOE_TILE
    pad_end = jnp.cumsum(padded)
    pad_start = pad_end - padded
    pad_need = padded - counts
    j = jnp.arange(MOE_TILE, dtype=jnp.int32)[None, :]
    pad_key = jnp.where(j < pad_need[:, None], 2 * classes[:, None] + 1, 2 * N_CLASSES).reshape(-1)
    zeros_i = jnp.zeros((n_pad,), jnp.int32)
    zeros_f = jnp.zeros((n_pad,), F32)
    _, slot_tok, slot_wa, slot_wb = lax.sort(
        (jnp.concatenate([2 * cls, pad_key]), jnp.concatenate([jnp.arange(t, dtype=jnp.int32), zeros_i]),
         jnp.concatenate([wa, zeros_f]), jnp.concatenate([wb, zeros_f])), num_keys=1)
    slot_tok = slot_tok.reshape(n_tiles, 1, MOE_TILE)
    slot_wa = slot_wa.reshape(-1, 1)
    slot_wb = slot_wb.reshape(-1, 1)
    tile_row0 = jnp.arange(n_tiles, dtype=jnp.int32) * MOE_TILE
    tile_cls = jnp.minimum(jnp.sum((tile_row0[:, None] >= pad_end[None, :]).astype(jnp.int32), axis=1),
                           N_CLASSES - 1)
    onehot = (tile_cls[:, None] == classes[None, :]).astype(jnp.int32)
    tile_valid = jnp.clip(jnp.sum(onehot * (pad_start + counts)[None, :], axis=1) - tile_row0,
                          0, MOE_TILE).astype(jnp.int32)
    pair_lo = jnp.array([p[0] for p in PAIRS], jnp.int32)
    pair_hi = jnp.array([p[1] for p in PAIRS], jnp.int32)
    grp = tile_cls // len(PAIRS)
    pair_onehot = ((tile_cls % len(PAIRS))[:, None] == jnp.arange(len(PAIRS), dtype=jnp.int32)[None, :])
    tile_a = grp * EXPERTS_PER_GROUP + jnp.sum(pair_onehot * pair_lo[None, :], axis=1)
    tile_b = grp * EXPERTS_PER_GROUP + jnp.sum(pair_onehot * pair_hi[None, :], axis=1)
    n_used = (pad_end[-1] // MOE_TILE).astype(jnp.int32).reshape(1)
    return tile_a, tile_b, n_used, tile_valid, slot_tok, slot_wa, slot_wb


def kernel(x, p, positions, ln_in_g, ln_in_b, w_in, b_gate, pool_w, pool_scale, pool_proj, conv_dw, conv_b, conv_ln_g, conv_ln_b, conv_proj, q_norm_g, w_uq, kv_norm_g, w_ukv, mla_proj, w_out, ln1_g, ln1_b, w_router, router_bias, exp_w_up, exp_w_down, ple_proj, ple_gate, ln2_g, ln2_b):
    batch, seq, d = x.shape
    depth = w_in.shape[0]
    t = batch * seq
    alpha = float((2 * depth) ** 0.25)
    n_tiles = (t + N_CLASSES * (MOE_TILE - 1)) // MOE_TILE + 1

    w_in_p = _pack_w_in(w_in)
    pool_w_b = pool_w.astype(BF16)
    pool_proj_b, conv_proj_b, mla_proj_b = pool_proj.astype(BF16), conv_proj.astype(BF16), mla_proj.astype(BF16)
    qk = QK_NOPE + QK_ROPE
    wq_p = (_pack_heads(w_uq, 0, QK_NOPE, qk, lambda h: 0)
            + _pack_heads(w_uq, QK_NOPE, QK_ROPE, qk, lambda h: ROPE_LANE0))
    wqt_p = jnp.swapaxes(wq_p, 1, 2)
    wk_p = _pack_heads(w_ukv, 0, QK_NOPE, QK_NOPE + V_HEAD, lambda h: 0)
    kv_w = QK_NOPE + V_HEAD
    wvt_p = jnp.swapaxes(jnp.concatenate(
        [jnp.pad(w_ukv[..., h * kv_w + QK_NOPE:(h + 1) * kv_w], ((0, 0), (0, 0), (0, BF16_ROWS)))
         for h in range(N_HEADS)], axis=-1), 1, 2).astype(BF16)
    ones_rows = jnp.tile(jnp.concatenate([jnp.zeros((V_HEAD, 1), F32), jnp.ones((BF16_ROWS, 1), F32)]),
                         (N_HEADS, 1))
    w_out_b, ple_proj_b, ple_gate_b = w_out.astype(BF16), ple_proj.astype(BF16), ple_gate.astype(BF16)
    w_up_b = exp_w_up.reshape((depth * N_EXPERTS,) + exp_w_up.shape[2:])
    w_down_b = exp_w_down.reshape((depth * N_EXPERTS,) + exp_w_down.shape[2:])
    wr_t = w_router.T
    wr_hi = wr_t.astype(BF16)
    wr_lo = (wr_t - wr_hi.astype(F32)).astype(BF16)
    rope_tabs = _rope_tables(positions)
    rope_tabs_t = tuple(a.T for a in rope_tabs)
    p2 = p.reshape(depth, t, -1)

    pre, pre_g, pre_b = (x.reshape(t, d),), ln_in_g, ln_in_b
    for i in range(depth):
        h, u_pool, u_conv, cqkv, glog = _in_proj_call(alpha, pre, pre_g.reshape(1, d), pre_b.reshape(1, d),
                                                      w_in_p, i)
        a_pool = _pool_call(u_pool, pool_w_b[i], pool_scale[i].reshape(1, -1), batch, seq)
        a_conv = _conv_call(u_conv, conv_dw[i], conv_b[i].reshape(1, -1), conv_ln_g[i].reshape(1, -1),
                            conv_ln_b[i].reshape(1, -1), batch, seq)
        qt, k, vt = _mla_prep_call(cqkv, rope_tabs, rope_tabs_t, q_norm_g[i].reshape(1, -1),
                                   kv_norm_g[i].reshape(1, -1), wqt_p[i], wk_p[i], wvt_p[i], ones_rows)
        a_attn = _attn_call(qt, k, vt, batch, seq)
        h1_tm, e, logits = _merge_call(alpha, i, a_pool, a_conv, a_attn, glog, h, p2,
                                       pool_proj_b, conv_proj_b, mla_proj_b, b_gate[i], w_out_b,
                                       ln1_g[i].reshape(1, d), ln1_b[i].reshape(1, d),
                                       ple_proj_b, ple_gate_b, wr_hi, wr_lo)
        cls, wa, wb = _route_call(logits.reshape(N_EXPERTS, t // LANES, LANES), router_bias)
        plan = _routing_plan(cls.reshape(t), wa.reshape(t), wb.reshape(t), n_tiles)
        m_tm = _moe_call(i, *plan[:5], h1_tm, plan[5], plan[6], w_up_b, w_down_b, n_tiles, t)
        pre, pre_g, pre_b = (h1_tm, m_tm, e), ln2_g[i], ln2_b[i]
    h = _final_call(alpha, *pre, pre_g.reshape(1, d), pre_b.reshape(1, d))
    return h.reshape(batch, seq, d)
```

```python
import functools

import jax
import jax.numpy as jnp
from jax import lax
from jax.experimental import pallas as pl
from jax.experimental.pallas import tpu as pltpu

F32 = jnp.float32
BF16 = jnp.bfloat16

CHUNK = 64
POOL_WINDOWS = (2, 4, 8, 16)
POOL_GW = 128
D_POOL = 512
D_CONV = 512
CONV_WIDTH = 31
N_HEADS = 8
QK_NOPE = 64
QK_ROPE = 32
V_HEAD = 64
Q_LORA = 384
KV_LORA = 256
ROPE_THETA = 10000.0
N_EXPERTS = 16
N_GROUPS = 4
EXPERTS_PER_GROUP = 4
D_EXPERT = 512
LN_EPS = 1e-5
RMS_EPS = 1e-6

LANES = 128
SUBLANES = 8
HEAD_BLOCK = LANES
ROPE_LANE0 = QK_NOPE
HALF_ROPE = QK_ROPE // 2
BF16_ROWS = 16
VT_BLOCK = V_HEAD + BF16_ROWS
LOG2_E = 1.4426950408889634

TM_PROJ = 512
TM_PREP = 512
TM_MERGE = 512
MERGE_SUB = 256
TM_FINAL = 512
TQ = 256
TK = 256
Q_SUBS = 2
ATTN_LOOKAHEAD = 3
POOL_CHUNK = 256
POOL_HALO = 16
CONV_CHUNK = 128
CONV_HALO = 32
MOE_TILE = 256
ROUTE_ROWS = 8

PAIRS = ((0, 1), (0, 2), (0, 3), (1, 2), (1, 3), (2, 3))
N_CLASSES = N_GROUPS * len(PAIRS)

VMEM_LIMIT = 56 * 1024 * 1024


def _cparams(sem):
    return pltpu.CompilerParams(dimension_semantics=sem, vmem_limit_bytes=VMEM_LIMIT)


def _layer_norm(x, g, b):
    mu = jnp.mean(x, axis=-1, keepdims=True)
    xc = x - mu
    var = jnp.mean(xc * xc, axis=-1, keepdims=True)
    return xc * lax.rsqrt(var + LN_EPS) * g + b


def _rms_norm(x, g):
    ms = jnp.mean(x * x, axis=-1, keepdims=True)
    return x * lax.rsqrt(ms + RMS_EPS) * g


def _full(shape):
    n = len(shape)
    return pl.BlockSpec(shape, lambda *_: (0,) * n)


def _layer_full(stacked, layer):
    n = stacked.ndim - 1
    return pl.BlockSpec((None,) + stacked.shape[1:], lambda *_: (layer,) + (0,) * n)


OFF_CONV = D_POOL
OFF_CQKV = OFF_CONV + 2 * D_CONV
W_CQKV = Q_LORA + KV_LORA + HEAD_BLOCK
OFF_GATE = OFF_CQKV + W_CQKV
N_COLS_CHUNK = 512


def _in_proj_kernel(alpha, residual, tiles_per_seq, *refs):
    if residual:
        h1_ref, m_ref, e_ref, g_ref, b_ref, w_ref, dw_ref, cb_ref, cg_ref, cbb_ref = refs[:10]
    else:
        x_ref, g_ref, b_ref, w_ref, dw_ref, cb_ref, cg_ref, cbb_ref = refs[:8]
    h_ref, pool_ref, aconv_ref, cqkv_ref, gate_ref, zs = refs[-6:]
    rows, d = h_ref.shape

    @pl.when(pl.program_id(0) % tiles_per_seq == 0)
    def _():
        zs[0:CONV_HALO, :] = jnp.zeros((CONV_HALO, D_CONV), F32)

    if residual:
        pre = alpha * _load_token_major(h1_ref, rows, d) + _load_token_major(m_ref, rows, d)
        pre = pre + e_ref[...].astype(F32)
    else:
        pre = x_ref[...]
    h = _layer_norm(pre, g_ref[...], b_ref[...])
    h_ref[...] = h
    x = h.astype(BF16)

    def mm(lo, hi):
        return jnp.dot(x, w_ref[:, lo:hi], preferred_element_type=F32)

    zs[CONV_HALO:CONV_HALO + rows, :] = jax.nn.sigmoid(mm(OFF_CONV + D_CONV, OFF_CQKV))
    zs[CONV_HALO:CONV_HALO + rows, :] = mm(OFF_CONV, OFF_CONV + D_CONV) * zs[CONV_HALO:CONV_HALO + rows, :]
    pool_ref[...] = mm(0, OFF_CONV).astype(BF16)
    cqkv_ref[...] = mm(OFF_CQKV, OFF_GATE).astype(BF16)
    n_gate_chunks = gate_ref.shape[1] // N_COLS_CHUNK
    conv_chunks = list(range(0, rows, CONV_CHUNK))
    for c in range(n_gate_chunks):
        gate_ref[:, c * N_COLS_CHUNK:(c + 1) * N_COLS_CHUNK] = mm(
            OFF_GATE + c * N_COLS_CHUNK, OFF_GATE + (c + 1) * N_COLS_CHUNK).astype(BF16)
        lo = len(conv_chunks) * c // n_gate_chunks
        hi = len(conv_chunks) * (c + 1) // n_gate_chunks
        for r0 in conv_chunks[lo:hi]:
            aconv_ref[r0:r0 + CONV_CHUNK, :] = _conv_chunk(zs, r0, dw_ref, cb_ref, cg_ref, cbb_ref).astype(BF16)
    zs[0:CONV_HALO, :] = zs[rows:rows + CONV_HALO, :]


def _in_proj_call(alpha, pre, g, b, w, conv_params, layer, seq):
    residual = len(pre) == 3
    t, d = pre[-1].shape
    n = w.shape[-1]
    n_gate = n - OFF_GATE
    row = lambda i: (i, 0)
    tmaj = pl.BlockSpec((TM_PROJ * SUBLANES, LANES), row)
    pre_specs = [tmaj, tmaj, pl.BlockSpec((TM_PROJ, d), row)] if residual else [pl.BlockSpec((TM_PROJ, d), row)]
    return pl.pallas_call(
        functools.partial(_in_proj_kernel, alpha, residual, seq // TM_PROJ),
        grid=(t // TM_PROJ,),
        in_specs=pre_specs + [_full(g.shape), _full(b.shape), _layer_full(w, layer)]
        + [_full(a.shape) for a in conv_params],
        out_specs=[pl.BlockSpec((TM_PROJ, d), row),
                   pl.BlockSpec((TM_PROJ, D_POOL), row), pl.BlockSpec((TM_PROJ, D_CONV), row),
                   pl.BlockSpec((TM_PROJ, W_CQKV), row), pl.BlockSpec((TM_PROJ, n_gate), row)],
        out_shape=[jax.ShapeDtypeStruct((t, d), F32),
                   jax.ShapeDtypeStruct((t, D_POOL), BF16), jax.ShapeDtypeStruct((t, D_CONV), BF16),
                   jax.ShapeDtypeStruct((t, W_CQKV), BF16), jax.ShapeDtypeStruct((t, n_gate), BF16)],
        scratch_shapes=[pltpu.VMEM((TM_PROJ + CONV_HALO, D_CONV), F32)],
        compiler_params=_cparams(("arbitrary",)),
        name="in_proj",
    )(*pre, g, b, w, *conv_params)


def _pool_kernel(u_ref, w_ref, scale_ref, o_ref, buf):
    s = o_ref.shape[0]
    buf[0:POOL_HALO, :] = jnp.zeros((POOL_HALO, D_POOL), F32)

    def to_f32(c, carry):
        r = pl.multiple_of(c * POOL_CHUNK, POOL_CHUNK)
        buf[pl.ds(POOL_HALO + r, POOL_CHUNK), :] = u_ref[pl.ds(r, POOL_CHUNK), :].astype(F32)
        return carry

    lax.fori_loop(0, s // POOL_CHUNK, to_f32, 0)

    def body(c, carry):
        r0 = pl.multiple_of(c * POOL_CHUNK, POOL_CHUNK)
        t = r0 + lax.broadcasted_iota(jnp.int32, (POOL_CHUNK, 1), 0)
        for g, w in enumerate(POOL_WINDOWS):
            cols = slice(g * POOL_GW, (g + 1) * POOL_GW)
            xw = buf[pl.ds(r0, POOL_CHUNK + POOL_HALO), cols]
            acc = xw
            k = 1
            while k < w:
                acc = acc + pltpu.roll(acc, k, axis=0)
                k *= 2
            cnt = jnp.minimum(t + 1, w).astype(F32)
            mixed = acc[POOL_HALO:] / cnt - xw[POOL_HALO:]
            y = jnp.dot(mixed.astype(BF16), w_ref[g], preferred_element_type=F32) * scale_ref[:, cols]
            o_ref[pl.ds(r0, POOL_CHUNK), cols] = y.astype(BF16)
        return carry

    lax.fori_loop(0, s // POOL_CHUNK, body, 0)


def _pool_call(u, w, scale, batch, seq):
    return pl.pallas_call(
        _pool_kernel,
        grid=(batch,),
        in_specs=[pl.BlockSpec((seq, D_POOL), lambda b: (b, 0)),
                  _full(w.shape), _full(scale.shape)],
        out_specs=pl.BlockSpec((seq, D_POOL), lambda b: (b, 0)),
        out_shape=jax.ShapeDtypeStruct(u.shape, BF16),
        scratch_shapes=[pltpu.VMEM((seq + POOL_HALO, D_POOL), F32)],
        compiler_params=_cparams(("arbitrary",)),
        name="pool",
    )(u, w, scale)


def _conv_chunk(zs, r0, dw_ref, cb_ref, g_ref, b_ref):
    parts = []
    for cg in range(D_CONV // LANES):
        cols = slice(cg * LANES, (cg + 1) * LANES)
        win = zs[r0:r0 + CONV_CHUNK + CONV_HALO, cols]
        acc = jnp.zeros((CONV_CHUNK, LANES), F32) + cb_ref[:, cols]
        for sub in range(SUBLANES):
            shifted = win if sub == 0 else pltpu.roll(win, sub, axis=0)
            for a in range(CONV_HALO // SUBLANES):
                lag = SUBLANES * a + sub
                if lag >= CONV_WIDTH:
                    continue
                k = CONV_WIDTH - 1 - lag
                lo = CONV_HALO - SUBLANES * a
                acc = acc + dw_ref[k:k + 1, cols] * shifted[lo:lo + CONV_CHUNK]
        parts.append(acc)
    y = _layer_norm(jnp.concatenate(parts, axis=-1), g_ref[...], b_ref[...])
    return y * jax.nn.sigmoid(y)


def _rope_block(x, c, s1, s2):
    return x * c + pltpu.roll(x, LANES - HALF_ROPE, axis=1) * s1 + pltpu.roll(x, HALF_ROPE, axis=1) * s2


def _rope_block_t(x, c, s1, s2):
    return x * c + pltpu.roll(x, HEAD_BLOCK - HALF_ROPE, axis=0) * s1 + pltpu.roll(x, HALF_ROPE, axis=0) * s2


def _mla_prep_kernel(cqkv_ref, c_ref, s1_ref, s2_ref, ct_ref, s1t_ref, s2t_ref, qg_ref, kvg_ref,
                     wqt_ref, wk_ref, wvt_ref, ones_ref, qt_ref, k_ref, vt_ref):
    cq = _rms_norm(cqkv_ref[:, 0:Q_LORA].astype(F32), qg_ref[...]).astype(BF16)
    ckv = _rms_norm(cqkv_ref[:, Q_LORA:Q_LORA + KV_LORA].astype(F32), kvg_ref[...]).astype(BF16)
    kr = _rope_block(cqkv_ref[:, Q_LORA + KV_LORA:W_CQKV].astype(F32), c_ref[...], s1_ref[...], s2_ref[...])
    scale = float((QK_NOPE + QK_ROPE) ** -0.5 * LOG2_E)
    nt = (((1,), (1,)), ((), ()))
    ct, s1t, s2t = ct_ref[...], s1t_ref[...], s2t_ref[...]
    for h in range(N_HEADS):
        cols = slice(h * HEAD_BLOCK, (h + 1) * HEAD_BLOCK)
        qt = lax.dot_general(wqt_ref[cols, :], cq, nt, preferred_element_type=F32) * scale
        qt_ref[cols, :] = _rope_block_t(qt, ct, s1t, s2t).astype(BF16)
        k = jnp.dot(ckv, wk_ref[:, cols], preferred_element_type=F32) + kr
        k_ref[:, cols] = k.astype(BF16)
    vt = lax.dot_general(wvt_ref[...], ckv, nt, preferred_element_type=F32) + ones_ref[...]
    vt_ref[...] = vt.astype(BF16)


def _mla_prep_call(cqkv, tabs, tabs_t, qg, kvg, wqt, wk, wvt, ones_rows):
    t = cqkv.shape[0]
    n = N_HEADS * HEAD_BLOCK
    nv = N_HEADS * VT_BLOCK
    row = lambda i: (i, 0)
    col = lambda i: (0, i)
    tab = pl.BlockSpec((TM_PREP, LANES), row)
    tab_t = pl.BlockSpec((LANES, TM_PREP), col)
    return pl.pallas_call(
        _mla_prep_kernel,
        grid=(t // TM_PREP,),
        in_specs=[pl.BlockSpec((TM_PREP, W_CQKV), row), tab, tab, tab, tab_t, tab_t, tab_t,
                  _full(qg.shape), _full(kvg.shape), _full(wqt.shape), _full(wk.shape), _full(wvt.shape),
                  _full(ones_rows.shape)],
        out_specs=[pl.BlockSpec((n, TM_PREP), col), pl.BlockSpec((TM_PREP, n), row),
                   pl.BlockSpec((nv, TM_PREP), col)],
        out_shape=[jax.ShapeDtypeStruct((n, t), BF16), jax.ShapeDtypeStruct((t, n), BF16),
                   jax.ShapeDtypeStruct((nv, t), BF16)],
        compiler_params=_cparams(("arbitrary",)),
        name="mla_prep",
    )(cqkv, *tabs, *tabs_t, qg, kvg, wqt, wk, wvt, ones_rows)


def _attn_kernel(qt_ref, k_ref, vt_ref, o_ref, m_ref, acc_ref):
    j = pl.program_id(1)
    key_chunk = lax.broadcasted_iota(jnp.int32, (TK, TQ), 0) // CHUNK
    qry_chunk = lax.broadcasted_iota(jnp.int32, (TK, TQ), 1) // CHUNK
    diag_mask = key_chunk <= qry_chunk

    def scores(h, sub, kt):
        k0 = pl.multiple_of(kt * TK, TK)
        kk = k_ref[pl.ds(k0, TK), h * HEAD_BLOCK:(h + 1) * HEAD_BLOCK]
        return jnp.dot(kk, qt_ref[h * HEAD_BLOCK:(h + 1) * HEAD_BLOCK, sub * TQ:(sub + 1) * TQ],
                       preferred_element_type=F32)

    def values_t(h, kt):
        k0 = pl.multiple_of(kt * TK, TK)
        return vt_ref[h * VT_BLOCK:(h + 1) * VT_BLOCK, pl.ds(k0, TK)]

    def first_update(h, sub, kt, masked, s):
        st = h * Q_SUBS + sub
        if masked:
            s = jnp.where(diag_mask, s, -jnp.inf)
        m = jnp.max(s, axis=0, keepdims=True)
        p = jnp.exp2(s - m)
        m_ref[st] = jnp.broadcast_to(m, (SUBLANES, TQ))
        acc_ref[st] = jnp.dot(values_t(h, kt), p.astype(BF16), preferred_element_type=F32)

    def update(h, sub, kt, masked, s):
        st = h * Q_SUBS + sub
        if masked:
            s = jnp.where(diag_mask, s, -jnp.inf)
        m_old = m_ref[st]
        m_new = jnp.maximum(m_old, jnp.max(s, axis=0, keepdims=True))
        alpha = jnp.exp2(m_old - m_new)
        p = jnp.exp2(s - m_new[0:1, :])
        m_ref[st] = m_new
        acc_ref[st] = alpha[0:1, :] * acc_ref[st] + jnp.dot(values_t(h, kt), p.astype(BF16),
                                                           preferred_element_type=F32)

    def run(work):
        pending = {}
        for idx in range(len(work) + ATTN_LOOKAHEAD):
            if idx < len(work):
                h, sub, kt, _, _ = work[idx]
                pending[idx] = scores(h, sub, kt)
            if idx >= ATTN_LOOKAHEAD:
                h, sub, kt, masked, upd = work[idx - ATTN_LOOKAHEAD]
                upd(h, sub, kt, masked, pending.pop(idx - ATTN_LOOKAHEAD))

    tail = []
    for h in range(N_HEADS):
        for sub in range(Q_SUBS):
            for kk in range(sub + 1):
                tail.append((h, sub, Q_SUBS * j + kk, kk == sub, first_update if kk == 0 else update))
    run(tail)

    def step(kt, carry):
        run([(h, sub, kt, False, update) for h in range(N_HEADS) for sub in range(Q_SUBS)])
        return carry

    lax.fori_loop(0, Q_SUBS * j, step, 0)

    for sub in range(Q_SUBS):
        outs = []
        for h in range(N_HEADS):
            st = h * Q_SUBS + sub
            outs.append(acc_ref[st, 0:V_HEAD, :] * (1.0 / acc_ref[st, V_HEAD:V_HEAD + 1, :]))
        o_ref[sub * TQ:(sub + 1) * TQ, :] = jnp.concatenate(outs, axis=0).T.astype(BF16)


def _attn_call(qt, k, vt, batch, seq):
    n = N_HEADS * HEAD_BLOCK
    tq = Q_SUBS * TQ
    nq = seq // tq
    return pl.pallas_call(
        _attn_kernel,
        grid=(batch, nq),
        in_specs=[pl.BlockSpec((n, tq), lambda b, i: (0, b * nq + i)),
                  pl.BlockSpec((seq, n), lambda b, i: (b, 0)),
                  pl.BlockSpec((N_HEADS * VT_BLOCK, seq), lambda b, i: (0, b))],
        out_specs=pl.BlockSpec((tq, N_HEADS * V_HEAD), lambda b, i: (b * nq + i, 0)),
        out_shape=jax.ShapeDtypeStruct((k.shape[0], N_HEADS * V_HEAD), BF16),
        scratch_shapes=[pltpu.VMEM((N_HEADS * Q_SUBS, SUBLANES, TQ), F32),
                        pltpu.VMEM((N_HEADS * Q_SUBS, VT_BLOCK, TQ), F32)],
        compiler_params=_cparams(("arbitrary", "arbitrary")),
        name="attn",
    )(qt, k, vt)


def _store_token_major(ref, x):
    rows = x.shape[0]
    for j in range(x.shape[1] // LANES):
        ref[pl.ds(j, rows, stride=SUBLANES), :] = x[:, j * LANES:(j + 1) * LANES]


def _load_token_major(ref, rows, d):
    return jnp.concatenate([ref[pl.ds(j, rows, stride=SUBLANES), :] for j in range(d // LANES)], axis=-1)


def _merge_kernel(alpha, ap_ref, ac_ref, at_ref, gl_ref, h_ref, p_ref, pp_ref, cp_ref, mp_ref, bg_ref,
                  wo_ref, g_ref, b_ref, pproj_ref, pgate_ref, wrh_ref, wrl_ref,
                  h1_ref, e_ref, lg_ref):
    d = h_ref.shape[1]
    sub = MERGE_SUB
    parts = [pl.ds(k * sub, sub) for k in range(h_ref.shape[0] // sub)]
    nt = (((1,), (1,)), ((), ()))

    def merged_of(rows):
        merged = None
        for br, (a_ref, w_ref) in enumerate(((ap_ref, pp_ref), (ac_ref, cp_ref), (at_ref, mp_ref))):
            y = jnp.dot(a_ref[rows, :], w_ref[...], preferred_element_type=F32)
            gate = jax.nn.sigmoid(gl_ref[rows, br * d:(br + 1) * d].astype(F32) + bg_ref[br:br + 1, :])
            merged = gate * y if merged is None else merged + gate * y
        return merged.astype(BF16)

    merged = [merged_of(rows) for rows in parts]
    ys = [jnp.dot(m, wo_ref[...], preferred_element_type=F32) for m in merged]
    es = [jnp.dot(p_ref[rows, :].astype(BF16), pproj_ref[...], preferred_element_type=F32) for rows in parts]
    h1s = [_layer_norm(alpha * h_ref[rows, :] + y, g_ref[...], b_ref[...]) for rows, y in zip(parts, ys)]
    for k, (rows, h1, e) in enumerate(zip(parts, h1s, es)):
        h1b = h1.astype(BF16)
        e = e * jax.nn.sigmoid(jnp.dot(h1b, pgate_ref[...], preferred_element_type=F32))
        e_ref[rows, :] = e.astype(BF16)
        h1l = (h1 - h1b.astype(F32)).astype(BF16)
        lg = lax.dot_general(wrh_ref[...], h1b, nt, preferred_element_type=F32)
        lg = lg + lax.dot_general(wrh_ref[...], h1l, nt, preferred_element_type=F32)
        lg = lg + lax.dot_general(wrl_ref[...], h1b, nt, preferred_element_type=F32)
        lg_ref[:, k * sub:(k + 1) * sub] = lg
        _store_token_major(h1_ref.at[pl.ds(k * sub * SUBLANES, sub * SUBLANES), :], h1)


def _merge_call(alpha, layer, ap, ac, at, gl, h, p, pp, cp, mp, bg, wo, g, b, pproj, pgate, wrh, wrl):
    t, d = h.shape
    tm = TM_MERGE
    row = lambda i: (i, 0)
    ins = [ap, ac, at, gl, h]
    in_specs = [pl.BlockSpec((tm, a.shape[1]), row) for a in ins]
    in_specs.append(pl.BlockSpec((None, tm, p.shape[-1]), lambda i: (layer, i, 0)))
    stacked = {id(a) for a in (pp, cp, mp, wo, pproj, pgate)}
    consts = [pp, cp, mp, bg, wo, g, b, pproj, pgate, wrh, wrl]
    in_specs += [_layer_full(a, layer) if id(a) in stacked else _full(a.shape) for a in consts]
    return pl.pallas_call(
        functools.partial(_merge_kernel, alpha),
        grid=(t // tm,),
        in_specs=in_specs,
        out_specs=[pl.BlockSpec((tm * SUBLANES, LANES), row), pl.BlockSpec((tm, d), row),
                   pl.BlockSpec((N_EXPERTS, tm), lambda i: (0, i))],
        out_shape=[jax.ShapeDtypeStruct((t * SUBLANES, LANES), F32), jax.ShapeDtypeStruct((t, d), BF16),
                   jax.ShapeDtypeStruct((N_EXPERTS, t), F32)],
        compiler_params=_cparams(("arbitrary",)),
        name="merge",
    )(*ins, p, *consts)


def _route_kernel(lg_ref, bias_ref, cls_ref, wa_ref, wb_ref):
    aff = [jax.nn.sigmoid(lg_ref[e]) for e in range(N_EXPERTS)]
    sel = [aff[e] + bias_ref[e] for e in range(N_EXPERTS)]
    n = EXPERTS_PER_GROUP

    def top2_sum(vals):
        best = None
        for a, b in PAIRS:
            s = vals[a] + vals[b]
            best = s if best is None else jnp.maximum(best, s)
        return best

    grp = jnp.zeros(aff[0].shape, jnp.int32)
    best = top2_sum(sel[0:n])
    for g in range(1, N_GROUPS):
        sc = top2_sum(sel[g * n:(g + 1) * n])
        better = sc > best
        grp = jnp.where(better, g, grp)
        best = jnp.where(better, sc, best)
    vs, afs = [], []
    for j in range(n):
        v, a = sel[j], aff[j]
        for g in range(1, N_GROUPS):
            v = jnp.where(grp == g, sel[g * n + j], v)
            a = jnp.where(grp == g, aff[g * n + j], a)
        vs.append(v)
        afs.append(a)
    first = jnp.zeros_like(grp)
    fv = vs[0]
    for j in range(1, n):
        better = vs[j] > fv
        first = jnp.where(better, j, first)
        fv = jnp.where(better, vs[j], fv)
    second = jnp.full_like(grp, -1)
    sv = jnp.full_like(fv, -jnp.inf)
    for j in range(n):
        better = (first != j) & ((second < 0) | (vs[j] > sv))
        second = jnp.where(better, j, second)
        sv = jnp.where(better, vs[j], sv)
    lo = jnp.minimum(first, second)
    hi = jnp.maximum(first, second)
    a_lo, a_hi = afs[0], afs[0]
    for j in range(1, n):
        a_lo = jnp.where(lo == j, afs[j], a_lo)
        a_hi = jnp.where(hi == j, afs[j], a_hi)
    pair = jnp.zeros_like(grp)
    for idx, (a, b) in enumerate(PAIRS):
        pair = jnp.where((lo == a) & (hi == b), idx, pair)
    tot = a_lo + a_hi
    cls_ref[...] = grp * len(PAIRS) + pair
    wa_ref[...] = a_lo / tot
    wb_ref[...] = a_hi / tot


def _route_call(lg3, bias):
    _, rows, lanes = lg3.shape
    blk = pl.BlockSpec((ROUTE_ROWS, lanes), lambda i: (i, 0))
    return pl.pallas_call(
        _route_kernel,
        grid=(rows // ROUTE_ROWS,),
        in_specs=[pl.BlockSpec((N_EXPERTS, ROUTE_ROWS, lanes), lambda i: (0, i, 0)),
                  pl.BlockSpec(memory_space=pltpu.SMEM)],
        out_specs=[blk, blk, blk],
        out_shape=[jax.ShapeDtypeStruct((rows, lanes), jnp.int32),
                   jax.ShapeDtypeStruct((rows, lanes), F32), jax.ShapeDtypeStruct((rows, lanes), F32)],
        compiler_params=_cparams(("arbitrary",)),
        name="route",
    )(lg3, bias)


GATHER_UNROLL = 8
CAST_ROWS = 128


def _moe_kernel(ta_ref, tb_ref, nused_ref, valid_ref,
                h_hbm, tok_ref, tok_next_ref, wa_ref, wb_ref, upa_ref, upb_ref, dna_ref, dnb_ref,
                m_hbm, xbuf, ybuf, w_up, w_dn, gsem, ssem):
    i = pl.program_id(0)
    n_used = nused_ref[0]
    slot = i % 2
    rows = MOE_TILE * SUBLANES

    def row_copy_in(toks, s, r):
        tok = toks[0, 0, r]
        return pltpu.make_async_copy(
            h_hbm.at[pl.ds(pl.multiple_of(tok * SUBLANES, SUBLANES), SUBLANES), :],
            xbuf.at[s, pl.ds(pl.multiple_of(r * SUBLANES, SUBLANES), SUBLANES), :],
            gsem.at[s])

    def row_copy_out(toks, s, r):
        tok = toks[0, 0, r]
        return pltpu.make_async_copy(
            ybuf.at[s, pl.ds(pl.multiple_of(r * SUBLANES, SUBLANES), SUBLANES), :],
            m_hbm.at[pl.ds(pl.multiple_of(tok * SUBLANES, SUBLANES), SUBLANES), :],
            ssem.at[s])

    def start_rows(make, toks, s, count):
        def body8(c, carry):
            for u in range(GATHER_UNROLL):
                make(toks, s, c * GATHER_UNROLL + u).start()
            return carry

        def body1(r, carry):
            make(toks, s, r).start()
            return carry

        full = count // GATHER_UNROLL
        lax.fori_loop(0, full, body8, 0)
        lax.fori_loop(full * GATHER_UNROLL, count, body1, 0)

    def wait_gather(s):
        pltpu.make_async_copy(h_hbm.at[pl.ds(0, rows), :], xbuf.at[s], gsem.at[s]).wait()

    def wait_scatter(tile, s):
        n = pl.multiple_of(valid_ref[tile] * SUBLANES, SUBLANES)
        pltpu.make_async_copy(ybuf.at[s, pl.ds(0, n), :], m_hbm.at[pl.ds(0, n), :], ssem.at[s]).wait()

    @pl.when(i == 0)
    def _():
        start_rows(row_copy_in, tok_ref, 0, MOE_TILE)

    @pl.when(i + 1 < n_used)
    def _():
        start_rows(row_copy_in, tok_next_ref, 1 - slot, MOE_TILE)

    def refresh(which, ids_ref, up_ref, dn_ref):
        prev = ids_ref[jnp.maximum(i - 1, 0)]

        @pl.when((i == 0) | (ids_ref[i] != prev))
        def _():
            def cast_up(c, carry):
                r = pl.multiple_of(c * CAST_ROWS, CAST_ROWS)
                w_up[which, pl.ds(r, CAST_ROWS), :] = up_ref[0, pl.ds(r, CAST_ROWS), :].astype(BF16)
                return carry

            def cast_dn(c, carry):
                r = pl.multiple_of(c * CAST_ROWS, CAST_ROWS)
                w_dn[which, pl.ds(r, CAST_ROWS), :] = dn_ref[0, pl.ds(r, CAST_ROWS), :].astype(BF16)
                return carry

            lax.fori_loop(0, up_ref.shape[1] // CAST_ROWS, cast_up, 0)
            lax.fori_loop(0, dn_ref.shape[1] // CAST_ROWS, cast_dn, 0)

    @pl.when(i < n_used)
    def _():
        refresh(0, ta_ref, upa_ref, dna_ref)
        refresh(1, tb_ref, upb_ref, dnb_ref)
        wait_gather(slot)
        x = _load_token_major(xbuf.at[slot], MOE_TILE, upa_ref.shape[1]).astype(BF16)

        def ffn(which):
            gu = jnp.dot(x, w_up[which], preferred_element_type=F32)
            hid = jax.nn.silu(gu[:, :D_EXPERT]) * gu[:, D_EXPERT:]
            return jnp.dot(hid.astype(BF16), w_dn[which], preferred_element_type=F32)

        y = ffn(0) * wa_ref[...] + ffn(1) * wb_ref[...]

        @pl.when(i >= 2)
        def _():
            wait_scatter(i - 2, slot)

        _store_token_major(ybuf.at[slot], y)
        start_rows(row_copy_out, tok_ref, slot, valid_ref[i])

        @pl.when(i == n_used - 1)
        def _():
            @pl.when(i >= 1)
            def _():
                wait_scatter(i - 1, 1 - slot)
            wait_scatter(i, slot)


def _moe_call(layer, tile_a, tile_b, n_used, tile_valid, slot_tok, h1_tm, slot_wa, slot_wb, w_up, w_down,
              n_tiles, t):
    d = w_up.shape[1]
    e0 = layer * N_EXPERTS
    wspec = pl.BlockSpec((MOE_TILE, 1), lambda i, *_: (i, 0))
    tok_blk = (1, 1, MOE_TILE)
    grid_spec = pltpu.PrefetchScalarGridSpec(
        num_scalar_prefetch=4,
        grid=(n_tiles,),
        in_specs=[pl.BlockSpec(memory_space=pl.ANY),
                  pl.BlockSpec(tok_blk, lambda i, *_: (i, 0, 0), memory_space=pltpu.SMEM),
                  pl.BlockSpec(tok_blk, lambda i, *_: (jnp.minimum(i + 1, n_tiles - 1), 0, 0),
                               memory_space=pltpu.SMEM),
                  wspec, wspec,
                  pl.BlockSpec((1, d, 2 * D_EXPERT), lambda i, ta, tb, *_: (e0 + ta[i], 0, 0)),
                  pl.BlockSpec((1, d, 2 * D_EXPERT), lambda i, ta, tb, *_: (e0 + tb[i], 0, 0)),
                  pl.BlockSpec((1, D_EXPERT, d), lambda i, ta, tb, *_: (e0 + ta[i], 0, 0)),
                  pl.BlockSpec((1, D_EXPERT, d), lambda i, ta, tb, *_: (e0 + tb[i], 0, 0))],
        out_specs=pl.BlockSpec(memory_space=pl.ANY),
        scratch_shapes=[pltpu.VMEM((2, MOE_TILE * SUBLANES, LANES), F32),
                        pltpu.VMEM((2, MOE_TILE * SUBLANES, LANES), F32),
                        pltpu.VMEM((2, d, 2 * D_EXPERT), BF16), pltpu.VMEM((2, D_EXPERT, d), BF16),
                        pltpu.SemaphoreType.DMA((2,)), pltpu.SemaphoreType.DMA((2,))])
    return pl.pallas_call(
        _moe_kernel,
        grid_spec=grid_spec,
        out_shape=jax.ShapeDtypeStruct((t * SUBLANES, LANES), F32),
        compiler_params=_cparams(("arbitrary",)),
        name="moe",
    )(tile_a, tile_b, n_used, tile_valid, h1_tm, slot_tok, slot_tok, slot_wa, slot_wb, w_up, w_up, w_down, w_down)


def _final_kernel(alpha, h1_ref, m_ref, e_ref, g_ref, b_ref, o_ref):
    rows, d = o_ref.shape
    h1 = _load_token_major(h1_ref, rows, d)
    m = _load_token_major(m_ref, rows, d)
    o_ref[...] = _layer_norm(alpha * h1 + m + e_ref[...].astype(F32), g_ref[...], b_ref[...])


def _final_call(alpha, h1_tm, m_tm, e, g, b):
    t, d = e.shape
    tm = TM_FINAL
    row = lambda i: (i, 0)
    tmaj = pl.BlockSpec((tm * SUBLANES, LANES), row)
    return pl.pallas_call(
        functools.partial(_final_kernel, alpha),
        grid=(t // tm,),
        in_specs=[tmaj, tmaj, pl.BlockSpec((tm, d), row), _full(g.shape), _full(b.shape)],
        out_specs=pl.BlockSpec((tm, d), row),
        out_shape=jax.ShapeDtypeStruct((t, d), F32),
        compiler_params=_cparams(("arbitrary",)),
        name="final_ln",
    )(h1_tm, m_tm, e, g, b)


def _pack_w_in(w_in):
    off_q = OFF_CONV + 2 * D_CONV
    off_kr = off_q + Q_LORA + KV_LORA
    off_gate = off_kr + QK_ROPE
    kr = jnp.pad(w_in[..., off_kr:off_gate], ((0, 0), (0, 0), (ROPE_LANE0, HEAD_BLOCK - ROPE_LANE0 - QK_ROPE)))
    return jnp.concatenate([w_in[..., :off_kr], kr, w_in[..., off_gate:]], axis=-1).astype(BF16)


def _pack_heads(w, lo, width, stride, lane0):
    blocks = []
    for h in range(N_HEADS):
        l0 = lane0(h)
        blocks.append(jnp.pad(w[..., h * stride + lo:h * stride + lo + width],
                              ((0, 0), (0, 0), (l0, HEAD_BLOCK - l0 - width))))
    return jnp.concatenate(blocks, axis=-1).astype(BF16)


def _rope_tables(positions):
    inv_freq = jnp.power(ROPE_THETA, -jnp.arange(0, QK_ROPE, 2, dtype=F32) / QK_ROPE)
    ang = positions.astype(F32).reshape(-1, 1) * inv_freq
    cos, sin = jnp.cos(ang), jnp.sin(ang)
    t = ang.shape[0]
    ones_lo = jnp.ones((t, ROPE_LANE0), F32)
    ones_hi = jnp.ones((t, HEAD_BLOCK - ROPE_LANE0 - QK_ROPE), F32)
    zeros_lo = jnp.zeros((t, ROPE_LANE0), F32)
    zeros_half = jnp.zeros((t, HALF_ROPE), F32)
    zeros_hi = jnp.zeros((t, HEAD_BLOCK - ROPE_LANE0 - QK_ROPE), F32)
    c = jnp.concatenate([ones_lo, cos, cos, ones_hi], axis=-1)
    s1 = jnp.concatenate([zeros_lo, -sin, zeros_half, zeros_hi], axis=-1)
    s2 = jnp.concatenate([zeros_lo, zeros_half, sin, zeros_hi], axis=-1)
    return c, s1, s2


def _routing_plan(cls, wa, wb, n_tiles):
    t = cls.shape[0]
    n_pad = n_tiles * MOE_TILE - t
    assert n_pad == N_CLASSES * MOE_TILE, n_pad
    classes = jnp.arange(N_CLASSES, dtype=jnp.int32)
    counts = jnp.sum((cls[None, :] == classes[:, None]).astype(jnp.int32), axis=1)
    padded = (counts + MOE_TILE - 1) // MOE_TILE * MOE_TILE
    pad_end = jnp.cumsum(padded)
    pad_start = pad_end - padded
    pad_need = padded - counts
    j = jnp.arange(MOE_TILE, dtype=jnp.int32)[None, :]
    pad_key = jnp.where(j < pad_need[:, None], 2 * classes[:, None] + 1, 2 * N_CLASSES).reshape(-1)
    zeros_i = jnp.zeros((n_pad,), jnp.int32)
    zeros_f = jnp.zeros((n_pad,), F32)
    _, slot_tok, slot_wa, slot_wb = lax.sort(
        (jnp.concatenate([2 * cls, pad_key]), jnp.concatenate([jnp.arange(t, dtype=jnp.int32), zeros_i]),
         jnp.concatenate([wa, zeros_f]), jnp.concatenate([wb, zeros_f])), num_keys=1)
    slot_tok = slot_tok.reshape(n_tiles, 1, MOE_TILE)
    slot_wa = slot_wa.reshape(-1, 1)
    slot_wb = slot_wb.reshape(-1, 1)
    tile_row0 = jnp.arange(n_tiles, dtype=jnp.int32) * MOE_TILE
    tile_cls = jnp.minimum(jnp.sum((tile_row0[:, None] >= pad_end[None, :]).astype(jnp.int32), axis=1),
                           N_CLASSES - 1)
    onehot = (tile_cls[:, None] == classes[None, :]).astype(jnp.int32)
    tile_valid = jnp.clip(jnp.sum(onehot * (pad_start + counts)[None, :], axis=1) - tile_row0,
                          0, MOE_TILE).astype(jnp.int32)
    pair_lo = jnp.array([p[0] for p in PAIRS], jnp.int32)
    pair_hi = jnp.array([p[1] for p in PAIRS], jnp.int32)
    grp = tile_cls // len(PAIRS)
    pair_onehot = ((tile_cls % len(PAIRS))[:, None] == jnp.arange(len(PAIRS), dtype=jnp.int32)[None, :])
    tile_a = grp * EXPERTS_PER_GROUP + jnp.sum(pair_onehot * pair_lo[None, :], axis=1)
    tile_b = grp * EXPERTS_PER_GROUP + jnp.sum(pair_onehot * pair_hi[None, :], axis=1)
    n_used = (pad_end[-1] // MOE_TILE).astype(jnp.int32).reshape(1)
    return tile_a, tile_b, n_used, tile_valid, slot_tok, slot_wa, slot_wb


def kernel(x, p, positions, ln_in_g, ln_in_b, w_in, b_gate, pool_w, pool_scale, pool_proj, conv_dw, conv_b, conv_ln_g, conv_ln_b, conv_proj, q_norm_g, w_uq, kv_norm_g, w_ukv, mla_proj, w_out, ln1_g, ln1_b, w_router, router_bias, exp_w_up, exp_w_down, ple_proj, ple_gate, ln2_g, ln2_b):
    batch, seq, d = x.shape
    depth = w_in.shape[0]
    t = batch * seq
    alpha = float((2 * depth) ** 0.25)
    n_tiles = (t + N_CLASSES * (MOE_TILE - 1)) // MOE_TILE + 1

    w_in_p = _pack_w_in(w_in)
    pool_w_b = pool_w.astype(BF16)
    pool_proj_b, conv_proj_b, mla_proj_b = pool_proj.astype(BF16), conv_proj.astype(BF16), mla_proj.astype(BF16)
    qk = QK_NOPE + QK_ROPE
    wq_p = (_pack_heads(w_uq, 0, QK_NOPE, qk, lambda h: 0)
            + _pack_heads(w_uq, QK_NOPE, QK_ROPE, qk, lambda h: ROPE_LANE0))
    wqt_p = jnp.swapaxes(wq_p, 1, 2)
    wk_p = _pack_heads(w_ukv, 0, QK_NOPE, QK_NOPE + V_HEAD, lambda h: 0)
    kv_w = QK_NOPE + V_HEAD
    wvt_p = jnp.swapaxes(jnp.concatenate(
        [jnp.pad(w_ukv[..., h * kv_w + QK_NOPE:(h + 1) * kv_w], ((0, 0), (0, 0), (0, BF16_ROWS)))
         for h in range(N_HEADS)], axis=-1), 1, 2).astype(BF16)
    ones_rows = jnp.tile(jnp.concatenate([jnp.zeros((V_HEAD, 1), F32), jnp.ones((BF16_ROWS, 1), F32)]),
                         (N_HEADS, 1))
    w_out_b, ple_proj_b, ple_gate_b = w_out.astype(BF16), ple_proj.astype(BF16), ple_gate.astype(BF16)
    w_up_b = exp_w_up.reshape((depth * N_EXPERTS,) + exp_w_up.shape[2:])
    w_down_b = exp_w_down.reshape((depth * N_EXPERTS,) + exp_w_down.shape[2:])
    wr_t = w_router.T
    wr_hi = wr_t.astype(BF16)
    wr_lo = (wr_t - wr_hi.astype(F32)).astype(BF16)
    rope_tabs = _rope_tables(positions)
    rope_tabs_t = tuple(a.T for a in rope_tabs)
    p2 = p.reshape(depth, t, -1)

    pre, pre_g, pre_b = (x.reshape(t, d),), ln_in_g, ln_in_b
    for i in range(depth):
        conv_params = (conv_dw[i], conv_b[i].reshape(1, -1), conv_ln_g[i].reshape(1, -1),
                       conv_ln_b[i].reshape(1, -1))
        h, u_pool, a_conv, cqkv, glog = _in_proj_call(alpha, pre, pre_g.reshape(1, d), pre_b.reshape(1, d),
                                                      w_in_p, conv_params, i, seq)
        a_pool = _pool_call(u_pool, pool_w_b[i], pool_scale[i].reshape(1, -1), batch, seq)
        qt, k, vt = _mla_prep_call(cqkv, rope_tabs, rope_tabs_t, q_norm_g[i].reshape(1, -1),
                                   kv_norm_g[i].reshape(1, -1), wqt_p[i], wk_p[i], wvt_p[i], ones_rows)
        a_attn = _attn_call(qt, k, vt, batch, seq)
        h1_tm, e, logits = _merge_call(alpha, i, a_pool, a_conv, a_attn, glog, h, p2,
                                       pool_proj_b, conv_proj_b, mla_proj_b, b_gate[i], w_out_b,
                                       ln1_g[i].reshape(1, d), ln1_b[i].reshape(1, d),
                                       ple_proj_b, ple_gate_b, wr_hi, wr_lo)
        cls, wa, wb = _route_call(logits.reshape(N_EXPERTS, t // LANES, LANES), router_bias)
        plan = _routing_plan(cls.reshape(t), wa.reshape(t), wb.reshape(t), n_tiles)
        m_tm = _moe_call(i, *plan[:5], h1_tm, plan[5], plan[6], w_up_b, w_down_b, n_tiles, t)
        pre, pre_g, pre_b = (h1_tm, m_tm, e), ln2_g[i], ln2_b[i]
    h = _final_call(alpha, *pre, pre_g.reshape(1, d), pre_b.reshape(1, d))
    return h.reshape(batch, seq, d)
```

```python
import functools

import jax
import jax.numpy as jnp
from jax import lax
from jax.experimental import pallas as pl
from jax.experimental.pallas import tpu as pltpu

F32 = jnp.float32
BF16 = jnp.bfloat16

CHUNK = 64
POOL_WINDOWS = (2, 4, 8, 16)
POOL_GW = 128
D_POOL = 512
D_CONV = 512
CONV_WIDTH = 31
N_HEADS = 8
QK_NOPE = 64
QK_ROPE = 32
V_HEAD = 64
Q_LORA = 384
KV_LORA = 256
ROPE_THETA = 10000.0
N_EXPERTS = 16
N_GROUPS = 4
EXPERTS_PER_GROUP = 4
D_EXPERT = 512
LN_EPS = 1e-5
RMS_EPS = 1e-6

LANES = 128
SUBLANES = 8
HEAD_BLOCK = LANES
ROPE_LANE0 = QK_NOPE
HALF_ROPE = QK_ROPE // 2
BF16_ROWS = 16
VT_BLOCK = V_HEAD + BF16_ROWS
LOG2_E = 1.4426950408889634

TM_PROJ = 512
TM_PREP = 512
TM_MERGE = 512
MERGE_SUB = 256
TM_FINAL = 512
TQ = 256
TK = 256
Q_SUBS = 2
ATTN_LOOKAHEAD = 3
POOL_CHUNK = 256
POOL_HALO = 16
CONV_CHUNK = 128
CONV_HALO = 32
MOE_TILE = 256
ROUTE_ROWS = 8

PAIRS = ((0, 1), (0, 2), (0, 3), (1, 2), (1, 3), (2, 3))
N_CLASSES = N_GROUPS * len(PAIRS)

VMEM_LIMIT = 56 * 1024 * 1024


def _cparams(sem):
    return pltpu.CompilerParams(dimension_semantics=sem, vmem_limit_bytes=VMEM_LIMIT)


def _layer_norm(x, g, b):
    mu = jnp.mean(x, axis=-1, keepdims=True)
    xc = x - mu
    var = jnp.mean(xc * xc, axis=-1, keepdims=True)
    return xc * lax.rsqrt(var + LN_EPS) * g + b


def _rms_norm(x, g):
    ms = jnp.mean(x * x, axis=-1, keepdims=True)
    return x * lax.rsqrt(ms + RMS_EPS) * g


def _full(shape):
    n = len(shape)
    return pl.BlockSpec(shape, lambda *_: (0,) * n)


def _layer_full(stacked, layer):
    n = stacked.ndim - 1
    return pl.BlockSpec((None,) + stacked.shape[1:], lambda *_: (layer,) + (0,) * n)


OFF_CONV = D_POOL
OFF_CQKV = OFF_CONV + 2 * D_CONV
W_CQKV = Q_LORA + KV_LORA + HEAD_BLOCK
OFF_GATE = OFF_CQKV + W_CQKV
N_COLS_CHUNK = 512


def _in_proj_kernel(alpha, residual, tiles_per_seq, *refs):
    n_pre = 5 if residual else 3
    if residual:
        h1_ref, m_ref, e_ref, g_ref, b_ref = refs[:n_pre]
    else:
        x_ref, g_ref, b_ref = refs[:n_pre]
    w_ref, dw_ref, cb_ref, cg_ref, cbb_ref, pw_ref, pscale_ref = refs[n_pre:n_pre + 7]
    h_ref, apool_ref, aconv_ref, cqkv_ref, gate_ref, zs, ps = refs[-7:]
    rows, d = h_ref.shape
    seq_tile = pl.program_id(0) % tiles_per_seq

    @pl.when(seq_tile == 0)
    def _():
        zs[0:CONV_HALO, :] = jnp.zeros((CONV_HALO, D_CONV), F32)
        ps[0:POOL_HALO, :] = jnp.zeros((POOL_HALO, D_POOL), F32)

    if residual:
        pre = alpha * _load_token_major(h1_ref, rows, d) + _load_token_major(m_ref, rows, d)
        pre = pre + e_ref[...].astype(F32)
    else:
        pre = x_ref[...]
    h = _layer_norm(pre, g_ref[...], b_ref[...])
    h_ref[...] = h
    x = h.astype(BF16)

    def mm(lo, hi):
        return jnp.dot(x, w_ref[:, lo:hi], preferred_element_type=F32)

    zs[CONV_HALO:CONV_HALO + rows, :] = jax.nn.sigmoid(mm(OFF_CONV + D_CONV, OFF_CQKV))
    zs[CONV_HALO:CONV_HALO + rows, :] = mm(OFF_CONV, OFF_CONV + D_CONV) * zs[CONV_HALO:CONV_HALO + rows, :]
    ps[POOL_HALO:POOL_HALO + rows, :] = mm(0, OFF_CONV)
    cqkv_ref[...] = mm(OFF_CQKV, OFF_GATE).astype(BF16)

    def conv_work(r0):
        aconv_ref[r0:r0 + CONV_CHUNK, :] = _conv_chunk(zs, r0, dw_ref, cb_ref, cg_ref, cbb_ref).astype(BF16)

    def pool_work(r0):
        _pool_chunk(ps, r0, seq_tile * rows, pw_ref, pscale_ref, apool_ref)

    work = [functools.partial(conv_work, r0) for r0 in range(0, rows, CONV_CHUNK)]
    work += [functools.partial(pool_work, r0) for r0 in range(0, rows, POOL_CHUNK)]
    n_gate_chunks = gate_ref.shape[1] // N_COLS_CHUNK
    for c in range(n_gate_chunks):
        gate_ref[:, c * N_COLS_CHUNK:(c + 1) * N_COLS_CHUNK] = mm(
            OFF_GATE + c * N_COLS_CHUNK, OFF_GATE + (c + 1) * N_COLS_CHUNK).astype(BF16)
        for item in work[len(work) * c // n_gate_chunks:len(work) * (c + 1) // n_gate_chunks]:
            item()
    zs[0:CONV_HALO, :] = zs[rows:rows + CONV_HALO, :]
    ps[0:POOL_HALO, :] = ps[rows:rows + POOL_HALO, :]


def _in_proj_call(alpha, pre, g, b, w, branch_params, layer, seq):
    residual = len(pre) == 3
    t, d = pre[-1].shape
    n = w.shape[-1]
    n_gate = n - OFF_GATE
    row = lambda i: (i, 0)
    tmaj = pl.BlockSpec((TM_PROJ * SUBLANES, LANES), row)
    pre_specs = [tmaj, tmaj, pl.BlockSpec((TM_PROJ, d), row)] if residual else [pl.BlockSpec((TM_PROJ, d), row)]
    return pl.pallas_call(
        functools.partial(_in_proj_kernel, alpha, residual, seq // TM_PROJ),
        grid=(t // TM_PROJ,),
        in_specs=pre_specs + [_full(g.shape), _full(b.shape), _layer_full(w, layer)]
        + [_full(a.shape) for a in branch_params],
        out_specs=[pl.BlockSpec((TM_PROJ, d), row),
                   pl.BlockSpec((TM_PROJ, D_POOL), row), pl.BlockSpec((TM_PROJ, D_CONV), row),
                   pl.BlockSpec((TM_PROJ, W_CQKV), row), pl.BlockSpec((TM_PROJ, n_gate), row)],
        out_shape=[jax.ShapeDtypeStruct((t, d), F32),
                   jax.ShapeDtypeStruct((t, D_POOL), BF16), jax.ShapeDtypeStruct((t, D_CONV), BF16),
                   jax.ShapeDtypeStruct((t, W_CQKV), BF16), jax.ShapeDtypeStruct((t, n_gate), BF16)],
        scratch_shapes=[pltpu.VMEM((TM_PROJ + CONV_HALO, D_CONV), F32),
                        pltpu.VMEM((TM_PROJ + POOL_HALO, D_POOL), F32)],
        compiler_params=_cparams(("arbitrary",)),
        name="in_proj",
    )(*pre, g, b, w, *branch_params)


def _pool_chunk(ps, r0, seq_row0, w_ref, scale_ref, o_ref):
    t = seq_row0 + r0 + lax.broadcasted_iota(jnp.int32, (POOL_CHUNK, 1), 0)
    for g, w in enumerate(POOL_WINDOWS):
        cols = slice(g * POOL_GW, (g + 1) * POOL_GW)
        xw = ps[r0:r0 + POOL_CHUNK + POOL_HALO, cols]
        acc = xw
        k = 1
        while k < w:
            acc = acc + pltpu.roll(acc, k, axis=0)
            k *= 2
        cnt = jnp.minimum(t + 1, w).astype(F32)
        mixed = acc[POOL_HALO:] / cnt - xw[POOL_HALO:]
        y = jnp.dot(mixed.astype(BF16), w_ref[g], preferred_element_type=F32) * scale_ref[:, cols]
        o_ref[r0:r0 + POOL_CHUNK, cols] = y.astype(BF16)


def _conv_chunk(zs, r0, dw_ref, cb_ref, g_ref, b_ref):
    parts = []
    for cg in range(D_CONV // LANES):
        cols = slice(cg * LANES, (cg + 1) * LANES)
        win = zs[r0:r0 + CONV_CHUNK + CONV_HALO, cols]
        acc = jnp.zeros((CONV_CHUNK, LANES), F32) + cb_ref[:, cols]
        for sub in range(SUBLANES):
            shifted = win if sub == 0 else pltpu.roll(win, sub, axis=0)
            for a in range(CONV_HALO // SUBLANES):
                lag = SUBLANES * a + sub
                if lag >= CONV_WIDTH:
                    continue
                k = CONV_WIDTH - 1 - lag
                lo = CONV_HALO - SUBLANES * a
                acc = acc + dw_ref[k:k + 1, cols] * shifted[lo:lo + CONV_CHUNK]
        parts.append(acc)
    y = _layer_norm(jnp.concatenate(parts, axis=-1), g_ref[...], b_ref[...])
    return y * jax.nn.sigmoid(y)


def _rope_block(x, c, s1, s2):
    return x * c + pltpu.roll(x, LANES - HALF_ROPE, axis=1) * s1 + pltpu.roll(x, HALF_ROPE, axis=1) * s2


def _rope_block_t(x, c, s1, s2):
    return x * c + pltpu.roll(x, HEAD_BLOCK - HALF_ROPE, axis=0) * s1 + pltpu.roll(x, HALF_ROPE, axis=0) * s2


def _mla_prep_kernel(cqkv_ref, c_ref, s1_ref, s2_ref, ct_ref, s1t_ref, s2t_ref, qg_ref, kvg_ref,
                     wqt_ref, wk_ref, wvt_ref, ones_ref, qt_ref, k_ref, vt_ref):
    cq = _rms_norm(cqkv_ref[:, 0:Q_LORA].astype(F32), qg_ref[...]).astype(BF16)
    ckv = _rms_norm(cqkv_ref[:, Q_LORA:Q_LORA + KV_LORA].astype(F32), kvg_ref[...]).astype(BF16)
    kr = _rope_block(cqkv_ref[:, Q_LORA + KV_LORA:W_CQKV].astype(F32), c_ref[...], s1_ref[...], s2_ref[...])
    scale = float((QK_NOPE + QK_ROPE) ** -0.5 * LOG2_E)
    nt = (((1,), (1,)), ((), ()))
    ct, s1t, s2t = ct_ref[...], s1t_ref[...], s2t_ref[...]
    for h in range(N_HEADS):
        cols = slice(h * HEAD_BLOCK, (h + 1) * HEAD_BLOCK)
        qt = lax.dot_general(wqt_ref[cols, :], cq, nt, preferred_element_type=F32) * scale
        qt_ref[cols, :] = _rope_block_t(qt, ct, s1t, s2t).astype(BF16)
        k = jnp.dot(ckv, wk_ref[:, cols], preferred_element_type=F32) + kr
        k_ref[:, cols] = k.astype(BF16)
    vt = lax.dot_general(wvt_ref[...], ckv, nt, preferred_element_type=F32) + ones_ref[...]
    vt_ref[...] = vt.astype(BF16)


def _mla_prep_call(cqkv, tabs, tabs_t, qg, kvg, wqt, wk, wvt, ones_rows):
    t = cqkv.shape[0]
    n = N_HEADS * HEAD_BLOCK
    nv = N_HEADS * VT_BLOCK
    row = lambda i: (i, 0)
    col = lambda i: (0, i)
    tab = pl.BlockSpec((TM_PREP, LANES), row)
    tab_t = pl.BlockSpec((LANES, TM_PREP), col)
    return pl.pallas_call(
        _mla_prep_kernel,
        grid=(t // TM_PREP,),
        in_specs=[pl.BlockSpec((TM_PREP, W_CQKV), row), tab, tab, tab, tab_t, tab_t, tab_t,
                  _full(qg.shape), _full(kvg.shape), _full(wqt.shape), _full(wk.shape), _full(wvt.shape),
                  _full(ones_rows.shape)],
        out_specs=[pl.BlockSpec((n, TM_PREP), col), pl.BlockSpec((TM_PREP, n), row),
                   pl.BlockSpec((nv, TM_PREP), col)],
        out_shape=[jax.ShapeDtypeStruct((n, t), BF16), jax.ShapeDtypeStruct((t, n), BF16),
                   jax.ShapeDtypeStruct((nv, t), BF16)],
        compiler_params=_cparams(("arbitrary",)),
        name="mla_prep",
    )(cqkv, *tabs, *tabs_t, qg, kvg, wqt, wk, wvt, ones_rows)


def _attn_kernel(qt_ref, k_ref, vt_ref, o_ref, m_ref, acc_ref):
    j = pl.program_id(1)
    key_chunk = lax.broadcasted_iota(jnp.int32, (TK, TQ), 0) // CHUNK
    qry_chunk = lax.broadcasted_iota(jnp.int32, (TK, TQ), 1) // CHUNK
    diag_mask = key_chunk <= qry_chunk

    def scores(h, sub, kt):
        k0 = pl.multiple_of(kt * TK, TK)
        kk = k_ref[pl.ds(k0, TK), h * HEAD_BLOCK:(h + 1) * HEAD_BLOCK]
        return jnp.dot(kk, qt_ref[h * HEAD_BLOCK:(h + 1) * HEAD_BLOCK, sub * TQ:(sub + 1) * TQ],
                       preferred_element_type=F32)

    def values_t(h, kt):
        k0 = pl.multiple_of(kt * TK, TK)
        return vt_ref[h * VT_BLOCK:(h + 1) * VT_BLOCK, pl.ds(k0, TK)]

    def first_update(h, sub, kt, masked, s):
        st = h * Q_SUBS + sub
        if masked:
            s = jnp.where(diag_mask, s, -jnp.inf)
        m = jnp.max(s, axis=0, keepdims=True)
        p = jnp.exp2(s - m)
        m_ref[st] = jnp.broadcast_to(m, (SUBLANES, TQ))
        acc_ref[st] = jnp.dot(values_t(h, kt), p.astype(BF16), preferred_element_type=F32)

    def update(h, sub, kt, masked, s):
        st = h * Q_SUBS + sub
        if masked:
            s = jnp.where(diag_mask, s, -jnp.inf)
        m_old = m_ref[st]
        m_new = jnp.maximum(m_old, jnp.max(s, axis=0, keepdims=True))
        alpha = jnp.exp2(m_old - m_new)
        p = jnp.exp2(s - m_new[0:1, :])
        m_ref[st] = m_new
        acc_ref[st] = alpha[0:1, :] * acc_ref[st] + jnp.dot(values_t(h, kt), p.astype(BF16),
                                                           preferred_element_type=F32)

    def run(work):
        pending = {}
        for idx in range(len(work) + ATTN_LOOKAHEAD):
            if idx < len(work):
                h, sub, kt, _, _ = work[idx]
                pending[idx] = scores(h, sub, kt)
            if idx >= ATTN_LOOKAHEAD:
                h, sub, kt, masked, upd = work[idx - ATTN_LOOKAHEAD]
                upd(h, sub, kt, masked, pending.pop(idx - ATTN_LOOKAHEAD))

    tail = []
    for h in range(N_HEADS):
        for sub in range(Q_SUBS):
            for kk in range(sub + 1):
                tail.append((h, sub, Q_SUBS * j + kk, kk == sub, first_update if kk == 0 else update))
    run(tail)

    def step(kt, carry):
        run([(h, sub, kt, False, update) for h in range(N_HEADS) for sub in range(Q_SUBS)])
        return carry

    lax.fori_loop(0, Q_SUBS * j, step, 0)

    for sub in range(Q_SUBS):
        outs = []
        for h in range(N_HEADS):
            st = h * Q_SUBS + sub
            outs.append(acc_ref[st, 0:V_HEAD, :] * (1.0 / acc_ref[st, V_HEAD:V_HEAD + 1, :]))
        o_ref[sub * TQ:(sub + 1) * TQ, :] = jnp.concatenate(outs, axis=0).T.astype(BF16)


def _attn_call(qt, k, vt, batch, seq):
    n = N_HEADS * HEAD_BLOCK
    tq = Q_SUBS * TQ
    nq = seq // tq
    return pl.pallas_call(
        _attn_kernel,
        grid=(batch, nq),
        in_specs=[pl.BlockSpec((n, tq), lambda b, i: (0, b * nq + i)),
                  pl.BlockSpec((seq, n), lambda b, i: (b, 0)),
                  pl.BlockSpec((N_HEADS * VT_BLOCK, seq), lambda b, i: (0, b))],
        out_specs=pl.BlockSpec((tq, N_HEADS * V_HEAD), lambda b, i: (b * nq + i, 0)),
        out_shape=jax.ShapeDtypeStruct((k.shape[0], N_HEADS * V_HEAD), BF16),
        scratch_shapes=[pltpu.VMEM((N_HEADS * Q_SUBS, SUBLANES, TQ), F32),
                        pltpu.VMEM((N_HEADS * Q_SUBS, VT_BLOCK, TQ), F32)],
        compiler_params=_cparams(("arbitrary", "arbitrary")),
        name="attn",
    )(qt, k, vt)


def _store_token_major(ref, x):
    rows = x.shape[0]
    for j in range(x.shape[1] // LANES):
        ref[pl.ds(j, rows, stride=SUBLANES), :] = x[:, j * LANES:(j + 1) * LANES]


def _load_token_major(ref, rows, d):
    return jnp.concatenate([ref[pl.ds(j, rows, stride=SUBLANES), :] for j in range(d // LANES)], axis=-1)


def _merge_kernel(alpha, ap_ref, ac_ref, at_ref, gl_ref, h_ref, p_ref, pp_ref, cp_ref, mp_ref, bg_ref,
                  wo_ref, g_ref, b_ref, pproj_ref, pgate_ref, wrh_ref, wrl_ref,
                  h1_ref, e_ref, lg_ref):
    d = h_ref.shape[1]
    sub = MERGE_SUB
    parts = [pl.ds(k * sub, sub) for k in range(h_ref.shape[0] // sub)]
    nt = (((1,), (1,)), ((), ()))

    def merged_of(rows):
        merged = None
        for br, (a_ref, w_ref) in enumerate(((ap_ref, pp_ref), (ac_ref, cp_ref), (at_ref, mp_ref))):
            y = jnp.dot(a_ref[rows, :], w_ref[...], preferred_element_type=F32)
            gate = jax.nn.sigmoid(gl_ref[rows, br * d:(br + 1) * d].astype(F32) + bg_ref[br:br + 1, :])
            merged = gate * y if merged is None else merged + gate * y
        return merged.astype(BF16)

    merged = [merged_of(rows) for rows in parts]
    ys = [jnp.dot(m, wo_ref[...], preferred_element_type=F32) for m in merged]
    es = [jnp.dot(p_ref[rows, :].astype(BF16), pproj_ref[...], preferred_element_type=F32) for rows in parts]
    h1s = [_layer_norm(alpha * h_ref[rows, :] + y, g_ref[...], b_ref[...]) for rows, y in zip(parts, ys)]
    for k, (rows, h1, e) in enumerate(zip(parts, h1s, es)):
        h1b = h1.astype(BF16)
        e = e * jax.nn.sigmoid(jnp.dot(h1b, pgate_ref[...], preferred_element_type=F32))
        e_ref[rows, :] = e.astype(BF16)
        h1l = (h1 - h1b.astype(F32)).astype(BF16)
        lg = lax.dot_general(wrh_ref[...], h1b, nt, preferred_element_type=F32)
        lg = lg + lax.dot_general(wrh_ref[...], h1l, nt, preferred_element_type=F32)
        lg = lg + lax.dot_general(wrl_ref[...], h1b, nt, preferred_element_type=F32)
        lg_ref[:, k * sub:(k + 1) * sub] = lg
        _store_token_major(h1_ref.at[pl.ds(k * sub * SUBLANES, sub * SUBLANES), :], h1)


def _merge_call(alpha, layer, ap, ac, at, gl, h, p, pp, cp, mp, bg, wo, g, b, pproj, pgate, wrh, wrl):
    t, d = h.shape
    tm = TM_MERGE
    row = lambda i: (i, 0)
    ins = [ap, ac, at, gl, h]
    in_specs = [pl.BlockSpec((tm, a.shape[1]), row) for a in ins]
    in_specs.append(pl.BlockSpec((None, tm, p.shape[-1]), lambda i: (layer, i, 0)))
    stacked = {id(a) for a in (pp, cp, mp, wo, pproj, pgate)}
    consts = [pp, cp, mp, bg, wo, g, b, pproj, pgate, wrh, wrl]
    in_specs += [_layer_full(a, layer) if id(a) in stacked else _full(a.shape) for a in consts]
    return pl.pallas_call(
        functools.partial(_merge_kernel, alpha),
        grid=(t // tm,),
        in_specs=in_specs,
        out_specs=[pl.BlockSpec((tm * SUBLANES, LANES), row), pl.BlockSpec((tm, d), row),
                   pl.BlockSpec((N_EXPERTS, tm), lambda i: (0, i))],
        out_shape=[jax.ShapeDtypeStruct((t * SUBLANES, LANES), F32), jax.ShapeDtypeStruct((t, d), BF16),
                   jax.ShapeDtypeStruct((N_EXPERTS, t), F32)],
        compiler_params=_cparams(("arbitrary",)),
        name="merge",
    )(*ins, p, *consts)


def _route_kernel(lg_ref, bias_ref, cls_ref, wa_ref, wb_ref):
    aff = [jax.nn.sigmoid(lg_ref[e]) for e in range(N_EXPERTS)]
    sel = [aff[e] + bias_ref[e] for e in range(N_EXPERTS)]
    n = EXPERTS_PER_GROUP

    def top2_sum(vals):
        best = None
        for a, b in PAIRS:
            s = vals[a] + vals[b]
            best = s if best is None else jnp.maximum(best, s)
        return best

    grp = jnp.zeros(aff[0].shape, jnp.int32)
    best = top2_sum(sel[0:n])
    for g in range(1, N_GROUPS):
        sc = top2_sum(sel[g * n:(g + 1) * n])
        better = sc > best
        grp = jnp.where(better, g, grp)
        best = jnp.where(better, sc, best)
    vs, afs = [], []
    for j in range(n):
        v, a = sel[j], aff[j]
        for g in range(1, N_GROUPS):
            v = jnp.where(grp == g, sel[g * n + j], v)
            a = jnp.where(grp == g, aff[g * n + j], a)
        vs.append(v)
        afs.append(a)
    first = jnp.zeros_like(grp)
    fv = vs[0]
    for j in range(1, n):
        better = vs[j] > fv
        first = jnp.where(better, j, first)
        fv = jnp.where(better, vs[j], fv)
    second = jnp.full_like(grp, -1)
    sv = jnp.full_like(fv, -jnp.inf)
    for j in range(n):
        better = (first != j) & ((second < 0) | (vs[j] > sv))
        second = jnp.where(better, j, second)
        sv = jnp.where(better, vs[j], sv)
    lo = jnp.minimum(first, second)
    hi = jnp.maximum(first, second)
    a_lo, a_hi = afs[0], afs[0]
    for j in range(1, n):
        a_lo = jnp.where(lo == j, afs[j], a_lo)
        a_hi = jnp.where(hi == j, afs[j], a_hi)
    pair = jnp.zeros_like(grp)
    for idx, (a, b) in enumerate(PAIRS):
        pair = jnp.where((lo == a) & (hi == b), idx, pair)
    tot = a_lo + a_hi
    cls_ref[...] = grp * len(PAIRS) + pair
    wa_ref[...] = a_lo / tot
    wb_ref[...] = a_hi / tot


def _route_call(lg3, bias):
    _, rows, lanes = lg3.shape
    blk = pl.BlockSpec((ROUTE_ROWS, lanes), lambda i: (i, 0))
    return pl.pallas_call(
        _route_kernel,
        grid=(rows // ROUTE_ROWS,),
        in_specs=[pl.BlockSpec((N_EXPERTS, ROUTE_ROWS, lanes), lambda i: (0, i, 0)),
                  pl.BlockSpec(memory_space=pltpu.SMEM)],
        out_specs=[blk, blk, blk],
        out_shape=[jax.ShapeDtypeStruct((rows, lanes), jnp.int32),
                   jax.ShapeDtypeStruct((rows, lanes), F32), jax.ShapeDtypeStruct((rows, lanes), F32)],
        compiler_params=_cparams(("arbitrary",)),
        name="route",
    )(lg3, bias)


GATHER_UNROLL = 8
CAST_ROWS = 128


def _moe_kernel(ta_ref, tb_ref, nused_ref, valid_ref,
                h_hbm, tok_ref, tok_next_ref, wa_ref, wb_ref, upa_ref, upb_ref, dna_ref, dnb_ref,
                m_hbm, xbuf, ybuf, w_up, w_dn, gsem, ssem):
    i = pl.program_id(0)
    n_used = nused_ref[0]
    slot = i % 2
    rows = MOE_TILE * SUBLANES

    def row_copy_in(toks, s, r):
        tok = toks[0, 0, r]
        return pltpu.make_async_copy(
            h_hbm.at[pl.ds(pl.multiple_of(tok * SUBLANES, SUBLANES), SUBLANES), :],
            xbuf.at[s, pl.ds(pl.multiple_of(r * SUBLANES, SUBLANES), SUBLANES), :],
            gsem.at[s])

    def row_copy_out(toks, s, r):
        tok = toks[0, 0, r]
        return pltpu.make_async_copy(
            ybuf.at[s, pl.ds(pl.multiple_of(r * SUBLANES, SUBLANES), SUBLANES), :],
            m_hbm.at[pl.ds(pl.multiple_of(tok * SUBLANES, SUBLANES), SUBLANES), :],
            ssem.at[s])

    def start_rows(make, toks, s, count):
        def body8(c, carry):
            for u in range(GATHER_UNROLL):
                make(toks, s, c * GATHER_UNROLL + u).start()
            return carry

        def body1(r, carry):
            make(toks, s, r).start()
            return carry

        full = count // GATHER_UNROLL
        lax.fori_loop(0, full, body8, 0)
        lax.fori_loop(full * GATHER_UNROLL, count, body1, 0)

    def wait_gather(tile, s):
        n = pl.multiple_of(valid_ref[tile] * SUBLANES, SUBLANES)
        pltpu.make_async_copy(h_hbm.at[pl.ds(0, n), :], xbuf.at[s, pl.ds(0, n), :], gsem.at[s]).wait()

    def wait_scatter(tile, s):
        n = pl.multiple_of(valid_ref[tile] * SUBLANES, SUBLANES)
        pltpu.make_async_copy(ybuf.at[s, pl.ds(0, n), :], m_hbm.at[pl.ds(0, n), :], ssem.at[s]).wait()

    @pl.when(i == 0)
    def _():
        xbuf[...] = jnp.zeros(xbuf.shape, F32)
        start_rows(row_copy_in, tok_ref, 0, valid_ref[0])

    @pl.when(i + 1 < n_used)
    def _():
        start_rows(row_copy_in, tok_next_ref, 1 - slot, valid_ref[i + 1])

    def refresh(which, ids_ref, up_ref, dn_ref):
        prev = ids_ref[jnp.maximum(i - 1, 0)]

        @pl.when((i == 0) | (ids_ref[i] != prev))
        def _():
            def cast_up(c, carry):
                r = pl.multiple_of(c * CAST_ROWS, CAST_ROWS)
                w_up[which, pl.ds(r, CAST_ROWS), :] = up_ref[0, pl.ds(r, CAST_ROWS), :].astype(BF16)
                return carry

            def cast_dn(c, carry):
                r = pl.multiple_of(c * CAST_ROWS, CAST_ROWS)
                w_dn[which, pl.ds(r, CAST_ROWS), :] = dn_ref[0, pl.ds(r, CAST_ROWS), :].astype(BF16)
                return carry

            lax.fori_loop(0, up_ref.shape[1] // CAST_ROWS, cast_up, 0)
            lax.fori_loop(0, dn_ref.shape[1] // CAST_ROWS, cast_dn, 0)

    @pl.when(i < n_used)
    def _():
        refresh(0, ta_ref, upa_ref, dna_ref)
        refresh(1, tb_ref, upb_ref, dnb_ref)
        wait_gather(i, slot)
        x = _load_token_major(xbuf.at[slot], MOE_TILE, upa_ref.shape[1]).astype(BF16)

        def ffn(which):
            gu = jnp.dot(x, w_up[which], preferred_element_type=F32)
            hid = jax.nn.silu(gu[:, :D_EXPERT]) * gu[:, D_EXPERT:]
            return jnp.dot(hid.astype(BF16), w_dn[which], preferred_element_type=F32)

        y = ffn(0) * wa_ref[...] + ffn(1) * wb_ref[...]

        @pl.when(i >= 2)
        def _():
            wait_scatter(i - 2, slot)

        _store_token_major(ybuf.at[slot], y)
        start_rows(row_copy_out, tok_ref, slot, valid_ref[i])

        @pl.when(i == n_used - 1)
        def _():
            @pl.when(i >= 1)
            def _():
                wait_scatter(i - 1, 1 - slot)
            wait_scatter(i, slot)


def _moe_call(layer, tile_a, tile_b, n_used, tile_valid, slot_tok, h1_tm, slot_wa, slot_wb, w_up, w_down,
              n_tiles, t):
    d = w_up.shape[1]
    e0 = layer * N_EXPERTS
    wspec = pl.BlockSpec((MOE_TILE, 1), lambda i, *_: (i, 0))
    tok_blk = (1, 1, MOE_TILE)
    grid_spec = pltpu.PrefetchScalarGridSpec(
        num_scalar_prefetch=4,
        grid=(n_tiles,),
        in_specs=[pl.BlockSpec(memory_space=pl.ANY),
                  pl.BlockSpec(tok_blk, lambda i, *_: (i, 0, 0), memory_space=pltpu.SMEM),
                  pl.BlockSpec(tok_blk, lambda i, *_: (jnp.minimum(i + 1, n_tiles - 1), 0, 0),
                               memory_space=pltpu.SMEM),
                  wspec, wspec,
                  pl.BlockSpec((1, d, 2 * D_EXPERT), lambda i, ta, tb, *_: (e0 + ta[i], 0, 0)),
                  pl.BlockSpec((1, d, 2 * D_EXPERT), lambda i, ta, tb, *_: (e0 + tb[i], 0, 0)),
                  pl.BlockSpec((1, D_EXPERT, d), lambda i, ta, tb, *_: (e0 + ta[i], 0, 0)),
                  pl.BlockSpec((1, D_EXPERT, d), lambda i, ta, tb, *_: (e0 + tb[i], 0, 0))],
        out_specs=pl.BlockSpec(memory_space=pl.ANY),
        scratch_shapes=[pltpu.VMEM((2, MOE_TILE * SUBLANES, LANES), F32),
                        pltpu.VMEM((2, MOE_TILE * SUBLANES, LANES), F32),
                        pltpu.VMEM((2, d, 2 * D_EXPERT), BF16), pltpu.VMEM((2, D_EXPERT, d), BF16),
                        pltpu.SemaphoreType.DMA((2,)), pltpu.SemaphoreType.DMA((2,))])
    return pl.pallas_call(
        _moe_kernel,
        grid_spec=grid_spec,
        out_shape=jax.ShapeDtypeStruct((t * SUBLANES, LANES), F32),
        compiler_params=_cparams(("arbitrary",)),
        name="moe",
    )(tile_a, tile_b, n_used, tile_valid, h1_tm, slot_tok, slot_tok, slot_wa, slot_wb, w_up, w_up, w_down, w_down)


def _final_kernel(alpha, h1_ref, m_ref, e_ref, g_ref, b_ref, o_ref):
    rows, d = o_ref.shape
    h1 = _load_token_major(h1_ref, rows, d)
    m = _load_token_major(m_ref, rows, d)
    o_ref[...] = _layer_norm(alpha * h1 + m + e_ref[...].astype(F32), g_ref[...], b_ref[...])


def _final_call(alpha, h1_tm, m_tm, e, g, b):
    t, d = e.shape
    tm = TM_FINAL
    row = lambda i: (i, 0)
    tmaj = pl.BlockSpec((tm * SUBLANES, LANES), row)
    return pl.pallas_call(
        functools.partial(_final_kernel, alpha),
        grid=(t // tm,),
        in_specs=[tmaj, tmaj, pl.BlockSpec((tm, d), row), _full(g.shape), _full(b.shape)],
        out_specs=pl.BlockSpec((tm, d), row),
        out_shape=jax.ShapeDtypeStruct((t, d), F32),
        compiler_params=_cparams(("arbitrary",)),
        name="final_ln",
    )(h1_tm, m_tm, e, g, b)


def _pack_w_in(w_in):
    off_q = OFF_CONV + 2 * D_CONV
    off_kr = off_q + Q_LORA + KV_LORA
    off_gate = off_kr + QK_ROPE
    kr = jnp.pad(w_in[..., off_kr:off_gate], ((0, 0), (0, 0), (ROPE_LANE0, HEAD_BLOCK - ROPE_LANE0 - QK_ROPE)))
    return jnp.concatenate([w_in[..., :off_kr], kr, w_in[..., off_gate:]], axis=-1).astype(BF16)


def _pack_heads(w, lo, width, stride, lane0):
    blocks = []
    for h in range(N_HEADS):
        l0 = lane0(h)
        blocks.append(jnp.pad(w[..., h * stride + lo:h * stride + lo + width],
                              ((0, 0), (0, 0), (l0, HEAD_BLOCK - l0 - width))))
    return jnp.concatenate(blocks, axis=-1).astype(BF16)


def _rope_tables(positions):
    inv_freq = jnp.power(ROPE_THETA, -jnp.arange(0, QK_ROPE, 2, dtype=F32) / QK_ROPE)
    ang = positions.astype(F32).reshape(-1, 1) * inv_freq
    cos, sin = jnp.cos(ang), jnp.sin(ang)
    t = ang.shape[0]
    ones_lo = jnp.ones((t, ROPE_LANE0), F32)
    ones_hi = jnp.ones((t, HEAD_BLOCK - ROPE_LANE0 - QK_ROPE), F32)
    zeros_lo = jnp.zeros((t, ROPE_LANE0), F32)
    zeros_half = jnp.zeros((t, HALF_ROPE), F32)
    zeros_hi = jnp.zeros((t, HEAD_BLOCK - ROPE_LANE0 - QK_ROPE), F32)
    c = jnp.concatenate([ones_lo, cos, cos, ones_hi], axis=-1)
    s1 = jnp.concatenate([zeros_lo, -sin, zeros_half, zeros_hi], axis=-1)
    s2 = jnp.concatenate([zeros_lo, zeros_half, sin, zeros_hi], axis=-1)
    return c, s1, s2


def _routing_plan(cls, wa, wb, n_tiles):
    t = cls.shape[0]
    n_pad = n_tiles * MOE_TILE - t
    assert n_pad == N_CLASSES * MOE_TILE, n_pad
    classes = jnp.arange(N_CLASSES, dtype=jnp.int32)
    counts = jnp.sum((cls[None, :] == classes[:, None]).astype(jnp.int32), axis=1)
    padded = (counts + MOE_TILE - 1) // MOE_TILE * MOE_TILE
    pad_end = jnp.cumsum(padded)
    pad_start = pad_end - padded
    pad_need = padded - counts
    j = jnp.arange(MOE_TILE, dtype=jnp.int32)[None, :]
    pad_key = jnp.where(j < pad_need[:, None], 2 * classes[:, None] + 1, 2 * N_CLASSES).reshape(-1)
    zeros_i = jnp.zeros((n_pad,), jnp.int32)
    zeros_f = jnp.zeros((n_pad,), F32)
    _, slot_tok, slot_wa, slot_wb = lax.sort(
        (jnp.concatenate([2 * cls, pad_key]), jnp.concatenate([jnp.arange(t, dtype=jnp.int32), zeros_i]),
         jnp.concatenate([wa, zeros_f]), jnp.concatenate([wb, zeros_f])), num_keys=1)
    slot_tok = slot_tok.reshape(n_tiles, 1, MOE_TILE)
    slot_wa = slot_wa.reshape(-1, 1)
    slot_wb = slot_wb.reshape(-1, 1)
    tile_row0 = jnp.arange(n_tiles, dtype=jnp.int32) * MOE_TILE
    tile_cls = jnp.minimum(jnp.sum((tile_row0[:, None] >= pad_end[None, :]).astype(jnp.int32), axis=1),
                           N_CLASSES - 1)
    onehot = (tile_cls[:, None] == classes[None, :]).astype(jnp.int32)
    tile_valid = jnp.clip(jnp.sum(onehot * (pad_start + counts)[None, :], axis=1) - tile_row0,
                          0, MOE_TILE).astype(jnp.int32)
    pair_lo = jnp.array([p[0] for p in PAIRS], jnp.int32)
    pair_hi = jnp.array([p[1] for p in PAIRS], jnp.int32)
    grp = tile_cls // len(PAIRS)
    pair_onehot = ((tile_cls % len(PAIRS))[:, None] == jnp.arange(len(PAIRS), dtype=jnp.int32)[None, :])
    tile_a = grp * EXPERTS_PER_GROUP + jnp.sum(pair_onehot * pair_lo[None, :], axis=1)
    tile_b = grp * EXPERTS_PER_GROUP + jnp.sum(pair_onehot * pair_hi[None, :], axis=1)
    n_used = (pad_end[-1] // MOE_TILE).astype(jnp.int32).reshape(1)
    return tile_a, tile_b, n_used, tile_valid, slot_tok, slot_wa, slot_wb


def kernel(x, p, positions, ln_in_g, ln_in_b, w_in, b_gate, pool_w, pool_scale, pool_proj, conv_dw, conv_b, conv_ln_g, conv_ln_b, conv_proj, q_norm_g, w_uq, kv_norm_g, w_ukv, mla_proj, w_out, ln1_g, ln1_b, w_router, router_bias, exp_w_up, exp_w_down, ple_proj, ple_gate, ln2_g, ln2_b):
    batch, seq, d = x.shape
    depth = w_in.shape[0]
    t = batch * seq
    alpha = float((2 * depth) ** 0.25)
    n_tiles = (t + N_CLASSES * (MOE_TILE - 1)) // MOE_TILE + 1

    w_in_p = _pack_w_in(w_in)
    pool_w_b = pool_w.astype(BF16)
    pool_proj_b, conv_proj_b, mla_proj_b = pool_proj.astype(BF16), conv_proj.astype(BF16), mla_proj.astype(BF16)
    qk = QK_NOPE + QK_ROPE
    wq_p = (_pack_heads(w_uq, 0, QK_NOPE, qk, lambda h: 0)
            + _pack_heads(w_uq, QK_NOPE, QK_ROPE, qk, lambda h: ROPE_LANE0))
    wqt_p = jnp.swapaxes(wq_p, 1, 2)
    wk_p = _pack_heads(w_ukv, 0, QK_NOPE, QK_NOPE + V_HEAD, lambda h: 0)
    kv_w = QK_NOPE + V_HEAD
    wvt_p = jnp.swapaxes(jnp.concatenate(
        [jnp.pad(w_ukv[..., h * kv_w + QK_NOPE:(h + 1) * kv_w], ((0, 0), (0, 0), (0, BF16_ROWS)))
         for h in range(N_HEADS)], axis=-1), 1, 2).astype(BF16)
    ones_rows = jnp.tile(jnp.concatenate([jnp.zeros((V_HEAD, 1), F32), jnp.ones((BF16_ROWS, 1), F32)]),
                         (N_HEADS, 1))
    w_out_b, ple_proj_b, ple_gate_b = w_out.astype(BF16), ple_proj.astype(BF16), ple_gate.astype(BF16)
    w_up_b = exp_w_up.reshape((depth * N_EXPERTS,) + exp_w_up.shape[2:])
    w_down_b = exp_w_down.reshape((depth * N_EXPERTS,) + exp_w_down.shape[2:])
    wr_t = w_router.T
    wr_hi = wr_t.astype(BF16)
    wr_lo = (wr_t - wr_hi.astype(F32)).astype(BF16)
    rope_tabs = _rope_tables(positions)
    rope_tabs_t = tuple(a.T for a in rope_tabs)
    p2 = p.reshape(depth, t, -1)

    pre, pre_g, pre_b = (x.reshape(t, d),), ln_in_g, ln_in_b
    for i in range(depth):
        branch_params = (conv_dw[i], conv_b[i].reshape(1, -1), conv_ln_g[i].reshape(1, -1),
                         conv_ln_b[i].reshape(1, -1), pool_w_b[i], pool_scale[i].reshape(1, -1))
        h, a_pool, a_conv, cqkv, glog = _in_proj_call(alpha, pre, pre_g.reshape(1, d), pre_b.reshape(1, d),
                                                      w_in_p, branch_params, i, seq)
        qt, k, vt = _mla_prep_call(cqkv, rope_tabs, rope_tabs_t, q_norm_g[i].reshape(1, -1),
                                   kv_norm_g[i].reshape(1, -1), wqt_p[i], wk_p[i], wvt_p[i], ones_rows)
        a_attn = _attn_call(qt, k, vt, batch, seq)
        h1_tm, e, logits = _merge_call(alpha, i, a_pool, a_conv, a_attn, glog, h, p2,
                                       pool_proj_b, conv_proj_b, mla_proj_b, b_gate[i], w_out_b,
                                       ln1_g[i].reshape(1, d), ln1_b[i].reshape(1, d),
                                       ple_proj_b, ple_gate_b, wr_hi, wr_lo)
        cls, wa, wb = _route_call(logits.reshape(N_EXPERTS, t // LANES, LANES), router_bias)
        plan = _routing_plan(cls.reshape(t), wa.reshape(t), wb.reshape(t), n_tiles)
        m_tm = _moe_call(i, *plan[:5], h1_tm, plan[5], plan[6], w_up_b, w_down_b, n_tiles, t)
        pre, pre_g, pre_b = (h1_tm, m_tm, e), ln2_g[i], ln2_b[i]
    h = _final_call(alpha, *pre, pre_g.reshape(1, d), pre_b.reshape(1, d))
    return h.reshape(batch, seq, d)
```

```python
import functools

import jax
import jax.numpy as jnp
from jax import lax
from jax.experimental import pallas as pl
from jax.experimental.pallas import tpu as pltpu

F32 = jnp.float32
BF16 = jnp.bfloat16

CHUNK = 64
POOL_WINDOWS = (2, 4, 8, 16)
POOL_GW = 128
D_POOL = 512
D_CONV = 512
CONV_WIDTH = 31
N_HEADS = 8
QK_NOPE = 64
QK_ROPE = 32
V_HEAD = 64
Q_LORA = 384
KV_LORA = 256
ROPE_THETA = 10000.0
N_EXPERTS = 16
N_GROUPS = 4
EXPERTS_PER_GROUP = 4
D_EXPERT = 512
LN_EPS = 1e-5
RMS_EPS = 1e-6

LANES = 128
SUBLANES = 8
HEAD_BLOCK = LANES
ROPE_LANE0 = QK_NOPE
HALF_ROPE = QK_ROPE // 2
BF16_ROWS = 16
VT_BLOCK = V_HEAD + BF16_ROWS
LOG2_E = 1.4426950408889634

TM_PROJ = 512
TM_PREP = 512
TM_MERGE = 512
MERGE_SUB = 256
TM_FINAL = 512
TQ = 256
TK = 256
Q_SUBS = 2
ATTN_LOOKAHEAD = 3
POOL_CHUNK = 256
POOL_HALO = 16
CONV_CHUNK = 128
CONV_HALO = 32
MOE_TILE = 256
ROUTE_ROWS = 8

PAIRS = ((0, 1), (0, 2), (0, 3), (1, 2), (1, 3), (2, 3))
N_CLASSES = N_GROUPS * len(PAIRS)

VMEM_LIMIT = 56 * 1024 * 1024


def _cparams(sem):
    return pltpu.CompilerParams(dimension_semantics=sem, vmem_limit_bytes=VMEM_LIMIT)


def _layer_norm(x, g, b):
    mu = jnp.mean(x, axis=-1, keepdims=True)
    xc = x - mu
    var = jnp.mean(xc * xc, axis=-1, keepdims=True)
    return xc * lax.rsqrt(var + LN_EPS) * g + b


def _rms_norm(x, g):
    ms = jnp.mean(x * x, axis=-1, keepdims=True)
    return x * lax.rsqrt(ms + RMS_EPS) * g


def _sigmoid(x):
    return 0.5 * jnp.tanh(0.5 * x) + 0.5


def _full(shape):
    n = len(shape)
    return pl.BlockSpec(shape, lambda *_: (0,) * n)


def _layer_full(stacked, layer):
    n = stacked.ndim - 1
    return pl.BlockSpec((None,) + stacked.shape[1:], lambda *_: (layer,) + (0,) * n)


OFF_CONV = D_POOL
OFF_CQKV = OFF_CONV + 2 * D_CONV
W_CQKV = Q_LORA + KV_LORA + HEAD_BLOCK
OFF_GATE = OFF_CQKV + W_CQKV
N_COLS_CHUNK = 512


def _in_proj_kernel(alpha, residual, tiles_per_seq, *refs):
    n_pre = 5 if residual else 3
    if residual:
        h1_ref, m_ref, e_ref, g_ref, b_ref = refs[:n_pre]
    else:
        x_ref, g_ref, b_ref = refs[:n_pre]
    w_ref, dw_ref, cb_ref, cg_ref, cbb_ref, pw_ref, pscale_ref = refs[n_pre:n_pre + 7]
    mla_refs = refs[n_pre + 7:-9]
    h_ref, apool_ref, aconv_ref, qt_ref, k_ref, vt_ref, gate_ref, zs, ps = refs[-9:]
    rows, d = h_ref.shape
    seq_tile = pl.program_id(0) % tiles_per_seq

    @pl.when(seq_tile == 0)
    def _():
        zs[0:CONV_HALO, :] = jnp.zeros((CONV_HALO, D_CONV), F32)
        ps[0:POOL_HALO, :] = jnp.zeros((POOL_HALO, D_POOL), F32)

    if residual:
        pre = alpha * _load_token_major(h1_ref, rows, d) + _load_token_major(m_ref, rows, d)
        pre = pre + e_ref[...].astype(F32)
    else:
        pre = x_ref[...]
    h = _layer_norm(pre, g_ref[...], b_ref[...])
    h_ref[...] = h
    x = h.astype(BF16)

    def mm(lo, hi):
        return jnp.dot(x, w_ref[:, lo:hi], preferred_element_type=F32)

    zs[CONV_HALO:CONV_HALO + rows, :] = jax.nn.sigmoid(mm(OFF_CONV + D_CONV, OFF_CQKV))
    zs[CONV_HALO:CONV_HALO + rows, :] = mm(OFF_CONV, OFF_CONV + D_CONV) * zs[CONV_HALO:CONV_HALO + rows, :]
    ps[POOL_HALO:POOL_HALO + rows, :] = mm(0, OFF_CONV)
    _mla_project(mm(OFF_CQKV, OFF_GATE), *mla_refs, qt_ref, k_ref, vt_ref)

    def conv_work(r0):
        aconv_ref[r0:r0 + CONV_CHUNK, :] = _conv_chunk(zs, r0, dw_ref, cb_ref, cg_ref, cbb_ref).astype(BF16)

    def pool_work(r0):
        _pool_chunk(ps, r0, seq_tile * rows, pw_ref, pscale_ref, apool_ref)

    work = [functools.partial(conv_work, r0) for r0 in range(0, rows, CONV_CHUNK)]
    work += [functools.partial(pool_work, r0) for r0 in range(0, rows, POOL_CHUNK)]
    n_gate_chunks = gate_ref.shape[1] // N_COLS_CHUNK
    for c in range(n_gate_chunks):
        gate_ref[:, c * N_COLS_CHUNK:(c + 1) * N_COLS_CHUNK] = mm(
            OFF_GATE + c * N_COLS_CHUNK, OFF_GATE + (c + 1) * N_COLS_CHUNK).astype(BF16)
        for item in work[len(work) * c // n_gate_chunks:len(work) * (c + 1) // n_gate_chunks]:
            item()
    zs[0:CONV_HALO, :] = zs[rows:rows + CONV_HALO, :]
    ps[0:POOL_HALO, :] = ps[rows:rows + POOL_HALO, :]


def _in_proj_call(alpha, pre, g, b, w, branch_params, mla_params, layer, seq):
    residual = len(pre) == 3
    t, d = pre[-1].shape
    n = w.shape[-1]
    n_gate = n - OFF_GATE
    nh = N_HEADS * HEAD_BLOCK
    nv = N_HEADS * VT_BLOCK
    row = lambda i: (i, 0)
    col = lambda i: (0, i)
    tmaj = pl.BlockSpec((TM_PROJ * SUBLANES, LANES), row)
    pre_specs = [tmaj, tmaj, pl.BlockSpec((TM_PROJ, d), row)] if residual else [pl.BlockSpec((TM_PROJ, d), row)]
    tabs, tabs_t, qg, kvg, wqt, wk, wvt, ones_rows = mla_params
    mla_specs = ([pl.BlockSpec((TM_PROJ, LANES), row)] * 3 + [pl.BlockSpec((LANES, TM_PROJ), col)] * 3
                 + [_full(qg.shape), _full(kvg.shape), _layer_full(wqt, layer), _layer_full(wk, layer),
                    _layer_full(wvt, layer), _full(ones_rows.shape)])
    return pl.pallas_call(
        functools.partial(_in_proj_kernel, alpha, residual, seq // TM_PROJ),
        grid=(t // TM_PROJ,),
        in_specs=pre_specs + [_full(g.shape), _full(b.shape), _layer_full(w, layer)]
        + [_full(a.shape) for a in branch_params] + mla_specs,
        out_specs=[pl.BlockSpec((TM_PROJ, d), row),
                   pl.BlockSpec((TM_PROJ, D_POOL), row), pl.BlockSpec((TM_PROJ, D_CONV), row),
                   pl.BlockSpec((nh, TM_PROJ), col), pl.BlockSpec((TM_PROJ, nh), row),
                   pl.BlockSpec((nv, TM_PROJ), col), pl.BlockSpec((TM_PROJ, n_gate), row)],
        out_shape=[jax.ShapeDtypeStruct((t, d), F32),
                   jax.ShapeDtypeStruct((t, D_POOL), BF16), jax.ShapeDtypeStruct((t, D_CONV), BF16),
                   jax.ShapeDtypeStruct((nh, t), BF16), jax.ShapeDtypeStruct((t, nh), BF16),
                   jax.ShapeDtypeStruct((nv, t), BF16), jax.ShapeDtypeStruct((t, n_gate), BF16)],
        scratch_shapes=[pltpu.VMEM((TM_PROJ + CONV_HALO, D_CONV), F32),
                        pltpu.VMEM((TM_PROJ + POOL_HALO, D_POOL), F32)],
        compiler_params=_cparams(("arbitrary",)),
        name="in_proj",
    )(*pre, g, b, w, *branch_params, *tabs, *tabs_t, qg, kvg, wqt, wk, wvt, ones_rows)


def _pool_chunk(ps, r0, seq_row0, w_ref, scale_ref, o_ref):
    t = seq_row0 + r0 + lax.broadcasted_iota(jnp.int32, (POOL_CHUNK, 1), 0)
    for g, w in enumerate(POOL_WINDOWS):
        cols = slice(g * POOL_GW, (g + 1) * POOL_GW)
        xw = ps[r0:r0 + POOL_CHUNK + POOL_HALO, cols]
        acc = xw
        k = 1
        while k < w:
            acc = acc + pltpu.roll(acc, k, axis=0)
            k *= 2
        cnt = jnp.minimum(t + 1, w).astype(F32)
        mixed = acc[POOL_HALO:] / cnt - xw[POOL_HALO:]
        y = jnp.dot(mixed.astype(BF16), w_ref[g], preferred_element_type=F32) * scale_ref[:, cols]
        o_ref[r0:r0 + POOL_CHUNK, cols] = y.astype(BF16)


def _conv_chunk(zs, r0, dw_ref, cb_ref, g_ref, b_ref):
    parts = []
    for cg in range(D_CONV // LANES):
        cols = slice(cg * LANES, (cg + 1) * LANES)
        win = zs[r0:r0 + CONV_CHUNK + CONV_HALO, cols]
        acc = jnp.zeros((CONV_CHUNK, LANES), F32) + cb_ref[:, cols]
        for sub in range(SUBLANES):
            shifted = win if sub == 0 else pltpu.roll(win, sub, axis=0)
            for a in range(CONV_HALO // SUBLANES):
                lag = SUBLANES * a + sub
                if lag >= CONV_WIDTH:
                    continue
                k = CONV_WIDTH - 1 - lag
                lo = CONV_HALO - SUBLANES * a
                acc = acc + dw_ref[k:k + 1, cols] * shifted[lo:lo + CONV_CHUNK]
        parts.append(acc)
    y = _layer_norm(jnp.concatenate(parts, axis=-1), g_ref[...], b_ref[...])
    return y * jax.nn.sigmoid(y)


def _rope_block(x, c, s1, s2):
    return x * c + pltpu.roll(x, LANES - HALF_ROPE, axis=1) * s1 + pltpu.roll(x, HALF_ROPE, axis=1) * s2


def _rope_block_t(x, c, s1, s2):
    return x * c + pltpu.roll(x, HEAD_BLOCK - HALF_ROPE, axis=0) * s1 + pltpu.roll(x, HALF_ROPE, axis=0) * s2


def _mla_project(cqkv, c_ref, s1_ref, s2_ref, ct_ref, s1t_ref, s2t_ref, qg_ref, kvg_ref,
                 wqt_ref, wk_ref, wvt_ref, ones_ref, qt_ref, k_ref, vt_ref):
    cq = _rms_norm(cqkv[:, 0:Q_LORA], qg_ref[...]).astype(BF16)
    ckv = _rms_norm(cqkv[:, Q_LORA:Q_LORA + KV_LORA], kvg_ref[...]).astype(BF16)
    kr = _rope_block(cqkv[:, Q_LORA + KV_LORA:W_CQKV], c_ref[...], s1_ref[...], s2_ref[...])
    scale = float((QK_NOPE + QK_ROPE) ** -0.5 * LOG2_E)
    nt = (((1,), (1,)), ((), ()))
    ct, s1t, s2t = ct_ref[...], s1t_ref[...], s2t_ref[...]
    for h in range(N_HEADS):
        cols = slice(h * HEAD_BLOCK, (h + 1) * HEAD_BLOCK)
        qt = lax.dot_general(wqt_ref[cols, :], cq, nt, preferred_element_type=F32) * scale
        qt_ref[cols, :] = _rope_block_t(qt, ct, s1t, s2t).astype(BF16)
        k = jnp.dot(ckv, wk_ref[:, cols], preferred_element_type=F32) + kr
        k_ref[:, cols] = k.astype(BF16)
    vt = lax.dot_general(wvt_ref[...], ckv, nt, preferred_element_type=F32) + ones_ref[...]
    vt_ref[...] = vt.astype(BF16)


def _attn_kernel(qt_ref, k_ref, vt_ref, o_ref, m_ref, acc_ref):
    j = pl.program_id(1)
    key_chunk = lax.broadcasted_iota(jnp.int32, (TK, TQ), 0) // CHUNK
    qry_chunk = lax.broadcasted_iota(jnp.int32, (TK, TQ), 1) // CHUNK
    diag_mask = key_chunk <= qry_chunk

    def scores(h, sub, kt):
        k0 = pl.multiple_of(kt * TK, TK)
        kk = k_ref[pl.ds(k0, TK), h * HEAD_BLOCK:(h + 1) * HEAD_BLOCK]
        return jnp.dot(kk, qt_ref[h * HEAD_BLOCK:(h + 1) * HEAD_BLOCK, sub * TQ:(sub + 1) * TQ],
                       preferred_element_type=F32)

    def values_t(h, kt):
        k0 = pl.multiple_of(kt * TK, TK)
        return vt_ref[h * VT_BLOCK:(h + 1) * VT_BLOCK, pl.ds(k0, TK)]

    def first_update(h, sub, kt, masked, s):
        st = h * Q_SUBS + sub
        if masked:
            s = jnp.where(diag_mask, s, -jnp.inf)
        m = jnp.max(s, axis=0, keepdims=True)
        p = jnp.exp2(s - m)
        m_ref[st] = jnp.broadcast_to(m, (SUBLANES, TQ))
        acc_ref[st] = jnp.dot(values_t(h, kt), p.astype(BF16), preferred_element_type=F32)

    def update(h, sub, kt, masked, s):
        st = h * Q_SUBS + sub
        if masked:
            s = jnp.where(diag_mask, s, -jnp.inf)
        m_old = m_ref[st]
        m_new = jnp.maximum(m_old, jnp.max(s, axis=0, keepdims=True))
        alpha = jnp.exp2(m_old - m_new)
        p = jnp.exp2(s - m_new[0:1, :])
        m_ref[st] = m_new
        acc_ref[st] = alpha[0:1, :] * acc_ref[st] + jnp.dot(values_t(h, kt), p.astype(BF16),
                                                           preferred_element_type=F32)

    def run(work):
        pending = {}
        for idx in range(len(work) + ATTN_LOOKAHEAD):
            if idx < len(work):
                h, sub, kt, _, _ = work[idx]
                pending[idx] = scores(h, sub, kt)
            if idx >= ATTN_LOOKAHEAD:
                h, sub, kt, masked, upd = work[idx - ATTN_LOOKAHEAD]
                upd(h, sub, kt, masked, pending.pop(idx - ATTN_LOOKAHEAD))

    tail = []
    for h in range(N_HEADS):
        for sub in range(Q_SUBS):
            for kk in range(sub + 1):
                tail.append((h, sub, Q_SUBS * j + kk, kk == sub, first_update if kk == 0 else update))
    run(tail)

    def step(kt, carry):
        run([(h, sub, kt, False, update) for h in range(N_HEADS) for sub in range(Q_SUBS)])
        return carry

    lax.fori_loop(0, Q_SUBS * j, step, 0)

    for sub in range(Q_SUBS):
        outs = []
        for h in range(N_HEADS):
            st = h * Q_SUBS + sub
            outs.append(acc_ref[st, 0:V_HEAD, :] * (1.0 / acc_ref[st, V_HEAD:V_HEAD + 1, :]))
        o_ref[sub * TQ:(sub + 1) * TQ, :] = jnp.concatenate(outs, axis=0).T.astype(BF16)


def _attn_call(qt, k, vt, batch, seq):
    n = N_HEADS * HEAD_BLOCK
    tq = Q_SUBS * TQ
    nq = seq // tq
    return pl.pallas_call(
        _attn_kernel,
        grid=(batch, nq),
        in_specs=[pl.BlockSpec((n, tq), lambda b, i: (0, b * nq + i)),
                  pl.BlockSpec((seq, n), lambda b, i: (b, 0)),
                  pl.BlockSpec((N_HEADS * VT_BLOCK, seq), lambda b, i: (0, b))],
        out_specs=pl.BlockSpec((tq, N_HEADS * V_HEAD), lambda b, i: (b * nq + i, 0)),
        out_shape=jax.ShapeDtypeStruct((k.shape[0], N_HEADS * V_HEAD), BF16),
        scratch_shapes=[pltpu.VMEM((N_HEADS * Q_SUBS, SUBLANES, TQ), F32),
                        pltpu.VMEM((N_HEADS * Q_SUBS, VT_BLOCK, TQ), F32)],
        compiler_params=_cparams(("arbitrary", "arbitrary")),
        name="attn",
    )(qt, k, vt)


def _store_token_major(ref, x):
    rows = x.shape[0]
    for j in range(x.shape[1] // LANES):
        ref[pl.ds(j, rows, stride=SUBLANES), :] = x[:, j * LANES:(j + 1) * LANES]


def _load_token_major(ref, rows, d):
    return jnp.concatenate([ref[pl.ds(j, rows, stride=SUBLANES), :] for j in range(d // LANES)], axis=-1)


def _merge_kernel(alpha, ap_ref, ac_ref, at_ref, gl_ref, h_ref, p_ref, pp_ref, cp_ref, mp_ref, bg_ref,
                  wo_ref, g_ref, b_ref, pproj_ref, pgate_ref, wrh_ref, wrl_ref,
                  h1_ref, e_ref, lg_ref):
    d = h_ref.shape[1]
    sub = MERGE_SUB
    parts = [pl.ds(k * sub, sub) for k in range(h_ref.shape[0] // sub)]
    nt = (((1,), (1,)), ((), ()))

    def merged_of(rows):
        merged = None
        for br, (a_ref, w_ref) in enumerate(((ap_ref, pp_ref), (ac_ref, cp_ref), (at_ref, mp_ref))):
            y = jnp.dot(a_ref[rows, :], w_ref[...], preferred_element_type=F32)
            gate = _sigmoid(gl_ref[rows, br * d:(br + 1) * d].astype(F32) + bg_ref[br:br + 1, :])
            merged = gate * y if merged is None else merged + gate * y
        return merged.astype(BF16)

    merged = [merged_of(rows) for rows in parts]
    ys = [jnp.dot(m, wo_ref[...], preferred_element_type=F32) for m in merged]
    es = [jnp.dot(p_ref[rows, :].astype(BF16), pproj_ref[...], preferred_element_type=F32) for rows in parts]
    h1s = [_layer_norm(alpha * h_ref[rows, :] + y, g_ref[...], b_ref[...]) for rows, y in zip(parts, ys)]
    for k, (rows, h1, e) in enumerate(zip(parts, h1s, es)):
        h1b = h1.astype(BF16)
        e = e * _sigmoid(jnp.dot(h1b, pgate_ref[...], preferred_element_type=F32))
        e_ref[rows, :] = e.astype(BF16)
        h1l = (h1 - h1b.astype(F32)).astype(BF16)
        lg = lax.dot_general(wrh_ref[...], h1b, nt, preferred_element_type=F32)
        lg = lg + lax.dot_general(wrh_ref[...], h1l, nt, preferred_element_type=F32)
        lg = lg + lax.dot_general(wrl_ref[...], h1b, nt, preferred_element_type=F32)
        lg_ref[:, k * sub:(k + 1) * sub] = lg
        _store_token_major(h1_ref.at[pl.ds(k * sub * SUBLANES, sub * SUBLANES), :], h1)


def _merge_call(alpha, layer, ap, ac, at, gl, h, p, pp, cp, mp, bg, wo, g, b, pproj, pgate, wrh, wrl):
    t, d = h.shape
    tm = TM_MERGE
    row = lambda i: (i, 0)
    ins = [ap, ac, at, gl, h]
    in_specs = [pl.BlockSpec((tm, a.shape[1]), row) for a in ins]
    in_specs.append(pl.BlockSpec((None, tm, p.shape[-1]), lambda i: (layer, i, 0)))
    stacked = {id(a) for a in (pp, cp, mp, wo, pproj, pgate)}
    consts = [pp, cp, mp, bg, wo, g, b, pproj, pgate, wrh, wrl]
    in_specs += [_layer_full(a, layer) if id(a) in stacked else _full(a.shape) for a in consts]
    return pl.pallas_call(
        functools.partial(_merge_kernel, alpha),
        grid=(t // tm,),
        in_specs=in_specs,
        out_specs=[pl.BlockSpec((tm * SUBLANES, LANES), row), pl.BlockSpec((tm, d), row),
                   pl.BlockSpec((N_EXPERTS, tm), lambda i: (0, i))],
        out_shape=[jax.ShapeDtypeStruct((t * SUBLANES, LANES), F32), jax.ShapeDtypeStruct((t, d), BF16),
                   jax.ShapeDtypeStruct((N_EXPERTS, t), F32)],
        compiler_params=_cparams(("arbitrary",)),
        name="merge",
    )(*ins, p, *consts)


def _route_kernel(lg_ref, bias_ref, cls_ref, wa_ref, wb_ref):
    aff = [jax.nn.sigmoid(lg_ref[e]) for e in range(N_EXPERTS)]
    sel = [aff[e] + bias_ref[e] for e in range(N_EXPERTS)]
    n = EXPERTS_PER_GROUP

    def top2_sum(vals):
        best = None
        for a, b in PAIRS:
            s = vals[a] + vals[b]
            best = s if best is None else jnp.maximum(best, s)
        return best

    grp = jnp.zeros(aff[0].shape, jnp.int32)
    best = top2_sum(sel[0:n])
    for g in range(1, N_GROUPS):
        sc = top2_sum(sel[g * n:(g + 1) * n])
        better = sc > best
        grp = jnp.where(better, g, grp)
        best = jnp.where(better, sc, best)
    vs, afs = [], []
    for j in range(n):
        v, a = sel[j], aff[j]
        for g in range(1, N_GROUPS):
            v = jnp.where(grp == g, sel[g * n + j], v)
            a = jnp.where(grp == g, aff[g * n + j], a)
        vs.append(v)
        afs.append(a)
    first = jnp.zeros_like(grp)
    fv = vs[0]
    for j in range(1, n):
        better = vs[j] > fv
        first = jnp.where(better, j, first)
        fv = jnp.where(better, vs[j], fv)
    second = jnp.full_like(grp, -1)
    sv = jnp.full_like(fv, -jnp.inf)
    for j in range(n):
        better = (first != j) & ((second < 0) | (vs[j] > sv))
        second = jnp.where(better, j, second)
        sv = jnp.where(better, vs[j], sv)
    lo = jnp.minimum(first, second)
    hi = jnp.maximum(first, second)
    a_lo, a_hi = afs[0], afs[0]
    for j in range(1, n):
        a_lo = jnp.where(lo == j, afs[j], a_lo)
        a_hi = jnp.where(hi == j, afs[j], a_hi)
    pair = jnp.zeros_like(grp)
    for idx, (a, b) in enumerate(PAIRS):
        pair = jnp.where((lo == a) & (hi == b), idx, pair)
    tot = a_lo + a_hi
    cls_ref[...] = grp * len(PAIRS) + pair
    wa_ref[...] = a_lo / tot
    wb_ref[...] = a_hi / tot


def _route_call(lg3, bias):
    _, rows, lanes = lg3.shape
    blk = pl.BlockSpec((ROUTE_ROWS, lanes), lambda i: (i, 0))
    return pl.pallas_call(
        _route_kernel,
        grid=(rows // ROUTE_ROWS,),
        in_specs=[pl.BlockSpec((N_EXPERTS, ROUTE_ROWS, lanes), lambda i: (0, i, 0)),
                  pl.BlockSpec(memory_space=pltpu.SMEM)],
        out_specs=[blk, blk, blk],
        out_shape=[jax.ShapeDtypeStruct((rows, lanes), jnp.int32),
                   jax.ShapeDtypeStruct((rows, lanes), F32), jax.ShapeDtypeStruct((rows, lanes), F32)],
        compiler_params=_cparams(("arbitrary",)),
        name="route",
    )(lg3, bias)


GATHER_UNROLL = 8
CAST_ROWS = 128


def _moe_kernel(ta_ref, tb_ref, nused_ref, valid_ref,
                h_hbm, tok_ref, tok_next_ref, wa_ref, wb_ref, upa_ref, upb_ref, dna_ref, dnb_ref,
                m_hbm, xbuf, ybuf, w_up, w_dn, gsem, ssem):
    i = pl.program_id(0)
    n_used = nused_ref[0]
    slot = i % 2
    rows = MOE_TILE * SUBLANES

    def row_copy_in(toks, s, r):
        tok = toks[0, 0, r]
        return pltpu.make_async_copy(
            h_hbm.at[pl.ds(pl.multiple_of(tok * SUBLANES, SUBLANES), SUBLANES), :],
            xbuf.at[s, pl.ds(pl.multiple_of(r * SUBLANES, SUBLANES), SUBLANES), :],
            gsem.at[s])

    def row_copy_out(toks, s, r):
        tok = toks[0, 0, r]
        return pltpu.make_async_copy(
            ybuf.at[s, pl.ds(pl.multiple_of(r * SUBLANES, SUBLANES), SUBLANES), :],
            m_hbm.at[pl.ds(pl.multiple_of(tok * SUBLANES, SUBLANES), SUBLANES), :],
            ssem.at[s])

    def start_rows(make, toks, s, count):
        def body8(c, carry):
            for u in range(GATHER_UNROLL):
                make(toks, s, c * GATHER_UNROLL + u).start()
            return carry

        def body1(r, carry):
            make(toks, s, r).start()
            return carry

        full = count // GATHER_UNROLL
        lax.fori_loop(0, full, body8, 0)
        lax.fori_loop(full * GATHER_UNROLL, count, body1, 0)

    def wait_gather(tile, s):
        n = pl.multiple_of(valid_ref[tile] * SUBLANES, SUBLANES)
        pltpu.make_async_copy(h_hbm.at[pl.ds(0, n), :], xbuf.at[s, pl.ds(0, n), :], gsem.at[s]).wait()

    def wait_scatter(tile, s):
        n = pl.multiple_of(valid_ref[tile] * SUBLANES, SUBLANES)
        pltpu.make_async_copy(ybuf.at[s, pl.ds(0, n), :], m_hbm.at[pl.ds(0, n), :], ssem.at[s]).wait()

    @pl.when(i == 0)
    def _():
        xbuf[...] = jnp.zeros(xbuf.shape, F32)
        start_rows(row_copy_in, tok_ref, 0, valid_ref[0])

    @pl.when(i + 1 < n_used)
    def _():
        start_rows(row_copy_in, tok_next_ref, 1 - slot, valid_ref[i + 1])

    def refresh(which, ids_ref, up_ref, dn_ref):
        prev = ids_ref[jnp.maximum(i - 1, 0)]

        @pl.when((i == 0) | (ids_ref[i] != prev))
        def _():
            def cast_up(c, carry):
                r = pl.multiple_of(c * CAST_ROWS, CAST_ROWS)
                w_up[which, pl.ds(r, CAST_ROWS), :] = up_ref[0, pl.ds(r, CAST_ROWS), :].astype(BF16)
                return carry

            def cast_dn(c, carry):
                r = pl.multiple_of(c * CAST_ROWS, CAST_ROWS)
                w_dn[which, pl.ds(r, CAST_ROWS), :] = dn_ref[0, pl.ds(r, CAST_ROWS), :].astype(BF16)
                return carry

            lax.fori_loop(0, up_ref.shape[1] // CAST_ROWS, cast_up, 0)
            lax.fori_loop(0, dn_ref.shape[1] // CAST_ROWS, cast_dn, 0)

    @pl.when(i < n_used)
    def _():
        refresh(0, ta_ref, upa_ref, dna_ref)
        refresh(1, tb_ref, upb_ref, dnb_ref)
        wait_gather(i, slot)
        x = _load_token_major(xbuf.at[slot], MOE_TILE, upa_ref.shape[1]).astype(BF16)

        def ffn(which):
            gu = jnp.dot(x, w_up[which], preferred_element_type=F32)
            hid = jax.nn.silu(gu[:, :D_EXPERT]) * gu[:, D_EXPERT:]
            return jnp.dot(hid.astype(BF16), w_dn[which], preferred_element_type=F32)

        y = ffn(0) * wa_ref[...] + ffn(1) * wb_ref[...]

        @pl.when(i >= 2)
        def _():
            wait_scatter(i - 2, slot)

        _store_token_major(ybuf.at[slot], y)
        start_rows(row_copy_out, tok_ref, slot, valid_ref[i])

        @pl.when(i == n_used - 1)
        def _():
            @pl.when(i >= 1)
            def _():
                wait_scatter(i - 1, 1 - slot)
            wait_scatter(i, slot)


def _moe_call(layer, tile_a, tile_b, n_used, tile_valid, slot_tok, h1_tm, slot_wa, slot_wb, w_up, w_down,
              n_tiles, t):
    d = w_up.shape[1]
    e0 = layer * N_EXPERTS
    wspec = pl.BlockSpec((MOE_TILE, 1), lambda i, *_: (i, 0))
    tok_blk = (1, 1, MOE_TILE)
    grid_spec = pltpu.PrefetchScalarGridSpec(
        num_scalar_prefetch=4,
        grid=(n_tiles,),
        in_specs=[pl.BlockSpec(memory_space=pl.ANY),
                  pl.BlockSpec(tok_blk, lambda i, *_: (i, 0, 0), memory_space=pltpu.SMEM),
                  pl.BlockSpec(tok_blk, lambda i, *_: (jnp.minimum(i + 1, n_tiles - 1), 0, 0),
                               memory_space=pltpu.SMEM),
                  wspec, wspec,
                  pl.BlockSpec((1, d, 2 * D_EXPERT), lambda i, ta, tb, *_: (e0 + ta[i], 0, 0)),
                  pl.BlockSpec((1, d, 2 * D_EXPERT), lambda i, ta, tb, *_: (e0 + tb[i], 0, 0)),
                  pl.BlockSpec((1, D_EXPERT, d), lambda i, ta, tb, *_: (e0 + ta[i], 0, 0)),
                  pl.BlockSpec((1, D_EXPERT, d), lambda i, ta, tb, *_: (e0 + tb[i], 0, 0))],
        out_specs=pl.BlockSpec(memory_space=pl.ANY),
        scratch_shapes=[pltpu.VMEM((2, MOE_TILE * SUBLANES, LANES), F32),
                        pltpu.VMEM((2, MOE_TILE * SUBLANES, LANES), F32),
                        pltpu.VMEM((2, d, 2 * D_EXPERT), BF16), pltpu.VMEM((2, D_EXPERT, d), BF16),
                        pltpu.SemaphoreType.DMA((2,)), pltpu.SemaphoreType.DMA((2,))])
    return pl.pallas_call(
        _moe_kernel,
        grid_spec=grid_spec,
        out_shape=jax.ShapeDtypeStruct((t * SUBLANES, LANES), F32),
        compiler_params=_cparams(("arbitrary",)),
        name="moe",
    )(tile_a, tile_b, n_used, tile_valid, h1_tm, slot_tok, slot_tok, slot_wa, slot_wb, w_up, w_up, w_down, w_down)


def _final_kernel(alpha, h1_ref, m_ref, e_ref, g_ref, b_ref, o_ref):
    rows, d = o_ref.shape
    h1 = _load_token_major(h1_ref, rows, d)
    m = _load_token_major(m_ref, rows, d)
    o_ref[...] = _layer_norm(alpha * h1 + m + e_ref[...].astype(F32), g_ref[...], b_ref[...])


def _final_call(alpha, h1_tm, m_tm, e, g, b):
    t, d = e.shape
    tm = TM_FINAL
    row = lambda i: (i, 0)
    tmaj = pl.BlockSpec((tm * SUBLANES, LANES), row)
    return pl.pallas_call(
        functools.partial(_final_kernel, alpha),
        grid=(t // tm,),
        in_specs=[tmaj, tmaj, pl.BlockSpec((tm, d), row), _full(g.shape), _full(b.shape)],
        out_specs=pl.BlockSpec((tm, d), row),
        out_shape=jax.ShapeDtypeStruct((t, d), F32),
        compiler_params=_cparams(("arbitrary",)),
        name="final_ln",
    )(h1_tm, m_tm, e, g, b)


def _pack_w_in(w_in):
    off_q = OFF_CONV + 2 * D_CONV
    off_kr = off_q + Q_LORA + KV_LORA
    off_gate = off_kr + QK_ROPE
    kr = jnp.pad(w_in[..., off_kr:off_gate], ((0, 0), (0, 0), (ROPE_LANE0, HEAD_BLOCK - ROPE_LANE0 - QK_ROPE)))
    return jnp.concatenate([w_in[..., :off_kr], kr, w_in[..., off_gate:]], axis=-1).astype(BF16)


def _pack_heads(w, lo, width, stride, lane0):
    blocks = []
    for h in range(N_HEADS):
        l0 = lane0(h)
        blocks.append(jnp.pad(w[..., h * stride + lo:h * stride + lo + width],
                              ((0, 0), (0, 0), (l0, HEAD_BLOCK - l0 - width))))
    return jnp.concatenate(blocks, axis=-1).astype(BF16)


def _rope_tables(positions):
    inv_freq = jnp.power(ROPE_THETA, -jnp.arange(0, QK_ROPE, 2, dtype=F32) / QK_ROPE)
    ang = positions.astype(F32).reshape(-1, 1) * inv_freq
    cos, sin = jnp.cos(ang), jnp.sin(ang)
    t = ang.shape[0]
    ones_lo = jnp.ones((t, ROPE_LANE0), F32)
    ones_hi = jnp.ones((t, HEAD_BLOCK - ROPE_LANE0 - QK_ROPE), F32)
    zeros_lo = jnp.zeros((t, ROPE_LANE0), F32)
    zeros_half = jnp.zeros((t, HALF_ROPE), F32)
    zeros_hi = jnp.zeros((t, HEAD_BLOCK - ROPE_LANE0 - QK_ROPE), F32)
    c = jnp.concatenate([ones_lo, cos, cos, ones_hi], axis=-1)
    s1 = jnp.concatenate([zeros_lo, -sin, zeros_half, zeros_hi], axis=-1)
    s2 = jnp.concatenate([zeros_lo, zeros_half, sin, zeros_hi], axis=-1)
    return c, s1, s2


def _routing_plan(cls, wa, wb, n_tiles):
    t = cls.shape[0]
    n_pad = n_tiles * MOE_TILE - t
    assert n_pad == N_CLASSES * MOE_TILE, n_pad
    classes = jnp.arange(N_CLASSES, dtype=jnp.int32)
    counts = jnp.sum((cls[None, :] == classes[:, None]).astype(jnp.int32), axis=1)
    padded = (counts + MOE_TILE - 1) // MOE_TILE * MOE_TILE
    pad_end = jnp.cumsum(padded)
    pad_start = pad_end - padded
    pad_need = padded - counts
    j = jnp.arange(MOE_TILE, dtype=jnp.int32)[None, :]
    pad_key = jnp.where(j < pad_need[:, None], 2 * classes[:, None] + 1, 2 * N_CLASSES).reshape(-1)
    zeros_i = jnp.zeros((n_pad,), jnp.int32)
    zeros_f = jnp.zeros((n_pad,), F32)
    _, slot_tok, slot_wa, slot_wb = lax.sort(
        (jnp.concatenate([2 * cls, pad_key]), jnp.concatenate([jnp.arange(t, dtype=jnp.int32), zeros_i]),
         jnp.concatenate([wa, zeros_f]), jnp.concatenate([wb, zeros_f])), num_keys=1)
    slot_tok = slot_tok.reshape(n_tiles, 1, MOE_TILE)
    slot_wa = slot_wa.reshape(-1, 1)
    slot_wb = slot_wb.reshape(-1, 1)
    tile_row0 = jnp.arange(n_tiles, dtype=jnp.int32) * MOE_TILE
    tile_cls = jnp.minimum(jnp.sum((tile_row0[:, None] >= pad_end[None, :]).astype(jnp.int32), axis=1),
                           N_CLASSES - 1)
    onehot = (tile_cls[:, None] == classes[None, :]).astype(jnp.int32)
    tile_valid = jnp.clip(jnp.sum(onehot * (pad_start + counts)[None, :], axis=1) - tile_row0,
                          0, MOE_TILE).astype(jnp.int32)
    pair_lo = jnp.array([p[0] for p in PAIRS], jnp.int32)
    pair_hi = jnp.array([p[1] for p in PAIRS], jnp.int32)
    grp = tile_cls // len(PAIRS)
    pair_onehot = ((tile_cls % len(PAIRS))[:, None] == jnp.arange(len(PAIRS), dtype=jnp.int32)[None, :])
    tile_a = grp * EXPERTS_PER_GROUP + jnp.sum(pair_onehot * pair_lo[None, :], axis=1)
    tile_b = grp * EXPERTS_PER_GROUP + jnp.sum(pair_onehot * pair_hi[None, :], axis=1)
    n_used = (pad_end[-1] // MOE_TILE).astype(jnp.int32).reshape(1)
    return tile_a, tile_b, n_used, tile_valid, slot_tok, slot_wa, slot_wb


def kernel(x, p, positions, ln_in_g, ln_in_b, w_in, b_gate, pool_w, pool_scale, pool_proj, conv_dw, conv_b, conv_ln_g, conv_ln_b, conv_proj, q_norm_g, w_uq, kv_norm_g, w_ukv, mla_proj, w_out, ln1_g, ln1_b, w_router, router_bias, exp_w_up, exp_w_down, ple_proj, ple_gate, ln2_g, ln2_b):
    batch, seq, d = x.shape
    depth = w_in.shape[0]
    t = batch * seq
    alpha = float((2 * depth) ** 0.25)
    n_tiles = (t + N_CLASSES * (MOE_TILE - 1)) // MOE_TILE + 1

    w_in_p = _pack_w_in(w_in)
    pool_w_b = pool_w.astype(BF16)
    pool_proj_b, conv_proj_b, mla_proj_b = pool_proj.astype(BF16), conv_proj.astype(BF16), mla_proj.astype(BF16)
    qk = QK_NOPE + QK_ROPE
    wq_p = (_pack_heads(w_uq, 0, QK_NOPE, qk, lambda h: 0)
            + _pack_heads(w_uq, QK_NOPE, QK_ROPE, qk, lambda h: ROPE_LANE0))
    wqt_p = jnp.swapaxes(wq_p, 1, 2)
    wk_p = _pack_heads(w_ukv, 0, QK_NOPE, QK_NOPE + V_HEAD, lambda h: 0)
    kv_w = QK_NOPE + V_HEAD
    wvt_p = jnp.swapaxes(jnp.concatenate(
        [jnp.pad(w_ukv[..., h * kv_w + QK_NOPE:(h + 1) * kv_w], ((0, 0), (0, 0), (0, BF16_ROWS)))
         for h in range(N_HEADS)], axis=-1), 1, 2).astype(BF16)
    ones_rows = jnp.tile(jnp.concatenate([jnp.zeros((V_HEAD, 1), F32), jnp.ones((BF16_ROWS, 1), F32)]),
                         (N_HEADS, 1))
    w_out_b, ple_proj_b, ple_gate_b = w_out.astype(BF16), ple_proj.astype(BF16), ple_gate.astype(BF16)
    w_up_b = exp_w_up.reshape((depth * N_EXPERTS,) + exp_w_up.shape[2:])
    w_down_b = exp_w_down.reshape((depth * N_EXPERTS,) + exp_w_down.shape[2:])
    wr_t = w_router.T
    wr_hi = wr_t.astype(BF16)
    wr_lo = (wr_t - wr_hi.astype(F32)).astype(BF16)
    rope_tabs = _rope_tables(positions)
    rope_tabs_t = tuple(a.T for a in rope_tabs)
    p2 = p.reshape(depth, t, -1)

    pre, pre_g, pre_b = (x.reshape(t, d),), ln_in_g, ln_in_b
    for i in range(depth):
        branch_params = (conv_dw[i], conv_b[i].reshape(1, -1), conv_ln_g[i].reshape(1, -1),
                         conv_ln_b[i].reshape(1, -1), pool_w_b[i], pool_scale[i].reshape(1, -1))
        mla_params = (rope_tabs, rope_tabs_t, q_norm_g[i].reshape(1, -1), kv_norm_g[i].reshape(1, -1),
                      wqt_p, wk_p, wvt_p, ones_rows)
        h, a_pool, a_conv, qt, k, vt, glog = _in_proj_call(alpha, pre, pre_g.reshape(1, d), pre_b.reshape(1, d),
                                                           w_in_p, branch_params, mla_params, i, seq)
        a_attn = _attn_call(qt, k, vt, batch, seq)
        h1_tm, e, logits = _merge_call(alpha, i, a_pool, a_conv, a_attn, glog, h, p2,
                                       pool_proj_b, conv_proj_b, mla_proj_b, b_gate[i], w_out_b,
                                       ln1_g[i].reshape(1, d), ln1_b[i].reshape(1, d),
                                       ple_proj_b, ple_gate_b, wr_hi, wr_lo)
        cls, wa, wb = _route_call(logits.reshape(N_EXPERTS, t // LANES, LANES), router_bias)
        plan = _routing_plan(cls.reshape(t), wa.reshape(t), wb.reshape(t), n_tiles)
        m_tm = _moe_call(i, *plan[:5], h1_tm, plan[5], plan[6], w_up_b, w_down_b, n_tiles, t)
        pre, pre_g, pre_b = (h1_tm, m_tm, e), ln2_g[i], ln2_b[i]
    h = _final_call(alpha, *pre, pre_g.reshape(1, d), pre_b.reshape(1, d))
    return h.reshape(batch, seq, d)
```

```python
import functools

import jax
import jax.numpy as jnp
from jax import lax
from jax.experimental import pallas as pl
from jax.experimental.pallas import tpu as pltpu

F32 = jnp.float32
BF16 = jnp.bfloat16

CHUNK = 64
POOL_WINDOWS = (2, 4, 8, 16)
POOL_GW = 128
D_POOL = 512
D_CONV = 512
CONV_WIDTH = 31
N_HEADS = 8
QK_NOPE = 64
QK_ROPE = 32
V_HEAD = 64
Q_LORA = 384
KV_LORA = 256
ROPE_THETA = 10000.0
N_EXPERTS = 16
N_GROUPS = 4
EXPERTS_PER_GROUP = 4
D_EXPERT = 512
LN_EPS = 1e-5
RMS_EPS = 1e-6

LANES = 128
SUBLANES = 8
HEAD_BLOCK = LANES
ROPE_LANE0 = QK_NOPE
HALF_ROPE = QK_ROPE // 2
BF16_ROWS = 16
VT_BLOCK = V_HEAD + BF16_ROWS
LOG2_E = 1.4426950408889634

TM_PROJ = 512
TM_PREP = 512
TM_MERGE = 512
MERGE_SUB = 256
TM_FINAL = 512
TQ = 256
TK = 256
Q_SUBS = 2
ATTN_LOOKAHEAD = 3
POOL_CHUNK = 256
POOL_HALO = 16
CONV_CHUNK = 128
CONV_HALO = 32
MOE_TILE = 256
ROUTE_ROWS = 8

PAIRS = ((0, 1), (0, 2), (0, 3), (1, 2), (1, 3), (2, 3))
N_CLASSES = N_GROUPS * len(PAIRS)

VMEM_LIMIT = 56 * 1024 * 1024


def _cparams(sem):
    return pltpu.CompilerParams(dimension_semantics=sem, vmem_limit_bytes=VMEM_LIMIT)


def _layer_norm(x, g, b):
    mu = jnp.mean(x, axis=-1, keepdims=True)
    xc = x - mu
    var = jnp.mean(xc * xc, axis=-1, keepdims=True)
    return xc * lax.rsqrt(var + LN_EPS) * g + b


def _rms_norm(x, g):
    ms = jnp.mean(x * x, axis=-1, keepdims=True)
    return x * lax.rsqrt(ms + RMS_EPS) * g


def _sigmoid(x):
    return 0.5 * jnp.tanh(0.5 * x) + 0.5


def _full(shape):
    n = len(shape)
    return pl.BlockSpec(shape, lambda *_: (0,) * n)


def _layer_full(stacked, layer):
    n = stacked.ndim - 1
    return pl.BlockSpec((None,) + stacked.shape[1:], lambda *_: (layer,) + (0,) * n)


OFF_CONV = D_POOL
OFF_CQKV = OFF_CONV + 2 * D_CONV
W_CQKV = Q_LORA + KV_LORA + HEAD_BLOCK
OFF_GATE = OFF_CQKV + W_CQKV
N_COLS_CHUNK = 512


def _in_proj_kernel(alpha, residual, tiles_per_seq, *refs):
    n_pre = 5 if residual else 3
    if residual:
        h1_ref, m_ref, e_ref, g_ref, b_ref = refs[:n_pre]
    else:
        x_ref, g_ref, b_ref = refs[:n_pre]
    w_ref, dw_ref, cb_ref, cg_ref, cbb_ref, pw_ref, pscale_ref = refs[n_pre:n_pre + 7]
    mla_refs = refs[n_pre + 7:-9]
    h_ref, apool_ref, aconv_ref, qt_ref, k_ref, vt_ref, gate_ref, zs, ps = refs[-9:]
    rows, d = h_ref.shape
    seq_tile = pl.program_id(0) % tiles_per_seq

    @pl.when(seq_tile == 0)
    def _():
        zs[0:CONV_HALO, :] = jnp.zeros((CONV_HALO, D_CONV), F32)
        ps[0:POOL_HALO, :] = jnp.zeros((POOL_HALO, D_POOL), F32)

    half = rows // 2
    xs = []
    for k in range(2):
        r0 = k * half
        if residual:
            pre = (alpha * _load_token_major(h1_ref.at[pl.ds(r0 * SUBLANES, half * SUBLANES), :], half, d)
                   + _load_token_major(m_ref.at[pl.ds(r0 * SUBLANES, half * SUBLANES), :], half, d))
            pre = pre + e_ref[r0:r0 + half, :].astype(F32)
        else:
            pre = x_ref[r0:r0 + half, :]
        h = _layer_norm(pre, g_ref[...], b_ref[...])
        h_ref[r0:r0 + half, :] = h
        xs.append(h.astype(BF16))

    def mm(lo, hi):
        return jnp.concatenate([jnp.dot(xk, w_ref[:, lo:hi], preferred_element_type=F32) for xk in xs], axis=0)

    zs[CONV_HALO:CONV_HALO + rows, :] = jax.nn.sigmoid(mm(OFF_CONV + D_CONV, OFF_CQKV))
    zs[CONV_HALO:CONV_HALO + rows, :] = mm(OFF_CONV, OFF_CONV + D_CONV) * zs[CONV_HALO:CONV_HALO + rows, :]
    ps[POOL_HALO:POOL_HALO + rows, :] = mm(0, OFF_CONV)
    cqkv = mm(OFF_CQKV, OFF_GATE)

    def mla_work():
        _mla_project(cqkv, *mla_refs, qt_ref, k_ref, vt_ref)

    def conv_work(r0):
        aconv_ref[r0:r0 + CONV_CHUNK, :] = _conv_chunk(zs, r0, dw_ref, cb_ref, cg_ref, cbb_ref).astype(BF16)

    def pool_work(r0):
        _pool_chunk(ps, r0, seq_tile * rows, pw_ref, pscale_ref, apool_ref)

    work = [mla_work] + [functools.partial(conv_work, r0) for r0 in range(0, rows, CONV_CHUNK)]
    work += [functools.partial(pool_work, r0) for r0 in range(0, rows, POOL_CHUNK)]
    n_gate_chunks = gate_ref.shape[1] // N_COLS_CHUNK
    for c in range(n_gate_chunks):
        gate_ref[:, c * N_COLS_CHUNK:(c + 1) * N_COLS_CHUNK] = mm(
            OFF_GATE + c * N_COLS_CHUNK, OFF_GATE + (c + 1) * N_COLS_CHUNK).astype(BF16)
        for item in work[len(work) * c // n_gate_chunks:len(work) * (c + 1) // n_gate_chunks]:
            item()
    zs[0:CONV_HALO, :] = zs[rows:rows + CONV_HALO, :]
    ps[0:POOL_HALO, :] = ps[rows:rows + POOL_HALO, :]


def _in_proj_call(alpha, pre, g, b, w, branch_params, mla_params, layer, seq):
    residual = len(pre) == 3
    t, d = pre[-1].shape
    n = w.shape[-1]
    n_gate = n - OFF_GATE
    nh = N_HEADS * HEAD_BLOCK
    nv = N_HEADS * VT_BLOCK
    row = lambda i: (i, 0)
    col = lambda i: (0, i)
    tmaj = pl.BlockSpec((TM_PROJ * SUBLANES, LANES), row)
    pre_specs = [tmaj, tmaj, pl.BlockSpec((TM_PROJ, d), row)] if residual else [pl.BlockSpec((TM_PROJ, d), row)]
    tabs, tabs_t, qg, kvg, wqt, wk, wvt, ones_rows = mla_params
    mla_specs = ([pl.BlockSpec((TM_PROJ, LANES), row)] * 3 + [pl.BlockSpec((LANES, TM_PROJ), col)] * 3
                 + [_full(qg.shape), _full(kvg.shape), _layer_full(wqt, layer), _layer_full(wk, layer),
                    _layer_full(wvt, layer), _full(ones_rows.shape)])
    return pl.pallas_call(
        functools.partial(_in_proj_kernel, alpha, residual, seq // TM_PROJ),
        grid=(t // TM_PROJ,),
        in_specs=pre_specs + [_full(g.shape), _full(b.shape), _layer_full(w, layer)]
        + [_full(a.shape) for a in branch_params] + mla_specs,
        out_specs=[pl.BlockSpec((TM_PROJ, d), row),
                   pl.BlockSpec((TM_PROJ, D_POOL), row), pl.BlockSpec((TM_PROJ, D_CONV), row),
                   pl.BlockSpec((nh, TM_PROJ), col), pl.BlockSpec((TM_PROJ, nh), row),
                   pl.BlockSpec((nv, TM_PROJ), col), pl.BlockSpec((TM_PROJ, n_gate), row)],
        out_shape=[jax.ShapeDtypeStruct((t, d), F32),
                   jax.ShapeDtypeStruct((t, D_POOL), BF16), jax.ShapeDtypeStruct((t, D_CONV), BF16),
                   jax.ShapeDtypeStruct((nh, t), BF16), jax.ShapeDtypeStruct((t, nh), BF16),
                   jax.ShapeDtypeStruct((nv, t), BF16), jax.ShapeDtypeStruct((t, n_gate), BF16)],
        scratch_shapes=[pltpu.VMEM((TM_PROJ + CONV_HALO, D_CONV), F32),
                        pltpu.VMEM((TM_PROJ + POOL_HALO, D_POOL), F32)],
        compiler_params=_cparams(("arbitrary",)),
        name="in_proj",
    )(*pre, g, b, w, *branch_params, *tabs, *tabs_t, qg, kvg, wqt, wk, wvt, ones_rows)


def _pool_chunk(ps, r0, seq_row0, w_ref, scale_ref, o_ref):
    t = seq_row0 + r0 + lax.broadcasted_iota(jnp.int32, (POOL_CHUNK, 1), 0)
    for g, w in enumerate(POOL_WINDOWS):
        cols = slice(g * POOL_GW, (g + 1) * POOL_GW)
        xw = ps[r0:r0 + POOL_CHUNK + POOL_HALO, cols]
        acc = xw
        k = 1
        while k < w:
            acc = acc + pltpu.roll(acc, k, axis=0)
            k *= 2
        cnt = jnp.minimum(t + 1, w).astype(F32)
        mixed = acc[POOL_HALO:] / cnt - xw[POOL_HALO:]
        y = jnp.dot(mixed.astype(BF16), w_ref[g], preferred_element_type=F32) * scale_ref[:, cols]
        o_ref[r0:r0 + POOL_CHUNK, cols] = y.astype(BF16)


def _conv_chunk(zs, r0, dw_ref, cb_ref, g_ref, b_ref):
    parts = []
    for cg in range(D_CONV // LANES):
        cols = slice(cg * LANES, (cg + 1) * LANES)
        win = zs[r0:r0 + CONV_CHUNK + CONV_HALO, cols]
        acc = jnp.zeros((CONV_CHUNK, LANES), F32) + cb_ref[:, cols]
        for sub in range(SUBLANES):
            shifted = win if sub == 0 else pltpu.roll(win, sub, axis=0)
            for a in range(CONV_HALO // SUBLANES):
                lag = SUBLANES * a + sub
                if lag >= CONV_WIDTH:
                    continue
                k = CONV_WIDTH - 1 - lag
                lo = CONV_HALO - SUBLANES * a
                acc = acc + dw_ref[k:k + 1, cols] * shifted[lo:lo + CONV_CHUNK]
        parts.append(acc)
    y = _layer_norm(jnp.concatenate(parts, axis=-1), g_ref[...], b_ref[...])
    return y * jax.nn.sigmoid(y)


def _rope_block(x, c, s1, s2):
    return x * c + pltpu.roll(x, LANES - HALF_ROPE, axis=1) * s1 + pltpu.roll(x, HALF_ROPE, axis=1) * s2


def _rope_block_t(x, c, s1, s2):
    return x * c + pltpu.roll(x, HEAD_BLOCK - HALF_ROPE, axis=0) * s1 + pltpu.roll(x, HALF_ROPE, axis=0) * s2


def _mla_project(cqkv, c_ref, s1_ref, s2_ref, ct_ref, s1t_ref, s2t_ref, qg_ref, kvg_ref,
                 wqt_ref, wk_ref, wvt_ref, ones_ref, qt_ref, k_ref, vt_ref):
    cq = _rms_norm(cqkv[:, 0:Q_LORA], qg_ref[...]).astype(BF16)
    ckv = _rms_norm(cqkv[:, Q_LORA:Q_LORA + KV_LORA], kvg_ref[...]).astype(BF16)
    kr = _rope_block(cqkv[:, Q_LORA + KV_LORA:W_CQKV], c_ref[...], s1_ref[...], s2_ref[...])
    scale = float((QK_NOPE + QK_ROPE) ** -0.5 * LOG2_E)
    nt = (((1,), (1,)), ((), ()))
    ct, s1t, s2t = ct_ref[...], s1t_ref[...], s2t_ref[...]
    for h in range(N_HEADS):
        cols = slice(h * HEAD_BLOCK, (h + 1) * HEAD_BLOCK)
        qt = lax.dot_general(wqt_ref[cols, :], cq, nt, preferred_element_type=F32) * scale
        qt_ref[cols, :] = _rope_block_t(qt, ct, s1t, s2t).astype(BF16)
        k = jnp.dot(ckv, wk_ref[:, cols], preferred_element_type=F32) + kr
        k_ref[:, cols] = k.astype(BF16)
    vt = lax.dot_general(wvt_ref[...], ckv, nt, preferred_element_type=F32) + ones_ref[...]
    vt_ref[...] = vt.astype(BF16)


def _attn_kernel(qt_ref, k_ref, vt_ref, o_ref, m_ref, acc_ref):
    j = pl.program_id(1)
    key_chunk = lax.broadcasted_iota(jnp.int32, (TK, TQ), 0) // CHUNK
    qry_chunk = lax.broadcasted_iota(jnp.int32, (TK, TQ), 1) // CHUNK
    diag_mask = key_chunk <= qry_chunk

    def scores(h, sub, kt):
        k0 = pl.multiple_of(kt * TK, TK)
        kk = k_ref[pl.ds(k0, TK), h * HEAD_BLOCK:(h + 1) * HEAD_BLOCK]
        return jnp.dot(kk, qt_ref[h * HEAD_BLOCK:(h + 1) * HEAD_BLOCK, sub * TQ:(sub + 1) * TQ],
                       preferred_element_type=F32)

    def values_t(h, kt):
        k0 = pl.multiple_of(kt * TK, TK)
        return vt_ref[h * VT_BLOCK:(h + 1) * VT_BLOCK, pl.ds(k0, TK)]

    def first_update(h, sub, kt, masked, s):
        st = h * Q_SUBS + sub
        if masked:
            s = jnp.where(diag_mask, s, -jnp.inf)
        m = jnp.max(s, axis=0, keepdims=True)
        p = jnp.exp2(s - m)
        m_ref[st] = jnp.broadcast_to(m, (SUBLANES, TQ))
        acc_ref[st] = jnp.dot(values_t(h, kt), p.astype(BF16), preferred_element_type=F32)

    def update(h, sub, kt, masked, s):
        st = h * Q_SUBS + sub
        if masked:
            s = jnp.where(diag_mask, s, -jnp.inf)
        m_old = m_ref[st]
        m_new = jnp.maximum(m_old, jnp.max(s, axis=0, keepdims=True))
        alpha = jnp.exp2(m_old - m_new)
        p = jnp.exp2(s - m_new[0:1, :])
        m_ref[st] = m_new
        acc_ref[st] = alpha[0:1, :] * acc_ref[st] + jnp.dot(values_t(h, kt), p.astype(BF16),
                                                           preferred_element_type=F32)

    def run(work):
        pending = {}
        for idx in range(len(work) + ATTN_LOOKAHEAD):
            if idx < len(work):
                h, sub, kt, _, _ = work[idx]
                pending[idx] = scores(h, sub, kt)
            if idx >= ATTN_LOOKAHEAD:
                h, sub, kt, masked, upd = work[idx - ATTN_LOOKAHEAD]
                upd(h, sub, kt, masked, pending.pop(idx - ATTN_LOOKAHEAD))

    tail = []
    for h in range(N_HEADS):
        for sub in range(Q_SUBS):
            for kk in range(sub + 1):
                tail.append((h, sub, Q_SUBS * j + kk, kk == sub, first_update if kk == 0 else update))
    run(tail)

    def step(kt, carry):
        run([(h, sub, kt, False, update) for h in range(N_HEADS) for sub in range(Q_SUBS)])
        return carry

    lax.fori_loop(0, Q_SUBS * j, step, 0)

    for sub in range(Q_SUBS):
        outs = []
        for h in range(N_HEADS):
            st = h * Q_SUBS + sub
            outs.append(acc_ref[st, 0:V_HEAD, :] * (1.0 / acc_ref[st, V_HEAD:V_HEAD + 1, :]))
        o_ref[sub * TQ:(sub + 1) * TQ, :] = jnp.concatenate(outs, axis=0).T.astype(BF16)


def _attn_call(qt, k, vt, batch, seq):
    n = N_HEADS * HEAD_BLOCK
    tq = Q_SUBS * TQ
    nq = seq // tq
    return pl.pallas_call(
        _attn_kernel,
        grid=(batch, nq),
        in_specs=[pl.BlockSpec((n, tq), lambda b, i: (0, b * nq + i)),
                  pl.BlockSpec((seq, n), lambda b, i: (b, 0)),
                  pl.BlockSpec((N_HEADS * VT_BLOCK, seq), lambda b, i: (0, b))],
        out_specs=pl.BlockSpec((tq, N_HEADS * V_HEAD), lambda b, i: (b * nq + i, 0)),
        out_shape=jax.ShapeDtypeStruct((k.shape[0], N_HEADS * V_HEAD), BF16),
        scratch_shapes=[pltpu.VMEM((N_HEADS * Q_SUBS, SUBLANES, TQ), F32),
                        pltpu.VMEM((N_HEADS * Q_SUBS, VT_BLOCK, TQ), F32)],
        compiler_params=_cparams(("arbitrary", "arbitrary")),
        name="attn",
    )(qt, k, vt)


def _store_token_major(ref, x):
    rows = x.shape[0]
    for j in range(x.shape[1] // LANES):
        ref[pl.ds(j, rows, stride=SUBLANES), :] = x[:, j * LANES:(j + 1) * LANES]


def _load_token_major(ref, rows, d):
    return jnp.concatenate([ref[pl.ds(j, rows, stride=SUBLANES), :] for j in range(d // LANES)], axis=-1)


def _merge_kernel(alpha, ap_ref, ac_ref, at_ref, gl_ref, h_ref, p_ref, pp_ref, cp_ref, mp_ref, bg_ref,
                  wo_ref, g_ref, b_ref, pproj_ref, pgate_ref, wrh_ref, wrl_ref,
                  h1_ref, e_ref, lg_ref):
    d = h_ref.shape[1]
    sub = MERGE_SUB
    parts = [pl.ds(k * sub, sub) for k in range(h_ref.shape[0] // sub)]
    nt = (((1,), (1,)), ((), ()))

    def merged_of(rows):
        merged = None
        for br, (a_ref, w_ref) in enumerate(((ap_ref, pp_ref), (ac_ref, cp_ref), (at_ref, mp_ref))):
            y = jnp.dot(a_ref[rows, :], w_ref[...], preferred_element_type=F32)
            gate = _sigmoid(gl_ref[rows, br * d:(br + 1) * d].astype(F32) + bg_ref[br:br + 1, :])
            merged = gate * y if merged is None else merged + gate * y
        return merged.astype(BF16)

    merged = [merged_of(rows) for rows in parts]
    ys = [jnp.dot(m, wo_ref[...], preferred_element_type=F32) for m in merged]
    es = [jnp.dot(p_ref[rows, :].astype(BF16), pproj_ref[...], preferred_element_type=F32) for rows in parts]
    h1s = [_layer_norm(alpha * h_ref[rows, :] + y, g_ref[...], b_ref[...]) for rows, y in zip(parts, ys)]
    for k, (rows, h1, e) in enumerate(zip(parts, h1s, es)):
        h1b = h1.astype(BF16)
        e = e * _sigmoid(jnp.dot(h1b, pgate_ref[...], preferred_element_type=F32))
        e_ref[rows, :] = e.astype(BF16)
        h1l = (h1 - h1b.astype(F32)).astype(BF16)
        lg = lax.dot_general(wrh_ref[...], h1b, nt, preferred_element_type=F32)
        lg = lg + lax.dot_general(wrh_ref[...], h1l, nt, preferred_element_type=F32)
        lg = lg + lax.dot_general(wrl_ref[...], h1b, nt, preferred_element_type=F32)
        lg_ref[:, k * sub:(k + 1) * sub] = lg
        _store_token_major(h1_ref.at[pl.ds(k * sub * SUBLANES, sub * SUBLANES), :], h1)


def _merge_call(alpha, layer, ap, ac, at, gl, h, p, pp, cp, mp, bg, wo, g, b, pproj, pgate, wrh, wrl):
    t, d = h.shape
    tm = TM_MERGE
    row = lambda i: (i, 0)
    ins = [ap, ac, at, gl, h]
    in_specs = [pl.BlockSpec((tm, a.shape[1]), row) for a in ins]
    in_specs.append(pl.BlockSpec((None, tm, p.shape[-1]), lambda i: (layer, i, 0)))
    stacked = {id(a) for a in (pp, cp, mp, wo, pproj, pgate)}
    consts = [pp, cp, mp, bg, wo, g, b, pproj, pgate, wrh, wrl]
    in_specs += [_layer_full(a, layer) if id(a) in stacked else _full(a.shape) for a in consts]
    return pl.pallas_call(
        functools.partial(_merge_kernel, alpha),
        grid=(t // tm,),
        in_specs=in_specs,
        out_specs=[pl.BlockSpec((tm * SUBLANES, LANES), row), pl.BlockSpec((tm, d), row),
                   pl.BlockSpec((N_EXPERTS, tm), lambda i: (0, i))],
        out_shape=[jax.ShapeDtypeStruct((t * SUBLANES, LANES), F32), jax.ShapeDtypeStruct((t, d), BF16),
                   jax.ShapeDtypeStruct((N_EXPERTS, t), F32)],
        compiler_params=_cparams(("arbitrary",)),
        name="merge",
    )(*ins, p, *consts)


def _route_kernel(lg_ref, bias_ref, cls_ref, wa_ref, wb_ref):
    aff = [jax.nn.sigmoid(lg_ref[e]) for e in range(N_EXPERTS)]
    sel = [aff[e] + bias_ref[e] for e in range(N_EXPERTS)]
    n = EXPERTS_PER_GROUP

    def top2_sum(vals):
        best = None
        for a, b in PAIRS:
            s = vals[a] + vals[b]
            best = s if best is None else jnp.maximum(best, s)
        return best

    grp = jnp.zeros(aff[0].shape, jnp.int32)
    best = top2_sum(sel[0:n])
    for g in range(1, N_GROUPS):
        sc = top2_sum(sel[g * n:(g + 1) * n])
        better = sc > best
        grp = jnp.where(better, g, grp)
        best = jnp.where(better, sc, best)
    vs, afs = [], []
    for j in range(n):
        v, a = sel[j], aff[j]
        for g in range(1, N_GROUPS):
            v = jnp.where(grp == g, sel[g * n + j], v)
            a = jnp.where(grp == g, aff[g * n + j], a)
        vs.append(v)
        afs.append(a)
    first = jnp.zeros_like(grp)
    fv = vs[0]
    for j in range(1, n):
        better = vs[j] > fv
        first = jnp.where(better, j, first)
        fv = jnp.where(better, vs[j], fv)
    second = jnp.full_like(grp, -1)
    sv = jnp.full_like(fv, -jnp.inf)
    for j in range(n):
        better = (first != j) & ((second < 0) | (vs[j] > sv))
        second = jnp.where(better, j, second)
        sv = jnp.where(better, vs[j], sv)
    lo = jnp.minimum(first, second)
    hi = jnp.maximum(first, second)
    a_lo, a_hi = afs[0], afs[0]
    for j in range(1, n):
        a_lo = jnp.where(lo == j, afs[j], a_lo)
        a_hi = jnp.where(hi == j, afs[j], a_hi)
    pair = jnp.zeros_like(grp)
    for idx, (a, b) in enumerate(PAIRS):
        pair = jnp.where((lo == a) & (hi == b), idx, pair)
    tot = a_lo + a_hi
    cls_ref[...] = grp * len(PAIRS) + pair
    wa_ref[...] = a_lo / tot
    wb_ref[...] = a_hi / tot


def _route_call(lg3, bias):
    _, rows, lanes = lg3.shape
    blk = pl.BlockSpec((ROUTE_ROWS, lanes), lambda i: (i, 0))
    return pl.pallas_call(
        _route_kernel,
        grid=(rows // ROUTE_ROWS,),
        in_specs=[pl.BlockSpec((N_EXPERTS, ROUTE_ROWS, lanes), lambda i: (0, i, 0)),
                  pl.BlockSpec(memory_space=pltpu.SMEM)],
        out_specs=[blk, blk, blk],
        out_shape=[jax.ShapeDtypeStruct((rows, lanes), jnp.int32),
                   jax.ShapeDtypeStruct((rows, lanes), F32), jax.ShapeDtypeStruct((rows, lanes), F32)],
        compiler_params=_cparams(("arbitrary",)),
        name="route",
    )(lg3, bias)


GATHER_UNROLL = 8
CAST_ROWS = 128


def _moe_kernel(ta_ref, tb_ref, nused_ref, valid_ref,
                h_hbm, tok_ref, tok_next_ref, wa_ref, wb_ref, upa_ref, upb_ref, dna_ref, dnb_ref,
                m_hbm, xbuf, ybuf, w_up, w_dn, gsem, ssem):
    i = pl.program_id(0)
    n_used = nused_ref[0]
    slot = i % 2
    rows = MOE_TILE * SUBLANES

    def row_copy_in(toks, s, r):
        tok = toks[0, 0, r]
        return pltpu.make_async_copy(
            h_hbm.at[pl.ds(pl.multiple_of(tok * SUBLANES, SUBLANES), SUBLANES), :],
            xbuf.at[s, pl.ds(pl.multiple_of(r * SUBLANES, SUBLANES), SUBLANES), :],
            gsem.at[s])

    def row_copy_out(toks, s, r):
        tok = toks[0, 0, r]
        return pltpu.make_async_copy(
            ybuf.at[s, pl.ds(pl.multiple_of(r * SUBLANES, SUBLANES), SUBLANES), :],
            m_hbm.at[pl.ds(pl.multiple_of(tok * SUBLANES, SUBLANES), SUBLANES), :],
            ssem.at[s])

    def start_rows(make, toks, s, count):
        def body8(c, carry):
            for u in range(GATHER_UNROLL):
                make(toks, s, c * GATHER_UNROLL + u).start()
            return carry

        def body1(r, carry):
            make(toks, s, r).start()
            return carry

        full = count // GATHER_UNROLL
        lax.fori_loop(0, full, body8, 0)
        lax.fori_loop(full * GATHER_UNROLL, count, body1, 0)

    def wait_gather(tile, s):
        n = pl.multiple_of(valid_ref[tile] * SUBLANES, SUBLANES)
        pltpu.make_async_copy(h_hbm.at[pl.ds(0, n), :], xbuf.at[s, pl.ds(0, n), :], gsem.at[s]).wait()

    def wait_scatter(tile, s):
        n = pl.multiple_of(valid_ref[tile] * SUBLANES, SUBLANES)
        pltpu.make_async_copy(ybuf.at[s, pl.ds(0, n), :], m_hbm.at[pl.ds(0, n), :], ssem.at[s]).wait()

    @pl.when(i == 0)
    def _():
        xbuf[...] = jnp.zeros(xbuf.shape, F32)
        start_rows(row_copy_in, tok_ref, 0, valid_ref[0])

    @pl.when(i + 1 < n_used)
    def _():
        start_rows(row_copy_in, tok_next_ref, 1 - slot, valid_ref[i + 1])

    def refresh(which, ids_ref, up_ref, dn_ref):
        prev = ids_ref[jnp.maximum(i - 1, 0)]

        @pl.when((i == 0) | (ids_ref[i] != prev))
        def _():
            def cast_up(c, carry):
                r = pl.multiple_of(c * CAST_ROWS, CAST_ROWS)
                w_up[which, pl.ds(r, CAST_ROWS), :] = up_ref[0, pl.ds(r, CAST_ROWS), :].astype(BF16)
                return carry

            def cast_dn(c, carry):
                r = pl.multiple_of(c * CAST_ROWS, CAST_ROWS)
                w_dn[which, pl.ds(r, CAST_ROWS), :] = dn_ref[0, pl.ds(r, CAST_ROWS), :].astype(BF16)
                return carry

            lax.fori_loop(0, up_ref.shape[1] // CAST_ROWS, cast_up, 0)
            lax.fori_loop(0, dn_ref.shape[1] // CAST_ROWS, cast_dn, 0)

    @pl.when(i < n_used)
    def _():
        refresh(0, ta_ref, upa_ref, dna_ref)
        refresh(1, tb_ref, upb_ref, dnb_ref)
        wait_gather(i, slot)
        x = _load_token_major(xbuf.at[slot], MOE_TILE, upa_ref.shape[1]).astype(BF16)

        def ffn(which):
            gu = jnp.dot(x, w_up[which], preferred_element_type=F32)
            hid = jax.nn.silu(gu[:, :D_EXPERT]) * gu[:, D_EXPERT:]
            return jnp.dot(hid.astype(BF16), w_dn[which], preferred_element_type=F32)

        y = ffn(0) * wa_ref[...] + ffn(1) * wb_ref[...]

        @pl.when(i >= 2)
        def _():
            wait_scatter(i - 2, slot)

        _store_token_major(ybuf.at[slot], y)
        start_rows(row_copy_out, tok_ref, slot, valid_ref[i])

        @pl.when(i == n_used - 1)
        def _():
            @pl.when(i >= 1)
            def _():
                wait_scatter(i - 1, 1 - slot)
            wait_scatter(i, slot)


def _moe_call(layer, tile_a, tile_b, n_used, tile_valid, slot_tok, h1_tm, slot_wa, slot_wb, w_up, w_down,
              n_tiles, t):
    d = w_up.shape[1]
    e0 = layer * N_EXPERTS
    wspec = pl.BlockSpec((MOE_TILE, 1), lambda i, *_: (i, 0))
    tok_blk = (1, 1, MOE_TILE)
    grid_spec = pltpu.PrefetchScalarGridSpec(
        num_scalar_prefetch=4,
        grid=(n_tiles,),
        in_specs=[pl.BlockSpec(memory_space=pl.ANY),
                  pl.BlockSpec(tok_blk, lambda i, *_: (i, 0, 0), memory_space=pltpu.SMEM),
                  pl.BlockSpec(tok_blk, lambda i, *_: (jnp.minimum(i + 1, n_tiles - 1), 0, 0),
                               memory_space=pltpu.SMEM),
                  wspec, wspec,
                  pl.BlockSpec((1, d, 2 * D_EXPERT), lambda i, ta, tb, *_: (e0 + ta[i], 0, 0)),
                  pl.BlockSpec((1, d, 2 * D_EXPERT), lambda i, ta, tb, *_: (e0 + tb[i], 0, 0)),
                  pl.BlockSpec((1, D_EXPERT, d), lambda i, ta, tb, *_: (e0 + ta[i], 0, 0)),
                  pl.BlockSpec((1, D_EXPERT, d), lambda i, ta, tb, *_: (e0 + tb[i], 0, 0))],
        out_specs=pl.BlockSpec(memory_space=pl.ANY),
        scratch_shapes=[pltpu.VMEM((2, MOE_TILE * SUBLANES, LANES), F32),
                        pltpu.VMEM((2, MOE_TILE * SUBLANES, LANES), F32),
                        pltpu.VMEM((2, d, 2 * D_EXPERT), BF16), pltpu.VMEM((2, D_EXPERT, d), BF16),
                        pltpu.SemaphoreType.DMA((2,)), pltpu.SemaphoreType.DMA((2,))])
    return pl.pallas_call(
        _moe_kernel,
        grid_spec=grid_spec,
        out_shape=jax.ShapeDtypeStruct((t * SUBLANES, LANES), F32),
        compiler_params=_cparams(("arbitrary",)),
        name="moe",
    )(tile_a, tile_b, n_used, tile_valid, h1_tm, slot_tok, slot_tok, slot_wa, slot_wb, w_up, w_up, w_down, w_down)


def _final_kernel(alpha, h1_ref, m_ref, e_ref, g_ref, b_ref, o_ref):
    rows, d = o_ref.shape
    h1 = _load_token_major(h1_ref, rows, d)
    m = _load_token_major(m_ref, rows, d)
    o_ref[...] = _layer_norm(alpha * h1 + m + e_ref[...].astype(F32), g_ref[...], b_ref[...])


def _final_call(alpha, h1_tm, m_tm, e, g, b):
    t, d = e.shape
    tm = TM_FINAL
    row = lambda i: (i, 0)
    tmaj = pl.BlockSpec((tm * SUBLANES, LANES), row)
    return pl.pallas_call(
        functools.partial(_final_kernel, alpha),
        grid=(t // tm,),
        in_specs=[tmaj, tmaj, pl.BlockSpec((tm, d), row), _full(g.shape), _full(b.shape)],
        out_specs=pl.BlockSpec((tm, d), row),
        out_shape=jax.ShapeDtypeStruct((t, d), F32),
        compiler_params=_cparams(("arbitrary",)),
        name="final_ln",
    )(h1_tm, m_tm, e, g, b)


def _pack_w_in(w_in):
    off_q = OFF_CONV + 2 * D_CONV
    off_kr = off_q + Q_LORA + KV_LORA
    off_gate = off_kr + QK_ROPE
    kr = jnp.pad(w_in[..., off_kr:off_gate], ((0, 0), (0, 0), (ROPE_LANE0, HEAD_BLOCK - ROPE_LANE0 - QK_ROPE)))
    return jnp.concatenate([w_in[..., :off_kr], kr, w_in[..., off_gate:]], axis=-1).astype(BF16)


def _pack_heads(w, lo, width, stride, lane0):
    blocks = []
    for h in range(N_HEADS):
        l0 = lane0(h)
        blocks.append(jnp.pad(w[..., h * stride + lo:h * stride + lo + width],
                              ((0, 0), (0, 0), (l0, HEAD_BLOCK - l0 - width))))
    return jnp.concatenate(blocks, axis=-1).astype(BF16)


def _rope_tables(positions):
    inv_freq = jnp.power(ROPE_THETA, -jnp.arange(0, QK_ROPE, 2, dtype=F32) / QK_ROPE)
    ang = positions.astype(F32).reshape(-1, 1) * inv_freq
    cos, sin = jnp.cos(ang), jnp.sin(ang)
    t = ang.shape[0]
    ones_lo = jnp.ones((t, ROPE_LANE0), F32)
    ones_hi = jnp.ones((t, HEAD_BLOCK - ROPE_LANE0 - QK_ROPE), F32)
    zeros_lo = jnp.zeros((t, ROPE_LANE0), F32)
    zeros_half = jnp.zeros((t, HALF_ROPE), F32)
    zeros_hi = jnp.zeros((t, HEAD_BLOCK - ROPE_LANE0 - QK_ROPE), F32)
    c = jnp.concatenate([ones_lo, cos, cos, ones_hi], axis=-1)
    s1 = jnp.concatenate([zeros_lo, -sin, zeros_half, zeros_hi], axis=-1)
    s2 = jnp.concatenate([zeros_lo, zeros_half, sin, zeros_hi], axis=-1)
    return c, s1, s2


def _routing_plan(cls, wa, wb, n_tiles):
    t = cls.shape[0]
    n_pad = n_tiles * MOE_TILE - t
    assert n_pad == N_CLASSES * MOE_TILE, n_pad
    classes = jnp.arange(N_CLASSES, dtype=jnp.int32)
    counts = jnp.sum((cls[None, :] == classes[:, None]).astype(jnp.int32), axis=1)
    padded = (counts + MOE_TILE - 1) // MOE_TILE * MOE_TILE
    pad_end = jnp.cumsum(padded)
    pad_start = pad_end - padded
    pad_need = padded - counts
    j = jnp.arange(MOE_TILE, dtype=jnp.int32)[None, :]
    pad_key = jnp.where(j < pad_need[:, None], 2 * classes[:, None] + 1, 2 * N_CLASSES).reshape(-1)
    zeros_i = jnp.zeros((n_pad,), jnp.int32)
    zeros_f = jnp.zeros((n_pad,), F32)
    _, slot_tok, slot_wa, slot_wb = lax.sort(
        (jnp.concatenate([2 * cls, pad_key]), jnp.concatenate([jnp.arange(t, dtype=jnp.int32), zeros_i]),
         jnp.concatenate([wa, zeros_f]), jnp.concatenate([wb, zeros_f])), num_keys=1)
    slot_tok = slot_tok.reshape(n_tiles, 1, MOE_TILE)
    slot_wa = slot_wa.reshape(-1, 1)
    slot_wb = slot_wb.reshape(-1, 1)
    tile_row0 = jnp.arange(n_tiles, dtype=jnp.int32) * MOE_TILE
    tile_cls = jnp.minimum(jnp.sum((tile_row0[:, None] >= pad_end[None, :]).astype(jnp.int32), axis=1),
                           N_CLASSES - 1)
    onehot = (tile_cls[:, None] == classes[None, :]).astype(jnp.int32)
    tile_valid = jnp.clip(jnp.sum(onehot * (pad_start + counts)[None, :], axis=1) - tile_row0,
                          0, MOE_TILE).astype(jnp.int32)
    pair_lo = jnp.array([p[0] for p in PAIRS], jnp.int32)
    pair_hi = jnp.array([p[1] for p in PAIRS], jnp.int32)
    grp = tile_cls // len(PAIRS)
    pair_onehot = ((tile_cls % len(PAIRS))[:, None] == jnp.arange(len(PAIRS), dtype=jnp.int32)[None, :])
    tile_a = grp * EXPERTS_PER_GROUP + jnp.sum(pair_onehot * pair_lo[None, :], axis=1)
    tile_b = grp * EXPERTS_PER_GROUP + jnp.sum(pair_onehot * pair_hi[None, :], axis=1)
    n_used = (pad_end[-1] // MOE_TILE).astype(jnp.int32).reshape(1)
    return tile_a, tile_b, n_used, tile_valid, slot_tok, slot_wa, slot_wb


def kernel(x, p, positions, ln_in_g, ln_in_b, w_in, b_gate, pool_w, pool_scale, pool_proj, conv_dw, conv_b, conv_ln_g, conv_ln_b, conv_proj, q_norm_g, w_uq, kv_norm_g, w_ukv, mla_proj, w_out, ln1_g, ln1_b, w_router, router_bias, exp_w_up, exp_w_down, ple_proj, ple_gate, ln2_g, ln2_b):
    batch, seq, d = x.shape
    depth = w_in.shape[0]
    t = batch * seq
    alpha = float((2 * depth) ** 0.25)
    n_tiles = (t + N_CLASSES * (MOE_TILE - 1)) // MOE_TILE + 1

    w_in_p = _pack_w_in(w_in)
    pool_w_b = pool_w.astype(BF16)
    pool_proj_b, conv_proj_b, mla_proj_b = pool_proj.astype(BF16), conv_proj.astype(BF16), mla_proj.astype(BF16)
    qk = QK_NOPE + QK_ROPE
    wq_p = (_pack_heads(w_uq, 0, QK_NOPE, qk, lambda h: 0)
            + _pack_heads(w_uq, QK_NOPE, QK_ROPE, qk, lambda h: ROPE_LANE0))
    wqt_p = jnp.swapaxes(wq_p, 1, 2)
    wk_p = _pack_heads(w_ukv, 0, QK_NOPE, QK_NOPE + V_HEAD, lambda h: 0)
    kv_w = QK_NOPE + V_HEAD
    wvt_p = jnp.swapaxes(jnp.concatenate(
        [jnp.pad(w_ukv[..., h * kv_w + QK_NOPE:(h + 1) * kv_w], ((0, 0), (0, 0), (0, BF16_ROWS)))
         for h in range(N_HEADS)], axis=-1), 1, 2).astype(BF16)
    ones_rows = jnp.tile(jnp.concatenate([jnp.zeros((V_HEAD, 1), F32), jnp.ones((BF16_ROWS, 1), F32)]),
                         (N_HEADS, 1))
    w_out_b, ple_proj_b, ple_gate_b = w_out.astype(BF16), ple_proj.astype(BF16), ple_gate.astype(BF16)
    w_up_b = exp_w_up.reshape((depth * N_EXPERTS,) + exp_w_up.shape[2:])
    w_down_b = exp_w_down.reshape((depth * N_EXPERTS,) + exp_w_down.shape[2:])
    wr_t = w_router.T
    wr_hi = wr_t.astype(BF16)
    wr_lo = (wr_t - wr_hi.astype(F32)).astype(BF16)
    rope_tabs = _rope_tables(positions)
    rope_tabs_t = tuple(a.T for a in rope_tabs)
    p2 = p.reshape(depth, t, -1)

    pre, pre_g, pre_b = (x.reshape(t, d),), ln_in_g, ln_in_b
    for i in range(depth):
        branch_params = (conv_dw[i], conv_b[i].reshape(1, -1), conv_ln_g[i].reshape(1, -1),
                         conv_ln_b[i].reshape(1, -1), pool_w_b[i], pool_scale[i].reshape(1, -1))
        mla_params = (rope_tabs, rope_tabs_t, q_norm_g[i].reshape(1, -1), kv_norm_g[i].reshape(1, -1),
                      wqt_p, wk_p, wvt_p, ones_rows)
        h, a_pool, a_conv, qt, k, vt, glog = _in_proj_call(alpha, pre, pre_g.reshape(1, d), pre_b.reshape(1, d),
                                                           w_in_p, branch_params, mla_params, i, seq)
        a_attn = _attn_call(qt, k, vt, batch, seq)
        h1_tm, e, logits = _merge_call(alpha, i, a_pool, a_conv, a_attn, glog, h, p2,
                                       pool_proj_b, conv_proj_b, mla_proj_b, b_gate[i], w_out_b,
                                       ln1_g[i].reshape(1, d), ln1_b[i].reshape(1, d),
                                       ple_proj_b, ple_gate_b, wr_hi, wr_lo)
        cls, wa, wb = _route_call(logits.reshape(N_EXPERTS, t // LANES, LANES), router_bias)
        plan = _routing_plan(cls.reshape(t), wa.reshape(t), wb.reshape(t), n_tiles)
        m_tm = _moe_call(i, *plan[:5], h1_tm, plan[5], plan[6], w_up_b, w_down_b, n_tiles, t)
        pre, pre_g, pre_b = (h1_tm, m_tm, e), ln2_g[i], ln2_b[i]
    h = _final_call(alpha, *pre, pre_g.reshape(1, d), pre_b.reshape(1, d))
    return h.reshape(batch, seq, d)
```

```python
import functools

import jax
import jax.numpy as jnp
from jax import lax
from jax.experimental import pallas as pl
from jax.experimental.pallas import tpu as pltpu

F32 = jnp.float32
BF16 = jnp.bfloat16

CHUNK = 64
POOL_WINDOWS = (2, 4, 8, 16)
POOL_GW = 128
D_POOL = 512
D_CONV = 512
CONV_WIDTH = 31
N_HEADS = 8
QK_NOPE = 64
QK_ROPE = 32
V_HEAD = 64
Q_LORA = 384
KV_LORA = 256
ROPE_THETA = 10000.0
N_EXPERTS = 16
N_GROUPS = 4
EXPERTS_PER_GROUP = 4
D_EXPERT = 512
LN_EPS = 1e-5
RMS_EPS = 1e-6

LANES = 128
SUBLANES = 8
HEAD_BLOCK = LANES
ROPE_LANE0 = QK_NOPE
HALF_ROPE = QK_ROPE // 2
BF16_ROWS = 16
VT_BLOCK = V_HEAD + BF16_ROWS
LOG2_E = 1.4426950408889634

TM_PROJ = 512
TM_MERGE = 512
MERGE_SUB = 256
TM_FINAL = 512
TQ = 256
TK = 256
Q_SUBS = 2
ATTN_LOOKAHEAD = 3
POOL_CHUNK = 256
POOL_HALO = 16
CONV_CHUNK = 128
CONV_HALO = 32
MOE_TILE = 256
ROUTE_ROWS = 8

PAIRS = ((0, 1), (0, 2), (0, 3), (1, 2), (1, 3), (2, 3))
N_CLASSES = N_GROUPS * len(PAIRS)

VMEM_LIMIT = 56 * 1024 * 1024


def _cparams(sem):
    return pltpu.CompilerParams(dimension_semantics=sem, vmem_limit_bytes=VMEM_LIMIT)


def _layer_norm(x, g, b):
    mu = jnp.mean(x, axis=-1, keepdims=True)
    xc = x - mu
    var = jnp.mean(xc * xc, axis=-1, keepdims=True)
    return xc * lax.rsqrt(var + LN_EPS) * g + b


def _rms_norm(x, g):
    ms = jnp.mean(x * x, axis=-1, keepdims=True)
    return x * lax.rsqrt(ms + RMS_EPS) * g


def _sigmoid(x):
    return 0.5 * jnp.tanh(0.5 * x) + 0.5


def _full(shape):
    n = len(shape)
    return pl.BlockSpec(shape, lambda *_: (0,) * n)


def _layer_full(stacked, layer):
    n = stacked.ndim - 1
    return pl.BlockSpec((None,) + stacked.shape[1:], lambda *_: (layer,) + (0,) * n)


OFF_CONV = D_POOL
OFF_CQKV = OFF_CONV + 2 * D_CONV
W_CQKV = Q_LORA + KV_LORA + HEAD_BLOCK
N_COLS_CHUNK = 512


def _in_proj_kernel(alpha, residual, tiles_per_seq, *refs):
    n_pre = 5 if residual else 3
    if residual:
        h1_ref, m_ref, e_ref, g_ref, b_ref = refs[:n_pre]
    else:
        x_ref, g_ref, b_ref = refs[:n_pre]
    w_ref, wkr_ref, wg_ref, dw_ref, cb_ref, cg_ref, cbb_ref, pw_ref, pscale_ref = refs[n_pre:n_pre + 9]
    mla_refs = refs[n_pre + 9:-9]
    h_ref, apool_ref, aconv_ref, qt_ref, k_ref, vt_ref, gate_ref, zs, ps = refs[-9:]
    rows, d = h_ref.shape
    seq_tile = pl.program_id(0) % tiles_per_seq

    @pl.when(seq_tile == 0)
    def _():
        zs[0:CONV_HALO, :] = jnp.zeros((CONV_HALO, D_CONV), F32)
        ps[0:POOL_HALO, :] = jnp.zeros((POOL_HALO, D_POOL), F32)

    half = rows // 2
    xs = []
    for k in range(2):
        r0 = k * half
        if residual:
            pre = (alpha * _load_token_major(h1_ref.at[pl.ds(r0 * SUBLANES, half * SUBLANES), :], half, d)
                   + _load_token_major(m_ref.at[pl.ds(r0 * SUBLANES, half * SUBLANES), :], half, d))
            pre = pre + e_ref[r0:r0 + half, :].astype(F32)
        else:
            pre = x_ref[r0:r0 + half, :]
        h = _layer_norm(pre, g_ref[...], b_ref[...])
        h_ref[r0:r0 + half, :] = h
        xs.append(h.astype(BF16))

    def mm_ref(ref, lo, hi):
        return jnp.concatenate([jnp.dot(xk, ref[:, lo:hi], preferred_element_type=F32) for xk in xs], axis=0)

    mm = functools.partial(mm_ref, w_ref)

    zs[CONV_HALO:CONV_HALO + rows, :] = jax.nn.sigmoid(mm(OFF_CONV + D_CONV, OFF_CQKV))
    zs[CONV_HALO:CONV_HALO + rows, :] = mm(OFF_CONV, OFF_CONV + D_CONV) * zs[CONV_HALO:CONV_HALO + rows, :]
    ps[POOL_HALO:POOL_HALO + rows, :] = mm(0, OFF_CONV)
    cqkv = jnp.concatenate([mm(OFF_CQKV, OFF_CQKV + Q_LORA + KV_LORA), mm_ref(wkr_ref, 0, HEAD_BLOCK)], axis=-1)

    def mla_work():
        _mla_project(cqkv, *mla_refs, qt_ref, k_ref, vt_ref)

    def conv_work(r0):
        aconv_ref[r0:r0 + CONV_CHUNK, :] = _conv_chunk(zs, r0, dw_ref, cb_ref, cg_ref, cbb_ref).astype(BF16)

    def pool_work(r0):
        _pool_chunk(ps, r0, seq_tile * rows, pw_ref, pscale_ref, apool_ref)

    work = [mla_work] + [functools.partial(conv_work, r0) for r0 in range(0, rows, CONV_CHUNK)]
    work += [functools.partial(pool_work, r0) for r0 in range(0, rows, POOL_CHUNK)]
    n_gate_chunks = gate_ref.shape[1] // N_COLS_CHUNK
    for c in range(n_gate_chunks):
        gate_ref[:, c * N_COLS_CHUNK:(c + 1) * N_COLS_CHUNK] = mm_ref(
            wg_ref, c * N_COLS_CHUNK, (c + 1) * N_COLS_CHUNK).astype(BF16)
        for item in work[len(work) * c // n_gate_chunks:len(work) * (c + 1) // n_gate_chunks]:
            item()
    zs[0:CONV_HALO, :] = zs[rows:rows + CONV_HALO, :]
    ps[0:POOL_HALO, :] = ps[rows:rows + POOL_HALO, :]


def _in_proj_call(alpha, pre, g, b, w, branch_params, mla_params, layer, seq):
    residual = len(pre) == 3
    t, d = pre[-1].shape
    n_gate = w[2].shape[-1]
    nh = N_HEADS * HEAD_BLOCK
    nv = N_HEADS * VT_BLOCK
    row = lambda i: (i, 0)
    col = lambda i: (0, i)
    tmaj = pl.BlockSpec((TM_PROJ * SUBLANES, LANES), row)
    pre_specs = [tmaj, tmaj, pl.BlockSpec((TM_PROJ, d), row)] if residual else [pl.BlockSpec((TM_PROJ, d), row)]
    tabs, tabs_t, qg, kvg, wqt, wk, wvt, ones_rows = mla_params
    mla_specs = ([pl.BlockSpec((TM_PROJ, LANES), row)] * 3 + [pl.BlockSpec((LANES, TM_PROJ), col)] * 3
                 + [_full(qg.shape), _full(kvg.shape), _layer_full(wqt, layer), _layer_full(wk, layer),
                    _layer_full(wvt, layer), _full(ones_rows.shape)])
    return pl.pallas_call(
        functools.partial(_in_proj_kernel, alpha, residual, seq // TM_PROJ),
        grid=(t // TM_PROJ,),
        in_specs=pre_specs + [_full(g.shape), _full(b.shape)] + [_layer_full(a, layer) for a in w]
        + [_full(a.shape) for a in branch_params] + mla_specs,
        out_specs=[pl.BlockSpec((TM_PROJ, d), row),
                   pl.BlockSpec((TM_PROJ, D_POOL), row), pl.BlockSpec((TM_PROJ, D_CONV), row),
                   pl.BlockSpec((nh, TM_PROJ), col), pl.BlockSpec((TM_PROJ, nh), row),
                   pl.BlockSpec((nv, TM_PROJ), col), pl.BlockSpec((TM_PROJ, n_gate), row)],
        out_shape=[jax.ShapeDtypeStruct((t, d), F32),
                   jax.ShapeDtypeStruct((t, D_POOL), BF16), jax.ShapeDtypeStruct((t, D_CONV), BF16),
                   jax.ShapeDtypeStruct((nh, t), BF16), jax.ShapeDtypeStruct((t, nh), BF16),
                   jax.ShapeDtypeStruct((nv, t), BF16), jax.ShapeDtypeStruct((t, n_gate), BF16)],
        scratch_shapes=[pltpu.VMEM((TM_PROJ + CONV_HALO, D_CONV), F32),
                        pltpu.VMEM((TM_PROJ + POOL_HALO, D_POOL), F32)],
        compiler_params=_cparams(("arbitrary",)),
        name="in_proj",
    )(*pre, g, b, *w, *branch_params, *tabs, *tabs_t, qg, kvg, wqt, wk, wvt, ones_rows)


def _pool_chunk(ps, r0, seq_row0, w_ref, scale_ref, o_ref):
    t = seq_row0 + r0 + lax.broadcasted_iota(jnp.int32, (POOL_CHUNK, 1), 0)
    for g, w in enumerate(POOL_WINDOWS):
        cols = slice(g * POOL_GW, (g + 1) * POOL_GW)
        xw = ps[r0:r0 + POOL_CHUNK + POOL_HALO, cols]
        acc = xw
        k = 1
        while k < w:
            acc = acc + pltpu.roll(acc, k, axis=0)
            k *= 2
        cnt = jnp.minimum(t + 1, w).astype(F32)
        mixed = acc[POOL_HALO:] / cnt - xw[POOL_HALO:]
        y = jnp.dot(mixed.astype(BF16), w_ref[g], preferred_element_type=F32) * scale_ref[:, cols]
        o_ref[r0:r0 + POOL_CHUNK, cols] = y.astype(BF16)


def _conv_chunk(zs, r0, dw_ref, cb_ref, g_ref, b_ref):
    parts = []
    for cg in range(D_CONV // LANES):
        cols = slice(cg * LANES, (cg + 1) * LANES)
        win = zs[r0:r0 + CONV_CHUNK + CONV_HALO, cols]
        acc = jnp.zeros((CONV_CHUNK, LANES), F32) + cb_ref[:, cols]
        for sub in range(SUBLANES):
            shifted = win if sub == 0 else pltpu.roll(win, sub, axis=0)
            for a in range(CONV_HALO // SUBLANES):
                lag = SUBLANES * a + sub
                if lag >= CONV_WIDTH:
                    continue
                k = CONV_WIDTH - 1 - lag
                lo = CONV_HALO - SUBLANES * a
                acc = acc + dw_ref[k:k + 1, cols] * shifted[lo:lo + CONV_CHUNK]
        parts.append(acc)
    y = _layer_norm(jnp.concatenate(parts, axis=-1), g_ref[...], b_ref[...])
    return y * jax.nn.sigmoid(y)


def _rope_block(x, c, s1, s2):
    return x * c + pltpu.roll(x, LANES - HALF_ROPE, axis=1) * s1 + pltpu.roll(x, HALF_ROPE, axis=1) * s2


def _rope_block_t(x, c, s1, s2):
    return x * c + pltpu.roll(x, HEAD_BLOCK - HALF_ROPE, axis=0) * s1 + pltpu.roll(x, HALF_ROPE, axis=0) * s2


def _mla_project(cqkv, c_ref, s1_ref, s2_ref, ct_ref, s1t_ref, s2t_ref, qg_ref, kvg_ref,
                 wqt_ref, wk_ref, wvt_ref, ones_ref, qt_ref, k_ref, vt_ref):
    cq = _rms_norm(cqkv[:, 0:Q_LORA], qg_ref[...]).astype(BF16)
    ckv = _rms_norm(cqkv[:, Q_LORA:Q_LORA + KV_LORA], kvg_ref[...]).astype(BF16)
    kr = _rope_block(cqkv[:, Q_LORA + KV_LORA:W_CQKV], c_ref[...], s1_ref[...], s2_ref[...])
    scale = float((QK_NOPE + QK_ROPE) ** -0.5 * LOG2_E)
    nt = (((1,), (1,)), ((), ()))
    ct, s1t, s2t = ct_ref[...], s1t_ref[...], s2t_ref[...]
    for h in range(N_HEADS):
        cols = slice(h * HEAD_BLOCK, (h + 1) * HEAD_BLOCK)
        qt = lax.dot_general(wqt_ref[cols, :], cq, nt, preferred_element_type=F32) * scale
        qt_ref[cols, :] = _rope_block_t(qt, ct, s1t, s2t).astype(BF16)
        k = jnp.dot(ckv, wk_ref[:, cols], preferred_element_type=F32) + kr
        k_ref[:, cols] = k.astype(BF16)
    vt = lax.dot_general(wvt_ref[...], ckv, nt, preferred_element_type=F32) + ones_ref[...]
    vt_ref[...] = vt.astype(BF16)


def _attn_kernel(qt_ref, k_ref, vt_ref, o_ref, m_ref, acc_ref):
    j = pl.program_id(1)
    key_chunk = lax.broadcasted_iota(jnp.int32, (TK, TQ), 0) // CHUNK
    qry_chunk = lax.broadcasted_iota(jnp.int32, (TK, TQ), 1) // CHUNK
    diag_mask = key_chunk <= qry_chunk

    def scores(h, sub, kt):
        k0 = pl.multiple_of(kt * TK, TK)
        kk = k_ref[pl.ds(k0, TK), h * HEAD_BLOCK:(h + 1) * HEAD_BLOCK]
        return jnp.dot(kk, qt_ref[h * HEAD_BLOCK:(h + 1) * HEAD_BLOCK, sub * TQ:(sub + 1) * TQ],
                       preferred_element_type=F32)

    def values_t(h, kt):
        k0 = pl.multiple_of(kt * TK, TK)
        return vt_ref[h * VT_BLOCK:(h + 1) * VT_BLOCK, pl.ds(k0, TK)]

    def first_update(h, sub, kt, masked, s):
        st = h * Q_SUBS + sub
        if masked:
            s = jnp.where(diag_mask, s, -jnp.inf)
        m = jnp.max(s, axis=0, keepdims=True)
        p = jnp.exp2(s - m)
        m_ref[st] = jnp.broadcast_to(m, (SUBLANES, TQ))
        acc_ref[st] = jnp.dot(values_t(h, kt), p.astype(BF16), preferred_element_type=F32)

    def update(h, sub, kt, masked, s):
        st = h * Q_SUBS + sub
        if masked:
            s = jnp.where(diag_mask, s, -jnp.inf)
        m_old = m_ref[st]
        m_new = jnp.maximum(m_old, jnp.max(s, axis=0, keepdims=True))
        alpha = jnp.exp2(m_old - m_new)
        p = jnp.exp2(s - m_new[0:1, :])
        m_ref[st] = m_new
        acc_ref[st] = alpha[0:1, :] * acc_ref[st] + jnp.dot(values_t(h, kt), p.astype(BF16),
                                                           preferred_element_type=F32)

    def run(work):
        pending = {}
        for idx in range(len(work) + ATTN_LOOKAHEAD):
            if idx < len(work):
                h, sub, kt, _, _ = work[idx]
                pending[idx] = scores(h, sub, kt)
            if idx >= ATTN_LOOKAHEAD:
                h, sub, kt, masked, upd = work[idx - ATTN_LOOKAHEAD]
                upd(h, sub, kt, masked, pending.pop(idx - ATTN_LOOKAHEAD))

    tail = []
    for kk in range(Q_SUBS):
        for h in range(N_HEADS):
            for sub in range(kk, Q_SUBS):
                tail.append((h, sub, Q_SUBS * j + kk, kk == sub, first_update if kk == 0 else update))
    run(tail)

    def step(kt, carry):
        run([(h, sub, kt, False, update) for h in range(N_HEADS) for sub in range(Q_SUBS)])
        return carry

    lax.fori_loop(0, Q_SUBS * j, step, 0)

    for sub in range(Q_SUBS):
        outs = []
        for h in range(N_HEADS):
            st = h * Q_SUBS + sub
            outs.append(acc_ref[st, 0:V_HEAD, :] * (1.0 / acc_ref[st, V_HEAD:V_HEAD + 1, :]))
        o_ref[sub * TQ:(sub + 1) * TQ, :] = jnp.concatenate(outs, axis=0).T.astype(BF16)


def _attn_call(qt, k, vt, batch, seq):
    n = N_HEADS * HEAD_BLOCK
    tq = Q_SUBS * TQ
    nq = seq // tq
    return pl.pallas_call(
        _attn_kernel,
        grid=(batch, nq),
        in_specs=[pl.BlockSpec((n, tq), lambda b, i: (0, b * nq + i)),
                  pl.BlockSpec((seq, n), lambda b, i: (b, 0)),
                  pl.BlockSpec((N_HEADS * VT_BLOCK, seq), lambda b, i: (0, b))],
        out_specs=pl.BlockSpec((tq, N_HEADS * V_HEAD), lambda b, i: (b * nq + i, 0)),
        out_shape=jax.ShapeDtypeStruct((k.shape[0], N_HEADS * V_HEAD), BF16),
        scratch_shapes=[pltpu.VMEM((N_HEADS * Q_SUBS, SUBLANES, TQ), F32),
                        pltpu.VMEM((N_HEADS * Q_SUBS, VT_BLOCK, TQ), F32)],
        compiler_params=_cparams(("arbitrary", "arbitrary")),
        name="attn",
    )(qt, k, vt)


def _store_token_major(ref, x):
    rows = x.shape[0]
    for j in range(x.shape[1] // LANES):
        ref[pl.ds(j, rows, stride=SUBLANES), :] = x[:, j * LANES:(j + 1) * LANES]


def _load_token_major(ref, rows, d):
    return jnp.concatenate([ref[pl.ds(j, rows, stride=SUBLANES), :] for j in range(d // LANES)], axis=-1)


def _merge_kernel(alpha, ap_ref, ac_ref, at_ref, gl_ref, h_ref, p_ref, pp_ref, cp_ref, mp_ref, bg_ref,
                  wo_ref, g_ref, b_ref, pproj_ref, pgate_ref, wrh_ref, wrl_ref,
                  h1_ref, e_ref, lg_ref):
    d = h_ref.shape[1]
    sub = MERGE_SUB
    parts = [pl.ds(k * sub, sub) for k in range(h_ref.shape[0] // sub)]
    nt = (((1,), (1,)), ((), ()))

    def merged_of(rows):
        merged = None
        for br, (a_ref, w_ref) in enumerate(((ap_ref, pp_ref), (ac_ref, cp_ref), (at_ref, mp_ref))):
            y = jnp.dot(a_ref[rows, :], w_ref[...], preferred_element_type=F32)
            gate = _sigmoid(gl_ref[rows, br * d:(br + 1) * d].astype(F32) + bg_ref[br:br + 1, :])
            merged = gate * y if merged is None else merged + gate * y
        return merged.astype(BF16)

    merged = [merged_of(rows) for rows in parts]
    ys = [jnp.dot(m, wo_ref[...], preferred_element_type=F32) for m in merged]
    es = [jnp.dot(p_ref[rows, :].astype(BF16), pproj_ref[...], preferred_element_type=F32) for rows in parts]
    h1s = [_layer_norm(alpha * h_ref[rows, :] + y, g_ref[...], b_ref[...]) for rows, y in zip(parts, ys)]
    for k, (rows, h1, e) in enumerate(zip(parts, h1s, es)):
        h1b = h1.astype(BF16)
        e = e * _sigmoid(jnp.dot(h1b, pgate_ref[...], preferred_element_type=F32))
        e_ref[rows, :] = e.astype(BF16)
        h1l = (h1 - h1b.astype(F32)).astype(BF16)
        lg = lax.dot_general(wrh_ref[...], h1b, nt, preferred_element_type=F32)
        lg = lg + lax.dot_general(wrh_ref[...], h1l, nt, preferred_element_type=F32)
        lg = lg + lax.dot_general(wrl_ref[...], h1b, nt, preferred_element_type=F32)
        lg_ref[:, k * sub:(k + 1) * sub] = lg
        _store_token_major(h1_ref.at[pl.ds(k * sub * SUBLANES, sub * SUBLANES), :], h1)


def _merge_call(alpha, layer, ap, ac, at, gl, h, p, pp, cp, mp, bg, wo, g, b, pproj, pgate, wrh, wrl):
    t, d = h.shape
    tm = TM_MERGE
    row = lambda i: (i, 0)
    ins = [ap, ac, at, gl, h]
    in_specs = [pl.BlockSpec((tm, a.shape[1]), row) for a in ins]
    in_specs.append(pl.BlockSpec((None, tm, p.shape[-1]), lambda i: (layer, i, 0)))
    stacked = {id(a) for a in (pp, cp, mp, wo, pproj, pgate)}
    consts = [pp, cp, mp, bg, wo, g, b, pproj, pgate, wrh, wrl]
    in_specs += [_layer_full(a, layer) if id(a) in stacked else _full(a.shape) for a in consts]
    return pl.pallas_call(
        functools.partial(_merge_kernel, alpha),
        grid=(t // tm,),
        in_specs=in_specs,
        out_specs=[pl.BlockSpec((tm * SUBLANES, LANES), row), pl.BlockSpec((tm, d), row),
                   pl.BlockSpec((N_EXPERTS, tm), lambda i: (0, i))],
        out_shape=[jax.ShapeDtypeStruct((t * SUBLANES, LANES), F32), jax.ShapeDtypeStruct((t, d), BF16),
                   jax.ShapeDtypeStruct((N_EXPERTS, t), F32)],
        compiler_params=_cparams(("arbitrary",)),
        name="merge",
    )(*ins, p, *consts)


def _route_kernel(lg_ref, bias_ref, cls_ref, wa_ref, wb_ref):
    aff = [jax.nn.sigmoid(lg_ref[e]) for e in range(N_EXPERTS)]
    sel = [aff[e] + bias_ref[e] for e in range(N_EXPERTS)]
    n = EXPERTS_PER_GROUP

    def top2_sum(vals):
        best = None
        for a, b in PAIRS:
            s = vals[a] + vals[b]
            best = s if best is None else jnp.maximum(best, s)
        return best

    grp = jnp.zeros(aff[0].shape, jnp.int32)
    best = top2_sum(sel[0:n])
    for g in range(1, N_GROUPS):
        sc = top2_sum(sel[g * n:(g + 1) * n])
        better = sc > best
        grp = jnp.where(better, g, grp)
        best = jnp.where(better, sc, best)
    vs, afs = [], []
    for j in range(n):
        v, a = sel[j], aff[j]
        for g in range(1, N_GROUPS):
            v = jnp.where(grp == g, sel[g * n + j], v)
            a = jnp.where(grp == g, aff[g * n + j], a)
        vs.append(v)
        afs.append(a)
    first = jnp.zeros_like(grp)
    fv = vs[0]
    for j in range(1, n):
        better = vs[j] > fv
        first = jnp.where(better, j, first)
        fv = jnp.where(better, vs[j], fv)
    second = jnp.full_like(grp, -1)
    sv = jnp.full_like(fv, -jnp.inf)
    for j in range(n):
        better = (first != j) & ((second < 0) | (vs[j] > sv))
        second = jnp.where(better, j, second)
        sv = jnp.where(better, vs[j], sv)
    lo = jnp.minimum(first, second)
    hi = jnp.maximum(first, second)
    a_lo, a_hi = afs[0], afs[0]
    for j in range(1, n):
        a_lo = jnp.where(lo == j, afs[j], a_lo)
        a_hi = jnp.where(hi == j, afs[j], a_hi)
    pair = jnp.zeros_like(grp)
    for idx, (a, b) in enumerate(PAIRS):
        pair = jnp.where((lo == a) & (hi == b), idx, pair)
    tot = a_lo + a_hi
    cls_ref[...] = grp * len(PAIRS) + pair
    wa_ref[...] = a_lo / tot
    wb_ref[...] = a_hi / tot


def _route_call(lg3, bias):
    _, rows, lanes = lg3.shape
    blk = pl.BlockSpec((ROUTE_ROWS, lanes), lambda i: (i, 0))
    return pl.pallas_call(
        _route_kernel,
        grid=(rows // ROUTE_ROWS,),
        in_specs=[pl.BlockSpec((N_EXPERTS, ROUTE_ROWS, lanes), lambda i: (0, i, 0)),
                  pl.BlockSpec(memory_space=pltpu.SMEM)],
        out_specs=[blk, blk, blk],
        out_shape=[jax.ShapeDtypeStruct((rows, lanes), jnp.int32),
                   jax.ShapeDtypeStruct((rows, lanes), F32), jax.ShapeDtypeStruct((rows, lanes), F32)],
        compiler_params=_cparams(("arbitrary",)),
        name="route",
    )(lg3, bias)


GATHER_UNROLL = 8
CAST_ROWS = 128


def _moe_kernel(ta_ref, tb_ref, nused_ref, valid_ref,
                h_hbm, tok_ref, tok_next_ref, wa_ref, wb_ref, upa_ref, upb_ref, dna_ref, dnb_ref,
                m_hbm, xbuf, ybuf, w_up, w_dn, gsem, ssem):
    i = pl.program_id(0)
    n_used = nused_ref[0]
    slot = i % 2
    rows = MOE_TILE * SUBLANES

    def row_copy_in(toks, s, r):
        tok = toks[0, 0, r]
        return pltpu.make_async_copy(
            h_hbm.at[pl.ds(pl.multiple_of(tok * SUBLANES, SUBLANES), SUBLANES), :],
            xbuf.at[s, pl.ds(pl.multiple_of(r * SUBLANES, SUBLANES), SUBLANES), :],
            gsem.at[s])

    def row_copy_out(toks, s, r):
        tok = toks[0, 0, r]
        return pltpu.make_async_copy(
            ybuf.at[s, pl.ds(pl.multiple_of(r * SUBLANES, SUBLANES), SUBLANES), :],
            m_hbm.at[pl.ds(pl.multiple_of(tok * SUBLANES, SUBLANES), SUBLANES), :],
            ssem.at[s])

    def start_rows(make, toks, s, count):
        def body8(c, carry):
            for u in range(GATHER_UNROLL):
                make(toks, s, c * GATHER_UNROLL + u).start()
            return carry

        def body1(r, carry):
            make(toks, s, r).start()
            return carry

        full = count // GATHER_UNROLL
        lax.fori_loop(0, full, body8, 0)
        lax.fori_loop(full * GATHER_UNROLL, count, body1, 0)

    def wait_gather(tile, s):
        n = pl.multiple_of(valid_ref[tile] * SUBLANES, SUBLANES)
        pltpu.make_async_copy(h_hbm.at[pl.ds(0, n), :], xbuf.at[s, pl.ds(0, n), :], gsem.at[s]).wait()

    def wait_scatter(tile, s):
        n = pl.multiple_of(valid_ref[tile] * SUBLANES, SUBLANES)
        pltpu.make_async_copy(ybuf.at[s, pl.ds(0, n), :], m_hbm.at[pl.ds(0, n), :], ssem.at[s]).wait()

    @pl.when(i == 0)
    def _():
        xbuf[...] = jnp.zeros(xbuf.shape, F32)
        start_rows(row_copy_in, tok_ref, 0, valid_ref[0])

    @pl.when(i + 1 < n_used)
    def _():
        start_rows(row_copy_in, tok_next_ref, 1 - slot, valid_ref[i + 1])

    def refresh(which, ids_ref, up_ref, dn_ref):
        prev = ids_ref[jnp.maximum(i - 1, 0)]

        @pl.when((i == 0) | (ids_ref[i] != prev))
        def _():
            def cast_up(c, carry):
                r = pl.multiple_of(c * CAST_ROWS, CAST_ROWS)
                w_up[which, pl.ds(r, CAST_ROWS), :] = up_ref[0, pl.ds(r, CAST_ROWS), :].astype(BF16)
                return carry

            def cast_dn(c, carry):
                r = pl.multiple_of(c * CAST_ROWS, CAST_ROWS)
                w_dn[which, pl.ds(r, CAST_ROWS), :] = dn_ref[0, pl.ds(r, CAST_ROWS), :].astype(BF16)
                return carry

            lax.fori_loop(0, up_ref.shape[1] // CAST_ROWS, cast_up, 0)
            lax.fori_loop(0, dn_ref.shape[1] // CAST_ROWS, cast_dn, 0)

    @pl.when(i < n_used)
    def _():
        refresh(0, ta_ref, upa_ref, dna_ref)
        refresh(1, tb_ref, upb_ref, dnb_ref)
        wait_gather(i, slot)
        x = _load_token_major(xbuf.at[slot], MOE_TILE, upa_ref.shape[1]).astype(BF16)

        def ffn(which):
            gu = jnp.dot(x, w_up[which], preferred_element_type=F32)
            hid = jax.nn.silu(gu[:, :D_EXPERT]) * gu[:, D_EXPERT:]
            return jnp.dot(hid.astype(BF16), w_dn[which], preferred_element_type=F32)

        y = ffn(0) * wa_ref[...] + ffn(1) * wb_ref[...]

        @pl.when(i >= 2)
        def _():
            wait_scatter(i - 2, slot)

        _store_token_major(ybuf.at[slot], y)
        start_rows(row_copy_out, tok_ref, slot, valid_ref[i])

        @pl.when(i == n_used - 1)
        def _():
            @pl.when(i >= 1)
            def _():
                wait_scatter(i - 1, 1 - slot)
            wait_scatter(i, slot)


def _moe_call(layer, tile_a, tile_b, n_used, tile_valid, slot_tok, h1_tm, slot_wa, slot_wb, w_up, w_down,
              n_tiles, t):
    d = w_up.shape[1]
    e0 = layer * N_EXPERTS
    wspec = pl.BlockSpec((MOE_TILE, 1), lambda i, *_: (i, 0))
    tok_blk = (1, 1, MOE_TILE)
    grid_spec = pltpu.PrefetchScalarGridSpec(
        num_scalar_prefetch=4,
        grid=(n_tiles,),
        in_specs=[pl.BlockSpec(memory_space=pl.ANY),
                  pl.BlockSpec(tok_blk, lambda i, *_: (i, 0, 0), memory_space=pltpu.SMEM),
                  pl.BlockSpec(tok_blk, lambda i, *_: (jnp.minimum(i + 1, n_tiles - 1), 0, 0),
                               memory_space=pltpu.SMEM),
                  wspec, wspec,
                  pl.BlockSpec((1, d, 2 * D_EXPERT), lambda i, ta, tb, *_: (e0 + ta[i], 0, 0)),
                  pl.BlockSpec((1, d, 2 * D_EXPERT), lambda i, ta, tb, *_: (e0 + tb[i], 0, 0)),
                  pl.BlockSpec((1, D_EXPERT, d), lambda i, ta, tb, *_: (e0 + ta[i], 0, 0)),
                  pl.BlockSpec((1, D_EXPERT, d), lambda i, ta, tb, *_: (e0 + tb[i], 0, 0))],
        out_specs=pl.BlockSpec(memory_space=pl.ANY),
        scratch_shapes=[pltpu.VMEM((2, MOE_TILE * SUBLANES, LANES), F32),
                        pltpu.VMEM((2, MOE_TILE * SUBLANES, LANES), F32),
                        pltpu.VMEM((2, d, 2 * D_EXPERT), BF16), pltpu.VMEM((2, D_EXPERT, d), BF16),
                        pltpu.SemaphoreType.DMA((2,)), pltpu.SemaphoreType.DMA((2,))])
    return pl.pallas_call(
        _moe_kernel,
        grid_spec=grid_spec,
        out_shape=jax.ShapeDtypeStruct((t * SUBLANES, LANES), F32),
        compiler_params=_cparams(("arbitrary",)),
        name="moe",
    )(tile_a, tile_b, n_used, tile_valid, h1_tm, slot_tok, slot_tok, slot_wa, slot_wb, w_up, w_up, w_down, w_down)


def _final_kernel(alpha, h1_ref, m_ref, e_ref, g_ref, b_ref, o_ref):
    rows, d = o_ref.shape
    h1 = _load_token_major(h1_ref, rows, d)
    m = _load_token_major(m_ref, rows, d)
    o_ref[...] = _layer_norm(alpha * h1 + m + e_ref[...].astype(F32), g_ref[...], b_ref[...])


def _final_call(alpha, h1_tm, m_tm, e, g, b):
    t, d = e.shape
    tm = TM_FINAL
    row = lambda i: (i, 0)
    tmaj = pl.BlockSpec((tm * SUBLANES, LANES), row)
    return pl.pallas_call(
        functools.partial(_final_kernel, alpha),
        grid=(t // tm,),
        in_specs=[tmaj, tmaj, pl.BlockSpec((tm, d), row), _full(g.shape), _full(b.shape)],
        out_specs=pl.BlockSpec((tm, d), row),
        out_shape=jax.ShapeDtypeStruct((t, d), F32),
        compiler_params=_cparams(("arbitrary",)),
        name="final_ln",
    )(h1_tm, m_tm, e, g, b)


def _pack_w_in(w_in):
    off_q = OFF_CONV + 2 * D_CONV
    off_kr = off_q + Q_LORA + KV_LORA
    off_gate = off_kr + QK_ROPE
    kr = jnp.pad(w_in[..., off_kr:off_gate], ((0, 0), (0, 0), (ROPE_LANE0, HEAD_BLOCK - ROPE_LANE0 - QK_ROPE)))
    return w_in[..., :off_kr].astype(BF16), kr.astype(BF16), w_in[..., off_gate:].astype(BF16)


def _pack_heads(w, lo, width, stride, lane0):
    blocks = []
    for h in range(N_HEADS):
        l0 = lane0(h)
        blocks.append(jnp.pad(w[..., h * stride + lo:h * stride + lo + width],
                              ((0, 0), (0, 0), (l0, HEAD_BLOCK - l0 - width))))
    return jnp.concatenate(blocks, axis=-1).astype(BF16)


def _rope_tables(positions):
    inv_freq = jnp.power(ROPE_THETA, -jnp.arange(0, QK_ROPE, 2, dtype=F32) / QK_ROPE)
    ang = positions.astype(F32).reshape(-1, 1) * inv_freq
    cos, sin = jnp.cos(ang), jnp.sin(ang)
    t = ang.shape[0]
    ones_lo = jnp.ones((t, ROPE_LANE0), F32)
    ones_hi = jnp.ones((t, HEAD_BLOCK - ROPE_LANE0 - QK_ROPE), F32)
    zeros_lo = jnp.zeros((t, ROPE_LANE0), F32)
    zeros_half = jnp.zeros((t, HALF_ROPE), F32)
    zeros_hi = jnp.zeros((t, HEAD_BLOCK - ROPE_LANE0 - QK_ROPE), F32)
    c = jnp.concatenate([ones_lo, cos, cos, ones_hi], axis=-1)
    s1 = jnp.concatenate([zeros_lo, -sin, zeros_half, zeros_hi], axis=-1)
    s2 = jnp.concatenate([zeros_lo, zeros_half, sin, zeros_hi], axis=-1)
    return c, s1, s2


def _routing_plan(cls, wa, wb, n_tiles):
    t = cls.shape[0]
    n_pad = n_tiles * MOE_TILE - t
    assert n_pad == N_CLASSES * MOE_TILE, n_pad
    classes = jnp.arange(N_CLASSES, dtype=jnp.int32)
    counts = jnp.sum((cls[None, :] == classes[:, None]).astype(jnp.int32), axis=1)
    padded = (counts + MOE_TILE - 1) // MOE_TILE * MOE_TILE
    pad_end = jnp.cumsum(padded)
    pad_start = pad_end - padded
    pad_need = padded - counts
    j = jnp.arange(MOE_TILE, dtype=jnp.int32)[None, :]
    pad_key = jnp.where(j < pad_need[:, None], 2 * classes[:, None] + 1, 2 * N_CLASSES).reshape(-1)
    zeros_i = jnp.zeros((n_pad,), jnp.int32)
    zeros_f = jnp.zeros((n_pad,), F32)
    _, slot_tok, slot_wa, slot_wb = lax.sort(
        (jnp.concatenate([2 * cls, pad_key]), jnp.concatenate([jnp.arange(t, dtype=jnp.int32), zeros_i]),
         jnp.concatenate([wa, zeros_f]), jnp.concatenate([wb, zeros_f])), num_keys=1)
    slot_tok = slot_tok.reshape(n_tiles, 1, MOE_TILE)
    slot_wa = slot_wa.reshape(-1, 1)
    slot_wb = slot_wb.reshape(-1, 1)
    tile_row0 = jnp.arange(n_tiles, dtype=jnp.int32) * MOE_TILE
    tile_cls = jnp.minimum(jnp.sum((tile_row0[:, None] >= pad_end[None, :]).astype(jnp.int32), axis=1),
                           N_CLASSES - 1)
    onehot = (tile_cls[:, None] == classes[None, :]).astype(jnp.int32)
    tile_valid = jnp.clip(jnp.sum(onehot * (pad_start + counts)[None, :], axis=1) - tile_row0,
                          0, MOE_TILE).astype(jnp.int32)
    pair_lo = jnp.array([p[0] for p in PAIRS], jnp.int32)
    pair_hi = jnp.array([p[1] for p in PAIRS], jnp.int32)
    grp = tile_cls // len(PAIRS)
    pair_onehot = ((tile_cls % len(PAIRS))[:, None] == jnp.arange(len(PAIRS), dtype=jnp.int32)[None, :])
    tile_a = grp * EXPERTS_PER_GROUP + jnp.sum(pair_onehot * pair_lo[None, :], axis=1)
    tile_b = grp * EXPERTS_PER_GROUP + jnp.sum(pair_onehot * pair_hi[None, :], axis=1)
    n_used = (pad_end[-1] // MOE_TILE).astype(jnp.int32).reshape(1)
    return tile_a, tile_b, n_used, tile_valid, slot_tok, slot_wa, slot_wb


def kernel(x, p, positions, ln_in_g, ln_in_b, w_in, b_gate, pool_w, pool_scale, pool_proj, conv_dw, conv_b, conv_ln_g, conv_ln_b, conv_proj, q_norm_g, w_uq, kv_norm_g, w_ukv, mla_proj, w_out, ln1_g, ln1_b, w_router, router_bias, exp_w_up, exp_w_down, ple_proj, ple_gate, ln2_g, ln2_b):
    batch, seq, d = x.shape
    depth = w_in.shape[0]
    t = batch * seq
    alpha = float((2 * depth) ** 0.25)
    n_tiles = (t + N_CLASSES * (MOE_TILE - 1)) // MOE_TILE + 1

    w_in_p = _pack_w_in(w_in)
    pool_w_b = pool_w.astype(BF16)
    pool_proj_b, conv_proj_b, mla_proj_b = pool_proj.astype(BF16), conv_proj.astype(BF16), mla_proj.astype(BF16)
    qk = QK_NOPE + QK_ROPE
    wq_p = (_pack_heads(w_uq, 0, QK_NOPE, qk, lambda h: 0)
            + _pack_heads(w_uq, QK_NOPE, QK_ROPE, qk, lambda h: ROPE_LANE0))
    wqt_p = jnp.swapaxes(wq_p, 1, 2)
    wk_p = _pack_heads(w_ukv, 0, QK_NOPE, QK_NOPE + V_HEAD, lambda h: 0)
    kv_w = QK_NOPE + V_HEAD
    wvt_p = jnp.swapaxes(jnp.concatenate(
        [jnp.pad(w_ukv[..., h * kv_w + QK_NOPE:(h + 1) * kv_w], ((0, 0), (0, 0), (0, BF16_ROWS)))
         for h in range(N_HEADS)], axis=-1), 1, 2).astype(BF16)
    ones_rows = jnp.tile(jnp.concatenate([jnp.zeros((V_HEAD, 1), F32), jnp.ones((BF16_ROWS, 1), F32)]),
                         (N_HEADS, 1))
    w_out_b, ple_proj_b, ple_gate_b = w_out.astype(BF16), ple_proj.astype(BF16), ple_gate.astype(BF16)
    w_up_b = exp_w_up.reshape((depth * N_EXPERTS,) + exp_w_up.shape[2:])
    w_down_b = exp_w_down.reshape((depth * N_EXPERTS,) + exp_w_down.shape[2:])
    wr_t = w_router.T
    wr_hi = wr_t.astype(BF16)
    wr_lo = (wr_t - wr_hi.astype(F32)).astype(BF16)
    rope_tabs = _rope_tables(positions)
    rope_tabs_t = tuple(a.T for a in rope_tabs)
    p2 = p.reshape(depth, t, -1)

    pre, pre_g, pre_b = (x.reshape(t, d),), ln_in_g, ln_in_b
    for i in range(depth):
        branch_params = (conv_dw[i], conv_b[i].reshape(1, -1), conv_ln_g[i].reshape(1, -1),
                         conv_ln_b[i].reshape(1, -1), pool_w_b[i], pool_scale[i].reshape(1, -1))
        mla_params = (rope_tabs, rope_tabs_t, q_norm_g[i].reshape(1, -1), kv_norm_g[i].reshape(1, -1),
                      wqt_p, wk_p, wvt_p, ones_rows)
        h, a_pool, a_conv, qt, k, vt, glog = _in_proj_call(alpha, pre, pre_g.reshape(1, d), pre_b.reshape(1, d),
                                                           w_in_p, branch_params, mla_params, i, seq)
        a_attn = _attn_call(qt, k, vt, batch, seq)
        h1_tm, e, logits = _merge_call(alpha, i, a_pool, a_conv, a_attn, glog, h, p2,
                                       pool_proj_b, conv_proj_b, mla_proj_b, b_gate[i], w_out_b,
                                       ln1_g[i].reshape(1, d), ln1_b[i].reshape(1, d),
                                       ple_proj_b, ple_gate_b, wr_hi, wr_lo)
        cls, wa, wb = _route_call(logits.reshape(N_EXPERTS, t // LANES, LANES), router_bias)
        plan = _routing_plan(cls.reshape(t), wa.reshape(t), wb.reshape(t), n_tiles)
        m_tm = _moe_call(i, *plan[:5], h1_tm, plan[5], plan[6], w_up_b, w_down_b, n_tiles, t)
        pre, pre_g, pre_b = (h1_tm, m_tm, e), ln2_g[i], ln2_b[i]
    h = _final_call(alpha, *pre, pre_g.reshape(1, d), pre_b.reshape(1, d))
    return h.reshape(batch, seq, d)
```

```python
import functools

import jax
import jax.numpy as jnp
from jax import lax
from jax.experimental import pallas as pl
from jax.experimental.pallas import tpu as pltpu

F32 = jnp.float32
BF16 = jnp.bfloat16

CHUNK = 64
POOL_WINDOWS = (2, 4, 8, 16)
POOL_GW = 128
D_POOL = 512
D_CONV = 512
CONV_WIDTH = 31
N_HEADS = 8
QK_NOPE = 64
QK_ROPE = 32
V_HEAD = 64
Q_LORA = 384
KV_LORA = 256
ROPE_THETA = 10000.0
N_EXPERTS = 16
N_GROUPS = 4
EXPERTS_PER_GROUP = 4
D_EXPERT = 512
LN_EPS = 1e-5
RMS_EPS = 1e-6

LANES = 128
SUBLANES = 8
HEAD_BLOCK = LANES
ROPE_LANE0 = QK_NOPE
HALF_ROPE = QK_ROPE // 2
BF16_ROWS = 16
VT_BLOCK = V_HEAD + BF16_ROWS
LOG2_E = 1.4426950408889634

TM_PROJ = 512
TM_MERGE = 512
MERGE_SUB = 256
TM_FINAL = 512
TQ = 256
TK = 256
Q_SUBS = 2
ATTN_LOOKAHEAD = 3
POOL_CHUNK = 256
POOL_HALO = 16
CONV_CHUNK = 128
CONV_HALO = 32
MOE_TILE = 256
ROUTE_ROWS = 8

PAIRS = ((0, 1), (0, 2), (0, 3), (1, 2), (1, 3), (2, 3))
N_CLASSES = N_GROUPS * len(PAIRS)

VMEM_LIMIT = 56 * 1024 * 1024


def _cparams(sem):
    return pltpu.CompilerParams(dimension_semantics=sem, vmem_limit_bytes=VMEM_LIMIT)


def _layer_norm(x, g, b):
    mu = jnp.mean(x, axis=-1, keepdims=True)
    xc = x - mu
    var = jnp.mean(xc * xc, axis=-1, keepdims=True)
    return xc * lax.rsqrt(var + LN_EPS) * g + b


def _rms_norm(x, g):
    ms = jnp.mean(x * x, axis=-1, keepdims=True)
    return x * lax.rsqrt(ms + RMS_EPS) * g


def _sigmoid(x):
    return 0.5 * jnp.tanh(0.5 * x) + 0.5


def _full(shape):
    n = len(shape)
    return pl.BlockSpec(shape, lambda *_: (0,) * n)


def _layer_full(stacked, layer):
    n = stacked.ndim - 1
    return pl.BlockSpec((None,) + stacked.shape[1:], lambda *_: (layer,) + (0,) * n)


OFF_CONV = D_POOL
OFF_CQKV = OFF_CONV + 2 * D_CONV
W_CQKV = Q_LORA + KV_LORA + HEAD_BLOCK
OFF_GATE = OFF_CQKV + W_CQKV
N_COLS_CHUNK = 512


def _in_proj_kernel(alpha, residual, tiles_per_seq, *refs):
    n_pre = 5 if residual else 3
    if residual:
        h1_ref, m_ref, e_ref, g_ref, b_ref = refs[:n_pre]
    else:
        x_ref, g_ref, b_ref = refs[:n_pre]
    w_ref, dw_ref, cb_ref, cg_ref, cbb_ref, pw_ref, pscale_ref = refs[n_pre:n_pre + 7]
    mla_refs = refs[n_pre + 7:-9]
    h_ref, apool_ref, aconv_ref, qt_ref, k_ref, vt_ref, gate_ref, zs, ps = refs[-9:]
    rows, d = h_ref.shape
    seq_tile = pl.program_id(0) % tiles_per_seq

    @pl.when(seq_tile == 0)
    def _():
        zs[0:CONV_HALO, :] = jnp.zeros((CONV_HALO, D_CONV), F32)
        ps[0:POOL_HALO, :] = jnp.zeros((POOL_HALO, D_POOL), F32)

    half = rows // 2
    xs = []
    for k in range(2):
        r0 = k * half
        if residual:
            pre = (alpha * _load_token_major(h1_ref.at[pl.ds(r0 * SUBLANES, half * SUBLANES), :], half, d)
                   + _load_token_major(m_ref.at[pl.ds(r0 * SUBLANES, half * SUBLANES), :], half, d))
            pre = pre + e_ref[r0:r0 + half, :].astype(F32)
        else:
            pre = x_ref[r0:r0 + half, :]
        h = _layer_norm(pre, g_ref[...], b_ref[...])
        h_ref[r0:r0 + half, :] = h
        xs.append(h.astype(BF16))

    def mm(lo, hi):
        return jnp.concatenate([jnp.dot(xk, w_ref[:, lo:hi], preferred_element_type=F32) for xk in xs], axis=0)

    zs[CONV_HALO:CONV_HALO + rows, :] = jax.nn.sigmoid(mm(OFF_CONV + D_CONV, OFF_CQKV))
    zs[CONV_HALO:CONV_HALO + rows, :] = mm(OFF_CONV, OFF_CONV + D_CONV) * zs[CONV_HALO:CONV_HALO + rows, :]
    ps[POOL_HALO:POOL_HALO + rows, :] = mm(0, OFF_CONV)
    cqkv = mm(OFF_CQKV, OFF_GATE)

    def mla_work():
        _mla_project(cqkv, *mla_refs, qt_ref, k_ref, vt_ref)

    def conv_work(r0):
        aconv_ref[r0:r0 + CONV_CHUNK, :] = _conv_chunk(zs, r0, dw_ref, cb_ref, cg_ref, cbb_ref).astype(BF16)

    def pool_work(r0):
        _pool_chunk(ps, r0, seq_tile * rows, pw_ref, pscale_ref, apool_ref)

    work = [mla_work] + [functools.partial(conv_work, r0) for r0 in range(0, rows, CONV_CHUNK)]
    work += [functools.partial(pool_work, r0) for r0 in range(0, rows, POOL_CHUNK)]
    n_gate_chunks = gate_ref.shape[1] // N_COLS_CHUNK
    for c in range(n_gate_chunks):
        gate_ref[:, c * N_COLS_CHUNK:(c + 1) * N_COLS_CHUNK] = mm(
            OFF_GATE + c * N_COLS_CHUNK, OFF_GATE + (c + 1) * N_COLS_CHUNK).astype(BF16)
        for item in work[len(work) * c // n_gate_chunks:len(work) * (c + 1) // n_gate_chunks]:
            item()
    zs[0:CONV_HALO, :] = zs[rows:rows + CONV_HALO, :]
    ps[0:POOL_HALO, :] = ps[rows:rows + POOL_HALO, :]


def _in_proj_call(alpha, pre, g, b, w, branch_params, mla_params, layer, seq):
    residual = len(pre) == 3
    t, d = pre[-1].shape
    n = w.shape[-1]
    n_gate = n - OFF_GATE
    nh = N_HEADS * HEAD_BLOCK
    nv = N_HEADS * VT_BLOCK
    row = lambda i: (i, 0)
    col = lambda i: (0, i)
    tmaj = pl.BlockSpec((TM_PROJ * SUBLANES, LANES), row)
    pre_specs = [tmaj, tmaj, pl.BlockSpec((TM_PROJ, d), row)] if residual else [pl.BlockSpec((TM_PROJ, d), row)]
    tabs, tabs_t, qg, kvg, wqt, wk, wvt, ones_rows = mla_params
    mla_specs = ([pl.BlockSpec((TM_PROJ, LANES), row)] * 3 + [pl.BlockSpec((LANES, TM_PROJ), col)] * 3
                 + [_full(qg.shape), _full(kvg.shape), _layer_full(wqt, layer), _layer_full(wk, layer),
                    _layer_full(wvt, layer), _full(ones_rows.shape)])
    return pl.pallas_call(
        functools.partial(_in_proj_kernel, alpha, residual, seq // TM_PROJ),
        grid=(t // TM_PROJ,),
        in_specs=pre_specs + [_full(g.shape), _full(b.shape), _layer_full(w, layer)]
        + [_full(a.shape) for a in branch_params] + mla_specs,
        out_specs=[pl.BlockSpec((TM_PROJ, d), row),
                   pl.BlockSpec((TM_PROJ, D_POOL), row), pl.BlockSpec((TM_PROJ, D_CONV), row),
                   pl.BlockSpec((nh, TM_PROJ), col), pl.BlockSpec((TM_PROJ, nh), row),
                   pl.BlockSpec((nv, TM_PROJ), col), pl.BlockSpec((TM_PROJ, n_gate), row)],
        out_shape=[jax.ShapeDtypeStruct((t, d), F32),
                   jax.ShapeDtypeStruct((t, D_POOL), BF16), jax.ShapeDtypeStruct((t, D_CONV), BF16),
                   jax.ShapeDtypeStruct((nh, t), BF16), jax.ShapeDtypeStruct((t, nh), BF16),
                   jax.ShapeDtypeStruct((nv, t), BF16), jax.ShapeDtypeStruct((t, n_gate), BF16)],
        scratch_shapes=[pltpu.VMEM((TM_PROJ + CONV_HALO, D_CONV), F32),
                        pltpu.VMEM((TM_PROJ + POOL_HALO, D_POOL), F32)],
        compiler_params=_cparams(("arbitrary",)),
        name="in_proj",
    )(*pre, g, b, w, *branch_params, *tabs, *tabs_t, qg, kvg, wqt, wk, wvt, ones_rows)


def _pool_chunk(ps, r0, seq_row0, w_ref, scale_ref, o_ref):
    t = seq_row0 + r0 + lax.broadcasted_iota(jnp.int32, (POOL_CHUNK, 1), 0)
    for g, w in enumerate(POOL_WINDOWS):
        cols = slice(g * POOL_GW, (g + 1) * POOL_GW)
        xw = ps[r0:r0 + POOL_CHUNK + POOL_HALO, cols]
        acc = xw
        k = 1
        while k < w:
            acc = acc + pltpu.roll(acc, k, axis=0)
            k *= 2
        cnt = jnp.minimum(t + 1, w).astype(F32)
        mixed = acc[POOL_HALO:] / cnt - xw[POOL_HALO:]
        y = jnp.dot(mixed.astype(BF16), w_ref[g], preferred_element_type=F32) * scale_ref[:, cols]
        o_ref[r0:r0 + POOL_CHUNK, cols] = y.astype(BF16)


def _conv_chunk(zs, r0, dw_ref, cb_ref, g_ref, b_ref):
    parts = []
    for cg in range(D_CONV // LANES):
        cols = slice(cg * LANES, (cg + 1) * LANES)
        win = zs[r0:r0 + CONV_CHUNK + CONV_HALO, cols]
        acc = jnp.zeros((CONV_CHUNK, LANES), F32) + cb_ref[:, cols]
        for sub in range(SUBLANES):
            shifted = win if sub == 0 else pltpu.roll(win, sub, axis=0)
            for a in range(CONV_HALO // SUBLANES):
                lag = SUBLANES * a + sub
                if lag >= CONV_WIDTH:
                    continue
                k = CONV_WIDTH - 1 - lag
                lo = CONV_HALO - SUBLANES * a
                acc = acc + dw_ref[k:k + 1, cols] * shifted[lo:lo + CONV_CHUNK]
        parts.append(acc)
    y = _layer_norm(jnp.concatenate(parts, axis=-1), g_ref[...], b_ref[...])
    return y * jax.nn.sigmoid(y)


def _rope_block(x, c, s1, s2):
    return x * c + pltpu.roll(x, LANES - HALF_ROPE, axis=1) * s1 + pltpu.roll(x, HALF_ROPE, axis=1) * s2


def _rope_block_t(x, c, s1, s2):
    return x * c + pltpu.roll(x, HEAD_BLOCK - HALF_ROPE, axis=0) * s1 + pltpu.roll(x, HALF_ROPE, axis=0) * s2


def _mla_project(cqkv, c_ref, s1_ref, s2_ref, ct_ref, s1t_ref, s2t_ref, qg_ref, kvg_ref,
                 wqt_ref, wk_ref, wvt_ref, ones_ref, qt_ref, k_ref, vt_ref):
    cq = _rms_norm(cqkv[:, 0:Q_LORA], qg_ref[...]).astype(BF16)
    ckv = _rms_norm(cqkv[:, Q_LORA:Q_LORA + KV_LORA], kvg_ref[...]).astype(BF16)
    kr = _rope_block(cqkv[:, Q_LORA + KV_LORA:W_CQKV], c_ref[...], s1_ref[...], s2_ref[...])
    scale = float((QK_NOPE + QK_ROPE) ** -0.5 * LOG2_E)
    nt = (((1,), (1,)), ((), ()))
    ct, s1t, s2t = ct_ref[...], s1t_ref[...], s2t_ref[...]
    for h in range(N_HEADS):
        cols = slice(h * HEAD_BLOCK, (h + 1) * HEAD_BLOCK)
        qt = lax.dot_general(wqt_ref[cols, :], cq, nt, preferred_element_type=F32) * scale
        qt_ref[cols, :] = _rope_block_t(qt, ct, s1t, s2t).astype(BF16)
        k = jnp.dot(ckv, wk_ref[:, cols], preferred_element_type=F32) + kr
        k_ref[:, cols] = k.astype(BF16)
    vt = lax.dot_general(wvt_ref[...], ckv, nt, preferred_element_type=F32) + ones_ref[...]
    vt_ref[...] = vt.astype(BF16)


def _attn_kernel(qt_ref, k_ref, vt_ref, o_ref, m_ref, acc_ref):
    j = pl.program_id(1)
    key_chunk = lax.broadcasted_iota(jnp.int32, (TK, TQ), 0) // CHUNK
    qry_chunk = lax.broadcasted_iota(jnp.int32, (TK, TQ), 1) // CHUNK
    diag_mask = key_chunk <= qry_chunk

    def scores(h, sub, kt):
        k0 = pl.multiple_of(kt * TK, TK)
        kk = k_ref[pl.ds(k0, TK), h * HEAD_BLOCK:(h + 1) * HEAD_BLOCK]
        return jnp.dot(kk, qt_ref[h * HEAD_BLOCK:(h + 1) * HEAD_BLOCK, sub * TQ:(sub + 1) * TQ],
                       preferred_element_type=F32)

    def values_t(h, kt):
        k0 = pl.multiple_of(kt * TK, TK)
        return vt_ref[h * VT_BLOCK:(h + 1) * VT_BLOCK, pl.ds(k0, TK)]

    def first_update(h, sub, kt, masked, s):
        st = h * Q_SUBS + sub
        if masked:
            s = jnp.where(diag_mask, s, -jnp.inf)
        m = jnp.max(s, axis=0, keepdims=True)
        p = jnp.exp2(s - m)
        m_ref[st] = jnp.broadcast_to(m, (SUBLANES, TQ))
        acc_ref[st] = jnp.dot(values_t(h, kt), p.astype(BF16), preferred_element_type=F32)

    def update(h, sub, kt, masked, s):
        st = h * Q_SUBS + sub
        if masked:
            s = jnp.where(diag_mask, s, -jnp.inf)
        m_old = m_ref[st]
        m_new = jnp.maximum(m_old, jnp.max(s, axis=0, keepdims=True))
        alpha = jnp.exp2(m_old - m_new)
        p = jnp.exp2(s - m_new[0:1, :])
        m_ref[st] = m_new
        acc_ref[st] = alpha[0:1, :] * acc_ref[st] + jnp.dot(values_t(h, kt), p.astype(BF16),
                                                           preferred_element_type=F32)

    def run(work):
        pending = {}
        for idx in range(len(work) + ATTN_LOOKAHEAD):
            if idx < len(work):
                h, sub, kt, _, _ = work[idx]
                pending[idx] = scores(h, sub, kt)
            if idx >= ATTN_LOOKAHEAD:
                h, sub, kt, masked, upd = work[idx - ATTN_LOOKAHEAD]
                upd(h, sub, kt, masked, pending.pop(idx - ATTN_LOOKAHEAD))

    tail = []
    for h in range(N_HEADS):
        for sub in range(Q_SUBS):
            for kk in range(sub + 1):
                tail.append((h, sub, Q_SUBS * j + kk, kk == sub, first_update if kk == 0 else update))
    run(tail)

    def step(kt, carry):
        run([(h, sub, kt, False, update) for h in range(N_HEADS) for sub in range(Q_SUBS)])
        return carry

    lax.fori_loop(0, Q_SUBS * j, step, 0)

    for sub in range(Q_SUBS):
        outs = []
        for h in range(N_HEADS):
            st = h * Q_SUBS + sub
            outs.append(acc_ref[st, 0:V_HEAD, :] * (1.0 / acc_ref[st, V_HEAD:V_HEAD + 1, :]))
        o_ref[sub * TQ:(sub + 1) * TQ, :] = jnp.concatenate(outs, axis=0).T.astype(BF16)


def _attn_call(qt, k, vt, batch, seq):
    n = N_HEADS * HEAD_BLOCK
    tq = Q_SUBS * TQ
    nq = seq // tq
    return pl.pallas_call(
        _attn_kernel,
        grid=(batch, nq),
        in_specs=[pl.BlockSpec((n, tq), lambda b, i: (0, b * nq + i)),
                  pl.BlockSpec((seq, n), lambda b, i: (b, 0)),
                  pl.BlockSpec((N_HEADS * VT_BLOCK, seq), lambda b, i: (0, b))],
        out_specs=pl.BlockSpec((tq, N_HEADS * V_HEAD), lambda b, i: (b * nq + i, 0)),
        out_shape=jax.ShapeDtypeStruct((k.shape[0], N_HEADS * V_HEAD), BF16),
        scratch_shapes=[pltpu.VMEM((N_HEADS * Q_SUBS, SUBLANES, TQ), F32),
                        pltpu.VMEM((N_HEADS * Q_SUBS, VT_BLOCK, TQ), F32)],
        compiler_params=_cparams(("arbitrary", "arbitrary")),
        name="attn",
    )(qt, k, vt)


def _store_token_major(ref, x):
    rows = x.shape[0]
    for j in range(x.shape[1] // LANES):
        ref[pl.ds(j, rows, stride=SUBLANES), :] = x[:, j * LANES:(j + 1) * LANES]


def _load_token_major(ref, rows, d):
    return jnp.concatenate([ref[pl.ds(j, rows, stride=SUBLANES), :] for j in range(d // LANES)], axis=-1)


def _merge_kernel(alpha, ap_ref, ac_ref, at_ref, gl_ref, h_ref, p_ref, pp_ref, cp_ref, mp_ref, bg_ref,
                  wo_ref, g_ref, b_ref, pproj_ref, pgate_ref, wrh_ref, wrl_ref,
                  h1_ref, e_ref, lg_ref):
    d = h_ref.shape[1]
    sub = MERGE_SUB
    parts = [pl.ds(k * sub, sub) for k in range(h_ref.shape[0] // sub)]
    nt = (((1,), (1,)), ((), ()))

    def merged_of(rows):
        merged = None
        for br, (a_ref, w_ref) in enumerate(((ap_ref, pp_ref), (ac_ref, cp_ref), (at_ref, mp_ref))):
            y = jnp.dot(a_ref[rows, :], w_ref[...], preferred_element_type=F32)
            gate = _sigmoid(gl_ref[rows, br * d:(br + 1) * d].astype(F32) + bg_ref[br:br + 1, :])
            merged = gate * y if merged is None else merged + gate * y
        return merged.astype(BF16)

    merged = [merged_of(rows) for rows in parts]
    ys = [jnp.dot(m, wo_ref[...], preferred_element_type=F32) for m in merged]
    es = [jnp.dot(p_ref[rows, :].astype(BF16), pproj_ref[...], preferred_element_type=F32) for rows in parts]
    h1s = [_layer_norm(alpha * h_ref[rows, :] + y, g_ref[...], b_ref[...]) for rows, y in zip(parts, ys)]
    for k, (rows, h1, e) in enumerate(zip(parts, h1s, es)):
        h1b = h1.astype(BF16)
        e = e * _sigmoid(jnp.dot(h1b, pgate_ref[...], preferred_element_type=F32))
        e_ref[rows, :] = e.astype(BF16)
        h1l = (h1 - h1b.astype(F32)).astype(BF16)
        lg = lax.dot_general(wrh_ref[...], h1b, nt, preferred_element_type=F32)
        lg = lg + lax.dot_general(wrh_ref[...], h1l, nt, preferred_element_type=F32)
        lg = lg + lax.dot_general(wrl_ref[...], h1b, nt, preferred_element_type=F32)
        lg_ref[:, k * sub:(k + 1) * sub] = lg
        _store_token_major(h1_ref.at[pl.ds(k * sub * SUBLANES, sub * SUBLANES), :], h1)


def _merge_call(alpha, layer, ap, ac, at, gl, h, p, pp, cp, mp, bg, wo, g, b, pproj, pgate, wrh, wrl):
    t, d = h.shape
    tm = TM_MERGE
    row = lambda i: (i, 0)
    ins = [ap, ac, at, gl, h]
    in_specs = [pl.BlockSpec((tm, a.shape[1]), row) for a in ins]
    in_specs.append(pl.BlockSpec((None, tm, p.shape[-1]), lambda i: (layer, i, 0)))
    stacked = {id(a) for a in (pp, cp, mp, wo, pproj, pgate)}
    consts = [pp, cp, mp, bg, wo, g, b, pproj, pgate, wrh, wrl]
    in_specs += [_layer_full(a, layer) if id(a) in stacked else _full(a.shape) for a in consts]
    return pl.pallas_call(
        functools.partial(_merge_kernel, alpha),
        grid=(t // tm,),
        in_specs=in_specs,
        out_specs=[pl.BlockSpec((tm * SUBLANES, LANES), row), pl.BlockSpec((tm, d), row),
                   pl.BlockSpec((N_EXPERTS, tm), lambda i: (0, i))],
        out_shape=[jax.ShapeDtypeStruct((t * SUBLANES, LANES), F32), jax.ShapeDtypeStruct((t, d), BF16),
                   jax.ShapeDtypeStruct((N_EXPERTS, t), F32)],
        compiler_params=_cparams(("arbitrary",)),
        name="merge",
    )(*ins, p, *consts)


def _route_kernel(lg_ref, bias_ref, cls_ref, wa_ref, wb_ref):
    aff = [jax.nn.sigmoid(lg_ref[e]) for e in range(N_EXPERTS)]
    sel = [aff[e] + bias_ref[e] for e in range(N_EXPERTS)]
    n = EXPERTS_PER_GROUP

    def top2_sum(vals):
        best = None
        for a, b in PAIRS:
            s = vals[a] + vals[b]
            best = s if best is None else jnp.maximum(best, s)
        return best

    grp = jnp.zeros(aff[0].shape, jnp.int32)
    best = top2_sum(sel[0:n])
    for g in range(1, N_GROUPS):
        sc = top2_sum(sel[g * n:(g + 1) * n])
        better = sc > best
        grp = jnp.where(better, g, grp)
        best = jnp.where(better, sc, best)
    vs, afs = [], []
    for j in range(n):
        v, a = sel[j], aff[j]
        for g in range(1, N_GROUPS):
            v = jnp.where(grp == g, sel[g * n + j], v)
            a = jnp.where(grp == g, aff[g * n + j], a)
        vs.append(v)
        afs.append(a)
    first = jnp.zeros_like(grp)
    fv = vs[0]
    for j in range(1, n):
        better = vs[j] > fv
        first = jnp.where(better, j, first)
        fv = jnp.where(better, vs[j], fv)
    second = jnp.full_like(grp, -1)
    sv = jnp.full_like(fv, -jnp.inf)
    for j in range(n):
        better = (first != j) & ((second < 0) | (vs[j] > sv))
        second = jnp.where(better, j, second)
        sv = jnp.where(better, vs[j], sv)
    lo = jnp.minimum(first, second)
    hi = jnp.maximum(first, second)
    a_lo, a_hi = afs[0], afs[0]
    for j in range(1, n):
        a_lo = jnp.where(lo == j, afs[j], a_lo)
        a_hi = jnp.where(hi == j, afs[j], a_hi)
    pair = jnp.zeros_like(grp)
    for idx, (a, b) in enumerate(PAIRS):
        pair = jnp.where((lo == a) & (hi == b), idx, pair)
    tot = a_lo + a_hi
    cls_ref[...] = grp * len(PAIRS) + pair
    wa_ref[...] = a_lo / tot
    wb_ref[...] = a_hi / tot


def _route_call(lg3, bias):
    _, rows, lanes = lg3.shape
    blk = pl.BlockSpec((ROUTE_ROWS, lanes), lambda i: (i, 0))
    return pl.pallas_call(
        _route_kernel,
        grid=(rows // ROUTE_ROWS,),
        in_specs=[pl.BlockSpec((N_EXPERTS, ROUTE_ROWS, lanes), lambda i: (0, i, 0)),
                  pl.BlockSpec(memory_space=pltpu.SMEM)],
        out_specs=[blk, blk, blk],
        out_shape=[jax.ShapeDtypeStruct((rows, lanes), jnp.int32),
                   jax.ShapeDtypeStruct((rows, lanes), F32), jax.ShapeDtypeStruct((rows, lanes), F32)],
        compiler_params=_cparams(("arbitrary",)),
        name="route",
    )(lg3, bias)


GATHER_UNROLL = 8
CAST_ROWS = 128


def _moe_kernel(ta_ref, tb_ref, nused_ref, valid_ref,
                h_hbm, tok_ref, tok_next_ref, wa_ref, wb_ref, upa_ref, upb_ref, dna_ref, dnb_ref,
                m_hbm, xbuf, ybuf, w_up, w_dn, gsem, ssem):
    i = pl.program_id(0)
    n_used = nused_ref[0]
    slot = i % 2
    rows = MOE_TILE * SUBLANES

    def row_copy_in(toks, s, r):
        tok = toks[0, 0, r]
        return pltpu.make_async_copy(
            h_hbm.at[pl.ds(pl.multiple_of(tok * SUBLANES, SUBLANES), SUBLANES), :],
            xbuf.at[s, pl.ds(pl.multiple_of(r * SUBLANES, SUBLANES), SUBLANES), :],
            gsem.at[s])

    def row_copy_out(toks, s, r):
        tok = toks[0, 0, r]
        return pltpu.make_async_copy(
            ybuf.at[s, pl.ds(pl.multiple_of(r * SUBLANES, SUBLANES), SUBLANES), :],
            m_hbm.at[pl.ds(pl.multiple_of(tok * SUBLANES, SUBLANES), SUBLANES), :],
            ssem.at[s])

    def start_rows(make, toks, s, count):
        def body8(c, carry):
            for u in range(GATHER_UNROLL):
                make(toks, s, c * GATHER_UNROLL + u).start()
            return carry

        def body1(r, carry):
            make(toks, s, r).start()
            return carry

        full = count // GATHER_UNROLL
        lax.fori_loop(0, full, body8, 0)
        lax.fori_loop(full * GATHER_UNROLL, count, body1, 0)

    def wait_gather(tile, s):
        n = pl.multiple_of(valid_ref[tile] * SUBLANES, SUBLANES)
        pltpu.make_async_copy(h_hbm.at[pl.ds(0, n), :], xbuf.at[s, pl.ds(0, n), :], gsem.at[s]).wait()

    def wait_scatter(tile, s):
        n = pl.multiple_of(valid_ref[tile] * SUBLANES, SUBLANES)
        pltpu.make_async_copy(ybuf.at[s, pl.ds(0, n), :], m_hbm.at[pl.ds(0, n), :], ssem.at[s]).wait()

    @pl.when(i == 0)
    def _():
        xbuf[...] = jnp.zeros(xbuf.shape, F32)
        start_rows(row_copy_in, tok_ref, 0, valid_ref[0])

    @pl.when(i + 1 < n_used)
    def _():
        start_rows(row_copy_in, tok_next_ref, 1 - slot, valid_ref[i + 1])

    def refresh(which, ids_ref, up_ref, dn_ref):
        prev = ids_ref[jnp.maximum(i - 1, 0)]

        @pl.when((i == 0) | (ids_ref[i] != prev))
        def _():
            def cast_up(c, carry):
                r = pl.multiple_of(c * CAST_ROWS, CAST_ROWS)
                w_up[which, pl.ds(r, CAST_ROWS), :] = up_ref[0, pl.ds(r, CAST_ROWS), :].astype(BF16)
                return carry

            def cast_dn(c, carry):
                r = pl.multiple_of(c * CAST_ROWS, CAST_ROWS)
                w_dn[which, pl.ds(r, CAST_ROWS), :] = dn_ref[0, pl.ds(r, CAST_ROWS), :].astype(BF16)
                return carry

            lax.fori_loop(0, up_ref.shape[1] // CAST_ROWS, cast_up, 0)
            lax.fori_loop(0, dn_ref.shape[1] // CAST_ROWS, cast_dn, 0)

    @pl.when(i < n_used)
    def _():
        refresh(0, ta_ref, upa_ref, dna_ref)
        refresh(1, tb_ref, upb_ref, dnb_ref)
        wait_gather(i, slot)
        x = _load_token_major(xbuf.at[slot], MOE_TILE, upa_ref.shape[1]).astype(BF16)

        def ffn(which):
            gu = jnp.dot(x, w_up[which], preferred_element_type=F32)
            hid = jax.nn.silu(gu[:, :D_EXPERT]) * gu[:, D_EXPERT:]
            return jnp.dot(hid.astype(BF16), w_dn[which], preferred_element_type=F32)

        y = ffn(0) * wa_ref[...] + ffn(1) * wb_ref[...]

        @pl.when(i >= 2)
        def _():
            wait_scatter(i - 2, slot)

        _store_token_major(ybuf.at[slot], y)
        start_rows(row_copy_out, tok_ref, slot, valid_ref[i])

        @pl.when(i == n_used - 1)
        def _():
            @pl.when(i >= 1)
            def _():
                wait_scatter(i - 1, 1 - slot)
            wait_scatter(i, slot)


def _moe_call(layer, tile_a, tile_b, n_used, tile_valid, slot_tok, h1_tm, slot_wa, slot_wb, w_up, w_down,
              n_tiles, t):
    d = w_up.shape[1]
    e0 = layer * N_EXPERTS
    wspec = pl.BlockSpec((MOE_TILE, 1), lambda i, *_: (i, 0))
    tok_blk = (1, 1, MOE_TILE)
    grid_spec = pltpu.PrefetchScalarGridSpec(
        num_scalar_prefetch=4,
        grid=(n_tiles,),
        in_specs=[pl.BlockSpec(memory_space=pl.ANY),
                  pl.BlockSpec(tok_blk, lambda i, *_: (i, 0, 0), memory_space=pltpu.SMEM),
                  pl.BlockSpec(tok_blk, lambda i, *_: (jnp.minimum(i + 1, n_tiles - 1), 0, 0),
                               memory_space=pltpu.SMEM),
                  wspec, wspec,
                  pl.BlockSpec((1, d, 2 * D_EXPERT), lambda i, ta, tb, *_: (e0 + ta[i], 0, 0)),
                  pl.BlockSpec((1, d, 2 * D_EXPERT), lambda i, ta, tb, *_: (e0 + tb[i], 0, 0)),
                  pl.BlockSpec((1, D_EXPERT, d), lambda i, ta, tb, *_: (e0 + ta[i], 0, 0)),
                  pl.BlockSpec((1, D_EXPERT, d), lambda i, ta, tb, *_: (e0 + tb[i], 0, 0))],
        out_specs=pl.BlockSpec(memory_space=pl.ANY),
        scratch_shapes=[pltpu.VMEM((2, MOE_TILE * SUBLANES, LANES), F32),
                        pltpu.VMEM((2, MOE_TILE * SUBLANES, LANES), F32),
                        pltpu.VMEM((2, d, 2 * D_EXPERT), BF16), pltpu.VMEM((2, D_EXPERT, d), BF16),
                        pltpu.SemaphoreType.DMA((2,)), pltpu.SemaphoreType.DMA((2,))])
    return pl.pallas_call(
        _moe_kernel,
        grid_spec=grid_spec,
        out_shape=jax.ShapeDtypeStruct((t * SUBLANES, LANES), F32),
        compiler_params=_cparams(("arbitrary",)),
        name="moe",
    )(tile_a, tile_b, n_used, tile_valid, h1_tm, slot_tok, slot_tok, slot_wa, slot_wb, w_up, w_up, w_down, w_down)


def _final_kernel(alpha, h1_ref, m_ref, e_ref, g_ref, b_ref, o_ref):
    rows, d = o_ref.shape
    h1 = _load_token_major(h1_ref, rows, d)
    m = _load_token_major(m_ref, rows, d)
    o_ref[...] = _layer_norm(alpha * h1 + m + e_ref[...].astype(F32), g_ref[...], b_ref[...])


def _final_call(alpha, h1_tm, m_tm, e, g, b):
    t, d = e.shape
    tm = TM_FINAL
    row = lambda i: (i, 0)
    tmaj = pl.BlockSpec((tm * SUBLANES, LANES), row)
    return pl.pallas_call(
        functools.partial(_final_kernel, alpha),
        grid=(t // tm,),
        in_specs=[tmaj, tmaj, pl.BlockSpec((tm, d), row), _full(g.shape), _full(b.shape)],
        out_specs=pl.BlockSpec((tm, d), row),
        out_shape=jax.ShapeDtypeStruct((t, d), F32),
        compiler_params=_cparams(("arbitrary",)),
        name="final_ln",
    )(h1_tm, m_tm, e, g, b)


def _pack_w_in(w_in):
    off_q = OFF_CONV + 2 * D_CONV
    off_kr = off_q + Q_LORA + KV_LORA
    off_gate = off_kr + QK_ROPE
    kr = jnp.pad(w_in[..., off_kr:off_gate], ((0, 0), (0, 0), (ROPE_LANE0, HEAD_BLOCK - ROPE_LANE0 - QK_ROPE)))
    return jnp.concatenate([w_in[..., :off_kr], kr, w_in[..., off_gate:]], axis=-1).astype(BF16)


def _pack_heads(w, lo, width, stride, lane0):
    blocks = []
    for h in range(N_HEADS):
        l0 = lane0(h)
        blocks.append(jnp.pad(w[..., h * stride + lo:h * stride + lo + width],
                              ((0, 0), (0, 0), (l0, HEAD_BLOCK - l0 - width))))
    return jnp.concatenate(blocks, axis=-1).astype(BF16)


def _rope_tables(positions):
    inv_freq = jnp.power(ROPE_THETA, -jnp.arange(0, QK_ROPE, 2, dtype=F32) / QK_ROPE)
    ang = positions.astype(F32).reshape(-1, 1) * inv_freq
    cos, sin = jnp.cos(ang), jnp.sin(ang)
    t = ang.shape[0]
    ones_lo = jnp.ones((t, ROPE_LANE0), F32)
    ones_hi = jnp.ones((t, HEAD_BLOCK - ROPE_LANE0 - QK_ROPE), F32)
    zeros_lo = jnp.zeros((t, ROPE_LANE0), F32)
    zeros_half = jnp.zeros((t, HALF_ROPE), F32)
    zeros_hi = jnp.zeros((t, HEAD_BLOCK - ROPE_LANE0 - QK_ROPE), F32)
    c = jnp.concatenate([ones_lo, cos, cos, ones_hi], axis=-1)
    s1 = jnp.concatenate([zeros_lo, -sin, zeros_half, zeros_hi], axis=-1)
    s2 = jnp.concatenate([zeros_lo, zeros_half, sin, zeros_hi], axis=-1)
    return c, s1, s2


def _routing_plan(cls, wa, wb, n_tiles):
    t = cls.shape[0]
    n_pad = n_tiles * MOE_TILE - t
    assert n_pad == N_CLASSES * MOE_TILE, n_pad
    classes = jnp.arange(N_CLASSES, dtype=jnp.int32)
    counts = jnp.sum((cls[None, :] == classes[:, None]).astype(jnp.int32), axis=1)
    padded = (counts + MOE_TILE - 1) // MOE_TILE * MOE_TILE
    pad_end = jnp.cumsum(padded)
    pad_start = pad_end - padded
    pad_need = padded - counts
    j = jnp.arange(MOE_TILE, dtype=jnp.int32)[None, :]
    pad_key = jnp.where(j < pad_need[:, None], 2 * classes[:, None] + 1, 2 * N_CLASSES).reshape(-1)
    zeros_i = jnp.zeros((n_pad,), jnp.int32)
    zeros_f = jnp.zeros((n_pad,), F32)
    _, slot_tok, slot_wa, slot_wb = lax.sort(
        (jnp.concatenate([2 * cls, pad_key]), jnp.concatenate([jnp.arange(t, dtype=jnp.int32), zeros_i]),
         jnp.concatenate([wa, zeros_f]), jnp.concatenate([wb, zeros_f])), num_keys=1)
    slot_tok = slot_tok.reshape(n_tiles, 1, MOE_TILE)
    slot_wa = slot_wa.reshape(-1, 1)
    slot_wb = slot_wb.reshape(-1, 1)
    tile_row0 = jnp.arange(n_tiles, dtype=jnp.int32) * MOE_TILE
    tile_cls = jnp.minimum(jnp.sum((tile_row0[:, None] >= pad_end[None, :]).astype(jnp.int32), axis=1),
                           N_CLASSES - 1)
    onehot = (tile_cls[:, None] == classes[None, :]).astype(jnp.int32)
    tile_valid = jnp.clip(jnp.sum(onehot * (pad_start + counts)[None, :], axis=1) - tile_row0,
                          0, MOE_TILE).astype(jnp.int32)
    pair_lo = jnp.array([p[0] for p in PAIRS], jnp.int32)
    pair_hi = jnp.array([p[1] for p in PAIRS], jnp.int32)
    grp = tile_cls // len(PAIRS)
    pair_onehot = ((tile_cls % len(PAIRS))[:, None] == jnp.arange(len(PAIRS), dtype=jnp.int32)[None, :])
    tile_a = grp * EXPERTS_PER_GROUP + jnp.sum(pair_onehot * pair_lo[None, :], axis=1)
    tile_b = grp * EXPERTS_PER_GROUP + jnp.sum(pair_onehot * pair_hi[None, :], axis=1)
    n_used = (pad_end[-1] // MOE_TILE).astype(jnp.int32).reshape(1)
    return tile_a, tile_b, n_used, tile_valid, slot_tok, slot_wa, slot_wb


def kernel(x, p, positions, ln_in_g, ln_in_b, w_in, b_gate, pool_w, pool_scale, pool_proj, conv_dw, conv_b, conv_ln_g, conv_ln_b, conv_proj, q_norm_g, w_uq, kv_norm_g, w_ukv, mla_proj, w_out, ln1_g, ln1_b, w_router, router_bias, exp_w_up, exp_w_down, ple_proj, ple_gate, ln2_g, ln2_b):
    batch, seq, d = x.shape
    depth = w_in.shape[0]
    t = batch * seq
    alpha = float((2 * depth) ** 0.25)
    n_tiles = (t + N_CLASSES * (MOE_TILE - 1)) // MOE_TILE + 1

    w_in_p = _pack_w_in(w_in)
    pool_w_b = pool_w.astype(BF16)
    pool_proj_b, conv_proj_b, mla_proj_b = pool_proj.astype(BF16), conv_proj.astype(BF16), mla_proj.astype(BF16)
    qk = QK_NOPE + QK_ROPE
    wq_p = (_pack_heads(w_uq, 0, QK_NOPE, qk, lambda h: 0)
            + _pack_heads(w_uq, QK_NOPE, QK_ROPE, qk, lambda h: ROPE_LANE0))
    wqt_p = jnp.swapaxes(wq_p, 1, 2)
    wk_p = _pack_heads(w_ukv, 0, QK_NOPE, QK_NOPE + V_HEAD, lambda h: 0)
    kv_w = QK_NOPE + V_HEAD
    wvt_p = jnp.swapaxes(jnp.concatenate(
        [jnp.pad(w_ukv[..., h * kv_w + QK_NOPE:(h + 1) * kv_w], ((0, 0), (0, 0), (0, BF16_ROWS)))
         for h in range(N_HEADS)], axis=-1), 1, 2).astype(BF16)
    ones_rows = jnp.tile(jnp.concatenate([jnp.zeros((V_HEAD, 1), F32), jnp.ones((BF16_ROWS, 1), F32)]),
                         (N_HEADS, 1))
    w_out_b, ple_proj_b, ple_gate_b = w_out.astype(BF16), ple_proj.astype(BF16), ple_gate.astype(BF16)
    w_up_b = exp_w_up.reshape((depth * N_EXPERTS,) + exp_w_up.shape[2:])
    w_down_b = exp_w_down.reshape((depth * N_EXPERTS,) + exp_w_down.shape[2:])
    wr_t = w_router.T
    wr_hi = wr_t.astype(BF16)
    wr_lo = (wr_t - wr_hi.astype(F32)).astype(BF16)
    rope_tabs = _rope_tables(positions)
    rope_tabs_t = tuple(a.T for a in rope_tabs)
    p2 = p.reshape(depth, t, -1)

    pre, pre_g, pre_b = (x.reshape(t, d),), ln_in_g, ln_in_b
    for i in range(depth):
        branch_params = (conv_dw[i], conv_b[i].reshape(1, -1), conv_ln_g[i].reshape(1, -1),
                         conv_ln_b[i].reshape(1, -1), pool_w_b[i], pool_scale[i].reshape(1, -1))
        mla_params = (rope_tabs, rope_tabs_t, q_norm_g[i].reshape(1, -1), kv_norm_g[i].reshape(1, -1),
                      wqt_p, wk_p, wvt_p, ones_rows)
        h, a_pool, a_conv, qt, k, vt, glog = _in_proj_call(alpha, pre, pre_g.reshape(1, d), pre_b.reshape(1, d),
                                                           w_in_p, branch_params, mla_params, i, seq)
        a_attn = _attn_call(qt, k, vt, batch, seq)
        h1_tm, e, logits = _merge_call(alpha, i, a_pool, a_conv, a_attn, glog, h, p2,
                                       pool_proj_b, conv_proj_b, mla_proj_b, b_gate[i], w_out_b,
                                       ln1_g[i].reshape(1, d), ln1_b[i].reshape(1, d),
                                       ple_proj_b, ple_gate_b, wr_hi, wr_lo)
        cls, wa, wb = _route_call(logits.reshape(N_EXPERTS, t // LANES, LANES), router_bias)
        plan = _routing_plan(cls.reshape(t), wa.reshape(t), wb.reshape(t), n_tiles)
        m_tm = _moe_call(i, *plan[:5], h1_tm, plan[5], plan[6], w_up_b, w_down_b, n_tiles, t)
        pre, pre_g, pre_b = (h1_tm, m_tm, e), ln2_g[i], ln2_b[i]
    h = _final_call(alpha, *pre, pre_g.reshape(1, d), pre_b.reshape(1, d))
    return h.reshape(batch, seq, d)
```
